```python
import math
import jax, jax.numpy as jnp
from jax import lax
import numpy as np

D_MODEL = 2048
BATCH = 1
SEQ = 8192
DEPTH = 4

N_A_LAYERS = DEPTH // 2
N_B_LAYERS = DEPTH - N_A_LAYERS

HEAD_DIM = 128
D_FF = 4 * D_MODEL
LRU_WIDTH = 3 * D_MODEL // 4
LRU_BLOCK_WIDTH = 128
LRU_BLOCKS = LRU_WIDTH // LRU_BLOCK_WIDTH
CONV_WIDTH = 4
RG_C = 8.0
MEM_TOKENS = 256
MEM_HEADS = 4
MEM_WIDTH = MEM_HEADS * HEAD_DIM
DIL_PATTERNS = ((128, 1), (512, 4), (2048, 16))
DIL_GROUPS = len(DIL_PATTERNS)
DIL_HEADS = (D_MODEL - MEM_WIDTH) // (DIL_GROUPS * HEAD_DIM)
DIL_WIDTH = DIL_GROUPS * DIL_HEADS * HEAD_DIM
DIL_OUT_WIDTH = DIL_HEADS * HEAD_DIM
Q_BLOCK = 128
REL_BUCKETS = 32
REL_MAX_EXACT = REL_BUCKETS // 2
REL_MAX_DISTANCE = 2048
A_IN_WIDTH = 2 * LRU_WIDTH + MEM_WIDTH
A_OUT_WIDTH = LRU_WIDTH + MEM_WIDTH
B_IN_WIDTH = DIL_WIDTH + MEM_WIDTH
B_OUT_WIDTH = DIL_OUT_WIDTH + MEM_WIDTH
NORM_EPS = 1e-6
NEG_INF = -1e30

kernel_name = "hawk_yoco_dilated_hybrid"


def rms_norm(x, g):
    xf = x.astype(jnp.float32)
    y = xf * lax.rsqrt(jnp.mean(xf * xf, axis=-1, keepdims=True) + NORM_EPS)
    return (y * g.astype(jnp.float32)).astype(x.dtype)


def head_norm(x, g):
    xf = x.astype(jnp.float32)
    y = xf * lax.rsqrt(jnp.mean(xf * xf, axis=-1, keepdims=True) + NORM_EPS)
    return y * g.astype(jnp.float32)


def t5_bucket(distance):
    n = jnp.maximum(distance, 0)
    small = n < REL_MAX_EXACT
    nf = jnp.maximum(n, 1).astype(jnp.float32)
    large = REL_MAX_EXACT + (jnp.log(nf / REL_MAX_EXACT)
                             / math.log(REL_MAX_DISTANCE / REL_MAX_EXACT)
                             * (REL_BUCKETS - REL_MAX_EXACT)).astype(jnp.int32)
    large = jnp.minimum(large, REL_BUCKETS - 1)
    return jnp.where(small, n, large)


def rglru_branch(u, conv_w, conv_b, gr_w, gr_b, gi_w, gi_b, lam):
    B, S, W = u.shape
    upad = jnp.pad(u, ((0, 0), (CONV_WIDTH - 1, 0), (0, 0)))
    xc = conv_b + sum(upad[:, k:k + S] * conv_w[k] for k in range(CONV_WIDTH))
    xf = xc.astype(jnp.float32)
    xb = xf.reshape(B, S, LRU_BLOCKS, LRU_BLOCK_WIDTH)
    r = jax.nn.sigmoid(jnp.einsum('bsnc,ncd->bsnd', xb, gr_w.astype(jnp.float32))
                       + gr_b.astype(jnp.float32)).reshape(B, S, W)
    i = jax.nn.sigmoid(jnp.einsum('bsnc,ncd->bsnd', xb, gi_w.astype(jnp.float32))
                       + gi_b.astype(jnp.float32)).reshape(B, S, W)
    log_a = -RG_C * r * jax.nn.softplus(-lam.astype(jnp.float32))
    a = jnp.exp(log_a)
    b = jnp.sqrt(-jnp.expm1(2.0 * log_a)) * (i * xf)

    def combine(left, right):
        a_l, b_l = left
        a_r, b_r = right
        return a_l * a_r, a_r * b_l + b_r

    _, h = lax.associative_scan(combine, (a, b), axis=1)
    return h.astype(u.dtype)


def memory_kv(mem, g, w_kv, k_g):
    B, M, _ = mem.shape
    kv = rms_norm(mem, g) @ w_kv
    k = head_norm(kv[..., :MEM_WIDTH].reshape(B, M, MEM_HEADS, HEAD_DIM), k_g)
    v = kv[..., MEM_WIDTH:].reshape(B, M, MEM_HEADS, HEAD_DIM).astype(jnp.float32)
    return k, v


def memory_attention(q, k, v):
    logits = jnp.einsum('bshd,bmhd->bhsm', q, k)
    p = jax.nn.softmax(logits, axis=-1)
    return jnp.einsum('bhsm,bmhd->bshd', p, v)


def shared_kv(x, g, w_kv, k_g):
    B, S, _ = x.shape
    kv = rms_norm(x, g) @ w_kv
    k = kv[..., :DIL_WIDTH].reshape(B, S, DIL_GROUPS, DIL_HEADS, HEAD_DIM)
    v = kv[..., DIL_WIDTH:].reshape(B, S, DIL_GROUPS, DIL_HEADS, HEAD_DIM)
    k = head_norm(k, k_g[:, None, :])
    return k, v.astype(jnp.float32)


def dilated_window_attention(q, k, v, bias_table, window, dilation):
    B, S, H, hd = q.shape
    L = S // dilation
    n_back = window // dilation
    assert n_back <= Q_BLOCK
    nb = -(-L // Q_BLOCK)
    Lp = nb * Q_BLOCK

    def to_blocks(t):
        t = t.reshape(B, L, dilation, H, hd).transpose(0, 2, 1, 3, 4)
        t = jnp.pad(t, ((0, 0), (0, 0), (0, Lp - L), (0, 0), (0, 0)))
        return t.reshape(B, dilation, nb, Q_BLOCK, H, hd)

    def with_prev(t):
        prev = jnp.pad(t, ((0, 0), (0, 0), (1, 0), (0, 0), (0, 0), (0, 0)))[:, :, :-1]
        return jnp.concatenate([prev, t], axis=3)

    qb = to_blocks(q)
    kk = with_prev(to_blocks(k))
    vv = with_prev(to_blocks(v))
    logits = jnp.einsum('brnqhd,brnkhd->brnhqk', qb, kk)

    qi = jnp.arange(Q_BLOCK)[:, None]
    ki = jnp.arange(2 * Q_BLOCK)[None, :]
    u = qi + Q_BLOCK - ki
    bias = bias_table[t5_bucket(u * dilation)].transpose(2, 0, 1).astype(jnp.float32)
    kpos = jnp.arange(nb)[:, None] * Q_BLOCK + ki - Q_BLOCK
    valid = ((u >= 0) & (u <= n_back))[None] & (kpos >= 0)[:, None, :]
    logits = jnp.where(valid[None, None, :, None], logits + bias, NEG_INF)

    m = jnp.max(logits, axis=-1, keepdims=True)
    p = jnp.exp(logits - m)
    den = jnp.sum(p, axis=-1)
    o = jnp.einsum('brnhqk,brnkhd->brnqhd', p, vv) / den.transpose(0, 1, 2, 4, 3)[..., None]
    lse = (m[..., 0] + jnp.log(den)).transpose(0, 1, 2, 4, 3)

    o = o.reshape(B, dilation, Lp, H, hd)[:, :, :L].transpose(0, 2, 1, 3, 4).reshape(B, S, H, hd)
    lse = lse.reshape(B, dilation, Lp, H)[:, :, :L].transpose(0, 2, 1, 3).reshape(B, S, H)
    return o, lse


def dilated_mixture(q, k, v, rel_bias):
    outs, lses = [], []
    for g, (window, dilation) in enumerate(DIL_PATTERNS):
        o, l = dilated_window_attention(q[:, :, g], k[:, :, g], v[:, :, g],
                                        rel_bias[:, g * DIL_HEADS:(g + 1) * DIL_HEADS],
                                        window, dilation)
        outs.append(o)
        lses.append(l)
    w = jax.nn.softmax(jnp.stack(lses, axis=0), axis=0)
    return jnp.sum(w[..., None] * jnp.stack(outs, axis=0), axis=0)


def setup_inputs(seed: int = 0) -> dict:
    key = jax.random.key(seed)
    ks = jax.random.split(key, 26)
    f32 = jnp.float32

    def normal(k, shape, scale):
        return jax.random.normal(k, shape, f32) * scale

    def gain(k, shape):
        return 1.0 + 0.02 * jax.random.normal(k, shape, f32)

    a_pow = jax.random.uniform(ks[17], (N_A_LAYERS, LRU_WIDTH), f32, 0.9, 0.999)
    a_base = a_pow ** (1.0 / RG_C)
    a_lambda = jnp.log(a_base) - jnp.log1p(-a_base)
    return {
        'x': normal(ks[0], (BATCH, SEQ, D_MODEL), 1.0),
        'mem': normal(ks[1], (BATCH, MEM_TOKENS, D_MODEL), 1.0),
        'norm_mix_g': gain(ks[2], (DEPTH, D_MODEL)),
        'norm_mlp_g': gain(ks[3], (DEPTH, D_MODEL)),
        'mlp_w1': normal(ks[4], (DEPTH, D_MODEL, D_FF), D_MODEL ** -0.5),
        'mlp_w2': normal(ks[5], (DEPTH, D_FF, D_MODEL), 0.5 * D_FF ** -0.5),
        'mem_norm_g': gain(ks[6], (DEPTH, D_MODEL)),
        'mem_w_kv': normal(ks[7], (DEPTH, D_MODEL, 2 * MEM_WIDTH), D_MODEL ** -0.5),
        'mem_q_norm_g': gain(ks[8], (DEPTH, HEAD_DIM)),
        'mem_k_norm_g': gain(ks[9], (DEPTH, HEAD_DIM)),
        'a_w_in': normal(ks[10], (N_A_LAYERS, D_MODEL, A_IN_WIDTH), D_MODEL ** -0.5),
        'a_conv_w': normal(ks[11], (N_A_LAYERS, CONV_WIDTH, LRU_WIDTH), CONV_WIDTH ** -0.5),
        'a_conv_b': normal(ks[12], (N_A_LAYERS, LRU_WIDTH), 0.01),
        'a_gate_r_w': normal(ks[13], (N_A_LAYERS, LRU_BLOCKS, LRU_BLOCK_WIDTH, LRU_BLOCK_WIDTH), LRU_BLOCK_WIDTH ** -0.5),
        'a_gate_r_b': normal(ks[14], (N_A_LAYERS, LRU_BLOCKS, LRU_BLOCK_WIDTH), 0.01),
        'a_gate_i_w': normal(ks[15], (N_A_LAYERS, LRU_BLOCKS, LRU_BLOCK_WIDTH, LRU_BLOCK_WIDTH), LRU_BLOCK_WIDTH ** -0.5),
        'a_gate_i_b': normal(ks[16], (N_A_LAYERS, LRU_BLOCKS, LRU_BLOCK_WIDTH), 0.01),
        'a_lambda': a_lambda,
        'a_w_out': normal(ks[18], (N_A_LAYERS, A_OUT_WIDTH, D_MODEL), A_OUT_WIDTH ** -0.5),
        'kv_norm_g': gain(ks[19], (D_MODEL,)),
        'kv_w': normal(ks[20], (D_MODEL, 2 * DIL_WIDTH), D_MODEL ** -0.5),
        'k_norm_g': gain(ks[21], (DIL_GROUPS, HEAD_DIM)),
        'rel_bias': normal(ks[22], (REL_BUCKETS, DIL_GROUPS * DIL_HEADS), 0.3),
        'b_w_q': normal(ks[23], (N_B_LAYERS, D_MODEL, B_IN_WIDTH), D_MODEL ** -0.5),
        'b_q_norm_g': gain(ks[24], (N_B_LAYERS, DIL_GROUPS, HEAD_DIM)),
        'b_w_out': normal(ks[25], (N_B_LAYERS, B_OUT_WIDTH, D_MODEL), B_OUT_WIDTH ** -0.5),
    }


def reference(x, mem, norm_mix_g, norm_mlp_g, mlp_w1, mlp_w2, mem_norm_g, mem_w_kv,
              mem_q_norm_g, mem_k_norm_g, a_w_in, a_conv_w, a_conv_b, a_gate_r_w, a_gate_r_b,
              a_gate_i_w, a_gate_i_b, a_lambda, a_w_out, kv_norm_g, kv_w, k_norm_g, rel_bias,
              b_w_q, b_q_norm_g, b_w_out):
    B, S, _ = x.shape
    scale = HEAD_DIM ** -0.5
    k_shared = None
    v_shared = None
    for layer in range(DEPTH):
        h = rms_norm(x, norm_mix_g[layer])
        mem_k, mem_v = memory_kv(mem, mem_norm_g[layer], mem_w_kv[layer], mem_k_norm_g[layer])
        if layer < N_A_LAYERS:
            proj = h @ a_w_in[layer]
        else:
            proj = h @ b_w_q[layer - N_A_LAYERS]
        mq = proj[..., -MEM_WIDTH:].reshape(B, S, MEM_HEADS, HEAD_DIM)
        mq = head_norm(mq, mem_q_norm_g[layer]) * scale
        mem_out = memory_attention(mq, mem_k, mem_v).reshape(B, S, MEM_WIDTH).astype(x.dtype)

        if layer < N_A_LAYERS:
            i = layer
            lru = rglru_branch(proj[..., :LRU_WIDTH], a_conv_w[i], a_conv_b[i], a_gate_r_w[i],
                               a_gate_r_b[i], a_gate_i_w[i], a_gate_i_b[i], a_lambda[i])
            gate = jax.nn.gelu(proj[..., LRU_WIDTH:2 * LRU_WIDTH])
            mixed = jnp.concatenate([lru * gate, mem_out], axis=-1) @ a_w_out[i]
        else:
            j = layer - N_A_LAYERS
            if j == 0:
                k_shared, v_shared = shared_kv(x, kv_norm_g, kv_w, k_norm_g)
            dq = proj[..., :DIL_WIDTH].reshape(B, S, DIL_GROUPS, DIL_HEADS, HEAD_DIM)
            dq = head_norm(dq, b_q_norm_g[j][:, None, :]) * scale
            dil = dilated_mixture(dq, k_shared, v_shared, rel_bias)
            dil = dil.reshape(B, S, DIL_OUT_WIDTH).astype(x.dtype)
            mixed = jnp.concatenate([dil, mem_out], axis=-1) @ b_w_out[j]
        x = x + mixed

        hm = rms_norm(x, norm_mlp_g[layer])
        x = x + jnp.square(jax.nn.relu(hm @ mlp_w1[layer])) @ mlp_w2[layer]
    return x
```

```python
import functools
import math

import numpy as np
import jax
import jax.numpy as jnp
from jax import lax
from jax.experimental import pallas as pl
from jax.experimental.pallas import tpu as pltpu

F32 = jnp.float32
BF16 = jnp.bfloat16

D_MODEL = 2048
SEQ = 8192
DEPTH = 4
N_A_LAYERS = DEPTH // 2
HEAD_DIM = 128
D_FF = 4 * D_MODEL
LRU_WIDTH = 3 * D_MODEL // 4
LRU_BLOCK_WIDTH = 128
LRU_BLOCKS = LRU_WIDTH // LRU_BLOCK_WIDTH
CONV_WIDTH = 4
RG_C = 8.0
MEM_TOKENS = 256
MEM_HEADS = 4
MEM_WIDTH = MEM_HEADS * HEAD_DIM
DIL_PATTERNS = ((128, 1), (512, 4), (2048, 16))
DIL_GROUPS = len(DIL_PATTERNS)
DIL_HEADS = 4
DIL_WIDTH = DIL_GROUPS * DIL_HEADS * HEAD_DIM
DIL_OUT_WIDTH = DIL_HEADS * HEAD_DIM
Q_BLOCK = 128
REL_BUCKETS = 32
REL_MAX_EXACT = REL_BUCKETS // 2
REL_MAX_DISTANCE = 2048
A_IN_WIDTH = 2 * LRU_WIDTH + MEM_WIDTH
NORM_EPS = 1e-6
NEG_INF = -1e30
ATTN_SCALE = HEAD_DIM ** -0.5

V7X_VMEM_BYTES = 64 * 1024 * 1024
LANES = 128
SUBLANES = 8
HEAD_BLOCK = DIL_HEADS * HEAD_DIM


def _params(semantics, vmem_mib):
    assert vmem_mib * 1024 * 1024 < V7X_VMEM_BYTES
    return pltpu.CompilerParams(dimension_semantics=semantics,
                                vmem_limit_bytes=vmem_mib * 1024 * 1024)


def _rms_rows(x, g):
    ms = jnp.mean(x * x, axis=-1, keepdims=True)
    return x * lax.rsqrt(ms + NORM_EPS) * g


def _head_norm_cols(a, g, scale):
    parts = []
    for h in range(a.shape[1] // HEAD_DIM):
        ah = a[:, h * HEAD_DIM:(h + 1) * HEAD_DIM]
        y = ah * lax.rsqrt(jnp.mean(ah * ah, axis=-1, keepdims=True) + NORM_EPS) * g
        parts.append(y * scale if scale != 1.0 else y)
    return jnp.concatenate(parts, axis=-1)


def _norm_matmul_kernel(x_ref, g_ref, w_ref, hg_ref, o_ref, hn_ref, *, norm_lo, norm_hi, nblk, scale):
    j = pl.program_id(1)

    @pl.when(j == 0)
    def _():
        hn_ref[...] = _rms_rows(x_ref[...], g_ref[...]).astype(BF16)

    acc = jnp.dot(hn_ref[...], w_ref[...], preferred_element_type=F32)

    def store_normed():
        o_ref[...] = _head_norm_cols(acc, hg_ref[...], scale).astype(o_ref.dtype)

    def store_raw():
        o_ref[...] = acc.astype(o_ref.dtype)

    if norm_lo == 0 and norm_hi == nblk:
        store_normed()
    elif norm_lo == norm_hi:
        store_raw()
    else:
        is_normed = jnp.logical_and(j >= norm_lo, j < norm_hi)
        pl.when(is_normed)(store_normed)
        pl.when(jnp.logical_not(is_normed))(store_raw)


def _norm_matmul(x, g, w, head_gains, norm_lo, norm_hi, scale, out_dtype, name, tm=1024):
    S, D = x.shape
    N = w.shape[1]
    tn = HEAD_BLOCK
    nblk = N // tn
    kern = functools.partial(_norm_matmul_kernel, norm_lo=norm_lo, norm_hi=norm_hi, nblk=nblk, scale=scale)
    return pl.pallas_call(
        kern,
        grid=(S // tm, nblk),
        in_specs=[
            pl.BlockSpec((tm, D), lambda i, j: (i, 0)),
            pl.BlockSpec((1, D), lambda i, j: (0, 0)),
            pl.BlockSpec((D, tn), lambda i, j: (0, j)),
            pl.BlockSpec((None, 1, HEAD_DIM), lambda i, j: (j, 0, 0)),
        ],
        out_specs=pl.BlockSpec((tm, tn), lambda i, j: (i, j)),
        out_shape=jax.ShapeDtypeStruct((S, N), out_dtype),
        scratch_shapes=[pltpu.VMEM((tm, D), BF16)],
        compiler_params=_params(("parallel", "arbitrary"), 48),
        name=name,
    )(x, g.reshape(1, D), w, head_gains.reshape(nblk, 1, HEAD_DIM))


def _mlp_kernel(x_ref, g_ref, w1_ref, w2_ref, o_ref, hn_ref):
    j = pl.program_id(1)

    @pl.when(j == 0)
    def _():
        x = x_ref[...]
        hn_ref[...] = _rms_rows(x, g_ref[...]).astype(BF16)
        o_ref[...] = x

    h = jnp.dot(hn_ref[...], w1_ref[...], preferred_element_type=F32)
    h = jnp.maximum(h, 0.0)
    h = (h * h).astype(BF16)
    o_ref[...] += jnp.dot(h, w2_ref[...], preferred_element_type=F32)


def _mlp(x, g, w1, w2, name, tm=1024, tf=512):
    S, D = x.shape
    F = w1.shape[1]
    return pl.pallas_call(
        _mlp_kernel,
        grid=(S // tm, F // tf),
        in_specs=[
            pl.BlockSpec((tm, D), lambda i, j: (i, 0)),
            pl.BlockSpec((1, D), lambda i, j: (0, 0)),
            pl.BlockSpec((D, tf), lambda i, j: (0, j)),
            pl.BlockSpec((tf, D), lambda i, j: (j, 0)),
        ],
        out_specs=pl.BlockSpec((tm, D), lambda i, j: (i, 0)),
        out_shape=jax.ShapeDtypeStruct((S, D), F32),
        scratch_shapes=[pltpu.VMEM((tm, D), BF16)],
        compiler_params=_params(("parallel", "arbitrary"), 56),
        name=name,
    )(x, g.reshape(1, D), w1, w2)


def _mem_kv_kernel(mem_ref, g_ref, w_ref, kg_ref, k_ref, v_ref):
    hn = _rms_rows(mem_ref[...], g_ref[...]).astype(BF16)
    kv = jnp.dot(hn, w_ref[...].astype(BF16), preferred_element_type=F32)
    k_ref[...] = _head_norm_cols(kv[:, :MEM_WIDTH], kg_ref[...], 1.0).astype(BF16)
    v_ref[...] = kv[:, MEM_WIDTH:].astype(BF16)


def _mem_kv(mem, mem_norm_g, mem_w_kv, mem_k_norm_g):
    M, D = mem.shape
    out = jax.ShapeDtypeStruct((DEPTH, M, MEM_WIDTH), BF16)
    return pl.pallas_call(
        _mem_kv_kernel,
        grid=(DEPTH,),
        in_specs=[
            pl.BlockSpec((M, D), lambda l: (0, 0)),
            pl.BlockSpec((None, 1, D), lambda l: (l, 0, 0)),
            pl.BlockSpec((None, D, 2 * MEM_WIDTH), lambda l: (l, 0, 0)),
            pl.BlockSpec((None, 1, HEAD_DIM), lambda l: (l, 0, 0)),
        ],
        out_specs=[pl.BlockSpec((None, M, MEM_WIDTH), lambda l: (l, 0, 0)),
                   pl.BlockSpec((None, M, MEM_WIDTH), lambda l: (l, 0, 0))],
        out_shape=[out, out],
        compiler_params=_params(("arbitrary",), 40),
        name="mem_kv",
    )(mem, mem_norm_g.reshape(DEPTH, 1, D), mem_w_kv, mem_k_norm_g.reshape(DEPTH, 1, HEAD_DIM))


def _mem_attn_kernel(q_ref, k_ref, v_ref, o_ref):
    for h in range(MEM_HEADS):
        hs = slice(h * HEAD_DIM, (h + 1) * HEAD_DIM)
        q = q_ref[:, hs].astype(BF16)
        s = lax.dot_general(q, k_ref[:, hs], (((1,), (1,)), ((), ())), preferred_element_type=F32)
        m = jnp.max(s, axis=-1, keepdims=True)
        p = jnp.exp(s - m)
        den = jnp.sum(p, axis=-1, keepdims=True)
        o = jnp.dot(p.astype(BF16), v_ref[:, hs], preferred_element_type=F32) / den
        o_ref[:, hs] = o.astype(o_ref.dtype)


def _mem_attn(proj, q_col_block, mem_k, mem_v, layer, name, tm=512):
    S = proj.shape[0]
    return pl.pallas_call(
        _mem_attn_kernel,
        grid=(S // tm,),
        in_specs=[
            pl.BlockSpec((tm, MEM_WIDTH), lambda i: (i, q_col_block)),
            pl.BlockSpec((None, MEM_TOKENS, MEM_WIDTH), lambda i: (layer, 0, 0)),
            pl.BlockSpec((None, MEM_TOKENS, MEM_WIDTH), lambda i: (layer, 0, 0)),
        ],
        out_specs=pl.BlockSpec((tm, MEM_WIDTH), lambda i: (i, 0)),
        out_shape=jax.ShapeDtypeStruct((S, MEM_WIDTH), BF16),
        compiler_params=_params(("parallel",), 32),
        name=name,
    )(proj, mem_k, mem_v)


def _rglru_kernel(u_ref, gate_ref, cw_ref, cb_ref, wg_ref, bg_ref, lam_ref, y_ref, ubuf, hcar, *, T, LB):
    i = pl.program_id(1)

    @pl.when(i == 0)
    def _():
        ubuf[0:SUBLANES, :] = jnp.zeros((SUBLANES, LB), F32)
        hcar[...] = jnp.zeros_like(hcar)

    u = u_ref[...]
    ubuf[SUBLANES:T + SUBLANES, :] = u
    cw = cw_ref[...]
    xc = cb_ref[...] + (ubuf[SUBLANES - 3:T + SUBLANES - 3, :] * cw[0:1, :]
                        + ubuf[SUBLANES - 2:T + SUBLANES - 2, :] * cw[1:2, :]
                        + ubuf[SUBLANES - 1:T + SUBLANES - 1, :] * cw[2:3, :]
                        + u * cw[3:4, :])
    ubuf[0:SUBLANES, :] = u[T - SUBLANES:T, :]

    lam = lam_ref[...]
    sp = jnp.maximum(-lam, 0.0) + jnp.log1p(jnp.exp(-jnp.abs(lam)))
    row = lax.broadcasted_iota(jnp.int32, (T // SUBLANES, SUBLANES, LANES), 1)

    for c in range(LB // LANES):
        cs = slice(c * LANES, (c + 1) * LANES)
        xcc = xc[:, cs]
        z = jnp.dot(xcc.astype(BF16), wg_ref[c], preferred_element_type=F32) + bg_ref[c]
        r = jax.nn.sigmoid(z[:, :LANES])
        ig = jax.nn.sigmoid(z[:, LANES:])
        log_a = (-RG_C * r) * sp[:, cs]
        a = jnp.exp(log_a)
        b = jnp.sqrt(1.0 - a * a) * (ig * xcc)

        a3 = a.reshape(T // SUBLANES, SUBLANES, LANES)
        b3 = b.reshape(T // SUBLANES, SUBLANES, LANES)
        for s in (1, 2, 4):
            a_sh = pltpu.roll(a3, s, axis=1)
            b_sh = pltpu.roll(b3, s, axis=1)
            keep = row >= s
            b3 = jnp.where(keep, a3 * b_sh + b3, b3)
            a3 = jnp.where(keep, a3 * a_sh, a3)

        hprev = hcar[:, cs]
        hs = []
        for grp in range(T // SUBLANES):
            hg = a3[grp] * hprev + b3[grp]
            hprev = hg[SUBLANES - 1:SUBLANES, :]
            hs.append(hg)
        hcar[:, cs] = hprev
        h = jnp.concatenate(hs, axis=0)

        gt = gate_ref[:, cs]
        gelu = 0.5 * gt * (1.0 + jnp.tanh(math.sqrt(2.0 / math.pi) * (gt + 0.044715 * (gt * gt * gt))))
        y_ref[:, cs] = (h * gelu).astype(y_ref.dtype)


def _rglru(proj, conv_w, conv_b, wg, bg, lam, name, T=512, LB=512):
    S = proj.shape[0]
    W = LRU_WIDTH
    ncb = W // LB
    kern = functools.partial(_rglru_kernel, T=T, LB=LB)
    return pl.pallas_call(
        kern,
        grid=(ncb, S // T),
        in_specs=[
            pl.BlockSpec((T, LB), lambda n, i: (i, n)),
            pl.BlockSpec((T, LB), lambda n, i: (i, ncb + n)),
            pl.BlockSpec((CONV_WIDTH, LB), lambda n, i: (0, n)),
            pl.BlockSpec((1, LB), lambda n, i: (0, n)),
            pl.BlockSpec((LB // LANES, LRU_BLOCK_WIDTH, 2 * LRU_BLOCK_WIDTH), lambda n, i: (n, 0, 0)),
            pl.BlockSpec((LB // LANES, 1, 2 * LRU_BLOCK_WIDTH), lambda n, i: (n, 0, 0)),
            pl.BlockSpec((1, LB), lambda n, i: (0, n)),
        ],
        out_specs=pl.BlockSpec((T, LB), lambda n, i: (i, n)),
        out_shape=jax.ShapeDtypeStruct((S, W), BF16),
        scratch_shapes=[pltpu.VMEM((T + SUBLANES, LB), F32), pltpu.VMEM((1, LB), F32)],
        compiler_params=_params(("arbitrary", "arbitrary"), 32),
        name=name,
    )(proj, proj, conv_w, conv_b.reshape(1, W), wg, bg, lam.reshape(1, W))


def _out_a_kernel(x_ref, y_ref, m_ref, w_ref, o_ref):
    acc = jnp.dot(y_ref[...], w_ref[0:LRU_WIDTH, :], preferred_element_type=F32)
    acc += jnp.dot(m_ref[...], w_ref[LRU_WIDTH:LRU_WIDTH + MEM_WIDTH, :], preferred_element_type=F32)
    o_ref[...] = x_ref[...] + acc


def _out_a(x, y, mem_out, w, name, tm=512):
    S, D = x.shape
    return pl.pallas_call(
        _out_a_kernel,
        grid=(S // tm,),
        in_specs=[
            pl.BlockSpec((tm, D), lambda i: (i, 0)),
            pl.BlockSpec((tm, LRU_WIDTH), lambda i: (i, 0)),
            pl.BlockSpec((tm, MEM_WIDTH), lambda i: (i, 0)),
            pl.BlockSpec((LRU_WIDTH + MEM_WIDTH, D), lambda i: (0, 0)),
        ],
        out_specs=pl.BlockSpec((tm, D), lambda i: (i, 0)),
        out_shape=jax.ShapeDtypeStruct((S, D), F32),
        compiler_params=_params(("parallel",), 48),
        name=name,
    )(x, y, mem_out, w)


def _bucket_tables():
    qi = np.arange(Q_BLOCK)[:, None]
    ki = np.arange(2 * Q_BLOCK)[None, :]
    u = qi + Q_BLOCK - ki
    tables, valids = [], []
    for window, dilation in DIL_PATTERNS:
        n = np.maximum(u * dilation, 0)
        nf = np.maximum(n, 1).astype(np.float32)
        large = REL_MAX_EXACT + (np.log(nf / np.float32(REL_MAX_EXACT))
                                 / np.float32(math.log(REL_MAX_DISTANCE / REL_MAX_EXACT))
                                 * np.float32(REL_BUCKETS - REL_MAX_EXACT)).astype(np.int32)
        large = np.minimum(large, REL_BUCKETS - 1)
        tables.append(np.where(n < REL_MAX_EXACT, n, large).astype(np.int32))
        valids.append(((u >= 0) & (u <= window // dilation)).astype(np.int32))
    return np.stack(tables), np.stack(valids)


def _bias_kernel(tab_ref, idx_ref, valid_ref, o_ref):
    g = pl.program_id(0)
    idx = idx_ref[...]
    valid = valid_ref[...] > 0
    for h in range(DIL_HEADS):
        acc = jnp.zeros(idx.shape, F32)
        for b in range(REL_BUCKETS):
            acc = jnp.where(idx == b, tab_ref[b, g * DIL_HEADS + h], acc)
        o_ref[h] = jnp.where(valid, acc, NEG_INF)


def _bias_expand(rel_bias):
    idx, valid = _bucket_tables()
    blk = pl.BlockSpec((None, Q_BLOCK, 2 * Q_BLOCK), lambda g: (g, 0, 0))
    return pl.pallas_call(
        _bias_kernel,
        grid=(DIL_GROUPS,),
        in_specs=[pl.BlockSpec(memory_space=pltpu.SMEM), blk, blk],
        out_specs=pl.BlockSpec((DIL_HEADS, Q_BLOCK, 2 * Q_BLOCK), lambda g: (g, 0, 0)),
        out_shape=jax.ShapeDtypeStruct((DIL_GROUPS * DIL_HEADS, Q_BLOCK, 2 * Q_BLOCK), F32),
        compiler_params=_params(("arbitrary",), 32),
        name="bias_expand",
    )(rel_bias, jnp.asarray(idx), jnp.asarray(valid))


def _dil_kernel(q_ref, kc_ref, kp_ref, vc_ref, vp_ref, b_ref, o_ref, *, LQ):
    is_first = pl.program_id(1) == 0
    for h in range(DIL_HEADS):
        hs = slice(h * HEAD_DIM, (h + 1) * HEAD_DIM)
        for sb in range(LQ // Q_BLOCK):
            rows = slice(sb * Q_BLOCK, (sb + 1) * Q_BLOCK)
            q = q_ref[rows, hs]
            if sb == 0:
                kk = jnp.concatenate([kp_ref[:, hs], kc_ref[0:Q_BLOCK, hs]], axis=0)
                vv = jnp.concatenate([vp_ref[:, hs], vc_ref[0:Q_BLOCK, hs]], axis=0)
            else:
                kk = kc_ref[(sb - 1) * Q_BLOCK:(sb + 1) * Q_BLOCK, hs]
                vv = vc_ref[(sb - 1) * Q_BLOCK:(sb + 1) * Q_BLOCK, hs]
            s = lax.dot_general(q, kk, (((1,), (1,)), ((), ())), preferred_element_type=F32) + b_ref[h]
            if sb == 0:
                col = lax.broadcasted_iota(jnp.int32, s.shape, 1)
                s = jnp.where(jnp.logical_and(is_first, col < Q_BLOCK), NEG_INF, s)
            m = jnp.max(s, axis=-1, keepdims=True)
            p = jnp.exp(s - m)
            den = jnp.sum(p, axis=-1, keepdims=True)
            o = jnp.dot(p.astype(BF16), vv, preferred_element_type=F32) / den
            lse = m + jnp.log(den)
            o_ref[rows, hs] = o
            o_ref[rows, DIL_OUT_WIDTH + h * HEAD_DIM:DIL_OUT_WIDTH + (h + 1) * HEAD_DIM] = (
                jnp.broadcast_to(lse, (Q_BLOCK, HEAD_DIM)))


def _dil_attn(qb, kvb, bias, g, name, LQ=512):
    S = qb.shape[0]
    d = DIL_PATTERNS[g][1]
    L = S // d
    qw, kvw, ow = qb.shape[1], kvb.shape[1], 2 * DIL_OUT_WIDTH
    qv = qb.reshape(L, d * qw)
    kvv = kvb.reshape(L, d * kvw)
    qcb, kvcb = qw // HEAD_BLOCK, kvw // HEAD_BLOCK
    ratio = LQ // Q_BLOCK
    kern = functools.partial(_dil_kernel, LQ=LQ)
    prev_row = lambda n: jnp.maximum(n * ratio - 1, 0)
    out = pl.pallas_call(
        kern,
        grid=(d, L // LQ),
        in_specs=[
            pl.BlockSpec((LQ, HEAD_BLOCK), lambda r, n: (n, r * qcb + g)),
            pl.BlockSpec((LQ, HEAD_BLOCK), lambda r, n: (n, r * kvcb + g)),
            pl.BlockSpec((Q_BLOCK, HEAD_BLOCK), lambda r, n: (prev_row(n), r * kvcb + g)),
            pl.BlockSpec((LQ, HEAD_BLOCK), lambda r, n: (n, r * kvcb + DIL_GROUPS + g)),
            pl.BlockSpec((Q_BLOCK, HEAD_BLOCK), lambda r, n: (prev_row(n), r * kvcb + DIL_GROUPS + g)),
            pl.BlockSpec((DIL_HEADS, Q_BLOCK, 2 * Q_BLOCK), lambda r, n: (g, 0, 0)),
        ],
        out_specs=pl.BlockSpec((LQ, ow), lambda r, n: (n, r)),
        out_shape=jax.ShapeDtypeStruct((L, d * ow), F32),
        compiler_params=_params(("arbitrary", "arbitrary"), 32),
        name=name,
    )(qv, kvv, kvv, kvv, kvv, bias)
    return out.reshape(S, ow)


def _out_b_kernel(x_ref, o0_ref, o1_ref, o2_ref, m_ref, w_ref, o_ref):
    refs = (o0_ref, o1_ref, o2_ref)
    parts = []
    for h in range(DIL_HEADS):
        hs = slice(h * HEAD_DIM, (h + 1) * HEAD_DIM)
        ls = slice(DIL_OUT_WIDTH + h * HEAD_DIM, DIL_OUT_WIDTH + (h + 1) * HEAD_DIM)
        lses = [r[:, ls] for r in refs]
        mx = jnp.maximum(jnp.maximum(lses[0], lses[1]), lses[2])
        es = [jnp.exp(l - mx) for l in lses]
        den = es[0] + es[1] + es[2]
        num = es[0] * refs[0][:, hs] + es[1] * refs[1][:, hs] + es[2] * refs[2][:, hs]
        parts.append((num / den).astype(BF16))
    dil = jnp.concatenate(parts, axis=-1)
    acc = jnp.dot(dil, w_ref[0:DIL_OUT_WIDTH, :], preferred_element_type=F32)
    acc += jnp.dot(m_ref[...], w_ref[DIL_OUT_WIDTH:DIL_OUT_WIDTH + MEM_WIDTH, :], preferred_element_type=F32)
    o_ref[...] = x_ref[...] + acc


def _out_b(x, ogs, mem_out, w, name, tm=512):
    S, D = x.shape
    ow = 2 * DIL_OUT_WIDTH
    og_spec = pl.BlockSpec((tm, ow), lambda i: (i, 0))
    return pl.pallas_call(
        _out_b_kernel,
        grid=(S // tm,),
        in_specs=[
            pl.BlockSpec((tm, D), lambda i: (i, 0)),
            og_spec, og_spec, og_spec,
            pl.BlockSpec((tm, MEM_WIDTH), lambda i: (i, 0)),
            pl.BlockSpec((DIL_OUT_WIDTH + MEM_WIDTH, D), lambda i: (0, 0)),
        ],
        out_specs=pl.BlockSpec((tm, D), lambda i: (i, 0)),
        out_shape=jax.ShapeDtypeStruct((S, D), F32),
        compiler_params=_params(("parallel",), 48),
        name=name,
    )(x, ogs[0], ogs[1], ogs[2], mem_out, w)


def kernel(x, mem, norm_mix_g, norm_mlp_g, mlp_w1, mlp_w2, mem_norm_g, mem_w_kv, mem_q_norm_g, mem_k_norm_g, a_w_in, a_conv_w, a_conv_b, a_gate_r_w, a_gate_r_b, a_gate_i_w, a_gate_i_b, a_lambda, a_w_out, kv_norm_g, kv_w, k_norm_g, rel_bias, b_w_q, b_q_norm_g, b_w_out):
    B, S, D = x.shape
    assert (B, S, D) == (1, SEQ, D_MODEL) and mem.shape == (1, MEM_TOKENS, D_MODEL)
    xs = x[0]
    ones = jnp.ones((1, HEAD_DIM), F32)

    mem_k, mem_v = _mem_kv(mem[0], mem_norm_g, mem_w_kv, mem_k_norm_g)
    bias = _bias_expand(rel_bias)
    kvb = None

    for layer in range(DEPTH):
        mq_gain = mem_q_norm_g[layer][None, :]
        if layer < N_A_LAYERS:
            i = layer
            nblk = A_IN_WIDTH // HEAD_BLOCK
            gains = jnp.concatenate([jnp.tile(ones, (nblk - 1, 1)), mq_gain], axis=0)
            proj = _norm_matmul(xs, norm_mix_g[layer], a_w_in[i].astype(BF16), gains, nblk - 1, nblk,
                                ATTN_SCALE, F32, f"a{i}_in_proj")
            mem_out = _mem_attn(proj, nblk - 1, mem_k, mem_v, layer, f"a{i}_mem_attn")
            wg = jnp.concatenate([a_gate_r_w[i], a_gate_i_w[i]], axis=-1).astype(BF16)
            bg = jnp.concatenate([a_gate_r_b[i], a_gate_i_b[i]], axis=-1)[:, None, :]
            y = _rglru(proj, a_conv_w[i], a_conv_b[i], wg, bg, a_lambda[i], f"a{i}_rglru")
            xs = _out_a(xs, y, mem_out, a_w_out[i].astype(BF16), f"a{i}_out_proj")
        else:
            j = layer - N_A_LAYERS
            if j == 0:
                kv_gains = jnp.concatenate([k_norm_g, jnp.tile(ones, (DIL_GROUPS, 1))], axis=0)
                kvb = _norm_matmul(xs, kv_norm_g, kv_w.astype(BF16), kv_gains, 0, DIL_GROUPS, 1.0, BF16,
                                   "shared_kv")
            q_gains = jnp.concatenate([b_q_norm_g[j], mq_gain], axis=0)
            qb = _norm_matmul(xs, norm_mix_g[layer], b_w_q[j].astype(BF16), q_gains, 0, DIL_GROUPS + 1,
                              ATTN_SCALE, BF16, f"b{j}_q_proj")
            mem_out = _mem_attn(qb, DIL_GROUPS, mem_k, mem_v, layer, f"b{j}_mem_attn")
            ogs = [_dil_attn(qb, kvb, bias, g, f"b{j}_dil{g}") for g in range(DIL_GROUPS)]
            xs = _out_b(xs, ogs, mem_out, b_w_out[j].astype(BF16), f"b{j}_out_proj")
        xs = _mlp(xs, norm_mlp_g[layer], mlp_w1[layer].astype(BF16), mlp_w2[layer].astype(BF16),
                  f"mlp{layer}")
    return xs[None]
```

```python
import functools
import math

import numpy as np
import jax
import jax.numpy as jnp
from jax import lax
from jax.experimental import pallas as pl
from jax.experimental.pallas import tpu as pltpu

F32 = jnp.float32
BF16 = jnp.bfloat16

D_MODEL = 2048
SEQ = 8192
DEPTH = 4
N_A_LAYERS = DEPTH // 2
HEAD_DIM = 128
D_FF = 4 * D_MODEL
LRU_WIDTH = 3 * D_MODEL // 4
LRU_BLOCK_WIDTH = 128
LRU_BLOCKS = LRU_WIDTH // LRU_BLOCK_WIDTH
CONV_WIDTH = 4
RG_C = 8.0
MEM_TOKENS = 256
MEM_HEADS = 4
MEM_WIDTH = MEM_HEADS * HEAD_DIM
DIL_PATTERNS = ((128, 1), (512, 4), (2048, 16))
DIL_GROUPS = len(DIL_PATTERNS)
DIL_HEADS = 4
DIL_WIDTH = DIL_GROUPS * DIL_HEADS * HEAD_DIM
DIL_OUT_WIDTH = DIL_HEADS * HEAD_DIM
Q_BLOCK = 128
REL_BUCKETS = 32
REL_MAX_EXACT = REL_BUCKETS // 2
REL_MAX_DISTANCE = 2048
A_IN_WIDTH = 2 * LRU_WIDTH + MEM_WIDTH
NORM_EPS = 1e-6
NEG_INF = -1e30
ATTN_SCALE = HEAD_DIM ** -0.5

V7X_VMEM_BYTES = 64 * 1024 * 1024
LANES = 128
SUBLANES = 8
HEAD_BLOCK = DIL_HEADS * HEAD_DIM


def _params(semantics, vmem_mib):
    assert vmem_mib * 1024 * 1024 < V7X_VMEM_BYTES
    return pltpu.CompilerParams(dimension_semantics=semantics,
                                vmem_limit_bytes=vmem_mib * 1024 * 1024)


def _rms_rows(x, g):
    ms = jnp.mean(x * x, axis=-1, keepdims=True)
    return x * lax.rsqrt(ms + NORM_EPS) * g


def _head_norm_cols(a, g, scale):
    parts = []
    for h in range(a.shape[1] // HEAD_DIM):
        ah = a[:, h * HEAD_DIM:(h + 1) * HEAD_DIM]
        y = ah * lax.rsqrt(jnp.mean(ah * ah, axis=-1, keepdims=True) + NORM_EPS) * g
        parts.append(y * scale if scale != 1.0 else y)
    return jnp.concatenate(parts, axis=-1)


def _rms_kernel(x_ref, g_ref, *o_refs):
    x = x_ref[...]
    xh = x * lax.rsqrt(jnp.mean(x * x, axis=-1, keepdims=True) + NORM_EPS)
    for k, o_ref in enumerate(o_refs):
        o_ref[...] = (xh * g_ref[k:k + 1, :]).astype(BF16)


def _rms(x, gains, name, tm=512):
    S, D = x.shape
    n = gains.shape[0]
    spec = pl.BlockSpec((tm, D), lambda i: (i, 0))
    return pl.pallas_call(
        _rms_kernel,
        grid=(S // tm,),
        in_specs=[spec, pl.BlockSpec((n, D), lambda i: (0, 0))],
        out_specs=[spec] * n,
        out_shape=[jax.ShapeDtypeStruct((S, D), BF16)] * n,
        compiler_params=_params(("parallel",), 32),
        name=name,
    )(x, gains)


def _proj_kernel(hn_ref, w_ref, hg_ref, o_ref, *, norm_lo, norm_hi, nblk, scale):
    j = pl.program_id(1)
    acc = jnp.dot(hn_ref[...], w_ref[...].astype(BF16), preferred_element_type=F32)

    def store_normed():
        o_ref[...] = _head_norm_cols(acc, hg_ref[...], scale).astype(o_ref.dtype)

    def store_raw():
        o_ref[...] = acc.astype(o_ref.dtype)

    if norm_lo == 0 and norm_hi == nblk:
        store_normed()
    elif norm_lo == norm_hi:
        store_raw()
    else:
        is_normed = jnp.logical_and(j >= norm_lo, j < norm_hi)
        pl.when(is_normed)(store_normed)
        pl.when(jnp.logical_not(is_normed))(store_raw)


def _proj(hn, w, head_gains, norm_lo, norm_hi, scale, out_dtype, name, tm=2048):
    S, D = hn.shape
    N = w.shape[1]
    tn = HEAD_BLOCK
    nblk = N // tn
    kern = functools.partial(_proj_kernel, norm_lo=norm_lo, norm_hi=norm_hi, nblk=nblk, scale=scale)
    return pl.pallas_call(
        kern,
        grid=(S // tm, nblk),
        in_specs=[
            pl.BlockSpec((tm, D), lambda i, j: (i, 0)),
            pl.BlockSpec((D, tn), lambda i, j: (0, j)),
            pl.BlockSpec((None, 1, HEAD_DIM), lambda i, j: (j, 0, 0)),
        ],
        out_specs=pl.BlockSpec((tm, tn), lambda i, j: (i, j)),
        out_shape=jax.ShapeDtypeStruct((S, N), out_dtype),
        compiler_params=_params(("parallel", "arbitrary"), 48),
        name=name,
    )(hn, w, head_gains.reshape(nblk, 1, HEAD_DIM))


def _mlp_kernel(hn_ref, x_ref, w1_ref, w2_ref, *rest, n_steps, emit_next):
    if emit_next:
        gn_ref, o_ref, hno_ref = rest
    else:
        (o_ref,) = rest
    j = pl.program_id(1)

    @pl.when(j == 0)
    def _():
        o_ref[...] = x_ref[...]

    h = jnp.dot(hn_ref[...], w1_ref[...].astype(BF16), preferred_element_type=F32)
    h = jnp.maximum(h, 0.0)
    h = (h * h).astype(BF16)
    o_ref[...] += jnp.dot(h, w2_ref[...].astype(BF16), preferred_element_type=F32)

    if emit_next:
        @pl.when(j == n_steps - 1)
        def _():
            hno_ref[...] = _rms_rows(o_ref[...], gn_ref[...]).astype(BF16)


def _mlp(hn, x, w1, w2, g_next, name, tm=1024, tf=512):
    S, D = x.shape
    F = w1.shape[1]
    emit_next = g_next is not None
    row_spec = lambda: pl.BlockSpec((tm, D), lambda i, j: (i, 0))
    once = lambda: pl.BlockSpec((tm, D), lambda i, j: (i, 0), pipeline_mode=pl.Buffered(1))
    in_specs = [once(), once(),
                pl.BlockSpec((D, tf), lambda i, j: (0, j)),
                pl.BlockSpec((tf, D), lambda i, j: (j, 0))]
    args = [hn, x, w1, w2]
    out_specs = [row_spec()]
    out_shape = [jax.ShapeDtypeStruct((S, D), F32)]
    if emit_next:
        in_specs.append(pl.BlockSpec((1, D), lambda i, j: (0, 0)))
        args.append(g_next.reshape(1, D))
        out_specs.append(once())
        out_shape.append(jax.ShapeDtypeStruct((S, D), BF16))
    kern = functools.partial(_mlp_kernel, n_steps=F // tf, emit_next=emit_next)
    outs = pl.pallas_call(
        kern,
        grid=(S // tm, F // tf),
        in_specs=in_specs,
        out_specs=out_specs,
        out_shape=out_shape,
        compiler_params=_params(("parallel", "arbitrary"), 60),
        name=name,
    )(*args)
    return (outs[0], outs[1]) if emit_next else (outs[0], None)


def _mem_kv_kernel(mem_ref, g_ref, w_ref, kg_ref, k_ref, v_ref):
    hn = _rms_rows(mem_ref[...], g_ref[...]).astype(BF16)
    kv = jnp.dot(hn, w_ref[...].astype(BF16), preferred_element_type=F32)
    k_ref[...] = _head_norm_cols(kv[:, :MEM_WIDTH], kg_ref[...], 1.0).astype(BF16)
    v_ref[...] = kv[:, MEM_WIDTH:].astype(BF16)


def _mem_kv(mem, mem_norm_g, mem_w_kv, mem_k_norm_g):
    M, D = mem.shape
    out = jax.ShapeDtypeStruct((DEPTH, M, MEM_WIDTH), BF16)
    return pl.pallas_call(
        _mem_kv_kernel,
        grid=(DEPTH,),
        in_specs=[
            pl.BlockSpec((M, D), lambda l: (0, 0)),
            pl.BlockSpec((None, 1, D), lambda l: (l, 0, 0)),
            pl.BlockSpec((None, D, 2 * MEM_WIDTH), lambda l: (l, 0, 0)),
            pl.BlockSpec((None, 1, HEAD_DIM), lambda l: (l, 0, 0)),
        ],
        out_specs=[pl.BlockSpec((None, M, MEM_WIDTH), lambda l: (l, 0, 0)),
                   pl.BlockSpec((None, M, MEM_WIDTH), lambda l: (l, 0, 0))],
        out_shape=[out, out],
        compiler_params=_params(("arbitrary",), 40),
        name="mem_kv",
    )(mem, mem_norm_g.reshape(DEPTH, 1, D), mem_w_kv, mem_k_norm_g.reshape(DEPTH, 1, HEAD_DIM))


def _mem_attn_kernel(q_ref, k_ref, v_ref, o_ref):
    for h in range(MEM_HEADS):
        hs = slice(h * HEAD_DIM, (h + 1) * HEAD_DIM)
        q = q_ref[:, hs].astype(BF16)
        s = lax.dot_general(q, k_ref[:, hs], (((1,), (1,)), ((), ())), preferred_element_type=F32)
        m = jnp.max(s, axis=-1, keepdims=True)
        p = jnp.exp(s - m)
        den = jnp.sum(p, axis=-1, keepdims=True)
        o = jnp.dot(p.astype(BF16), v_ref[:, hs], preferred_element_type=F32) / den
        o_ref[:, hs] = o.astype(o_ref.dtype)


def _mem_attn(proj, q_col_block, mem_k, mem_v, layer, name, tm=512):
    S = proj.shape[0]
    return pl.pallas_call(
        _mem_attn_kernel,
        grid=(S // tm,),
        in_specs=[
            pl.BlockSpec((tm, MEM_WIDTH), lambda i: (i, q_col_block)),
            pl.BlockSpec((None, MEM_TOKENS, MEM_WIDTH), lambda i: (layer, 0, 0)),
            pl.BlockSpec((None, MEM_TOKENS, MEM_WIDTH), lambda i: (layer, 0, 0)),
        ],
        out_specs=pl.BlockSpec((tm, MEM_WIDTH), lambda i: (i, 0)),
        out_shape=jax.ShapeDtypeStruct((S, MEM_WIDTH), BF16),
        compiler_params=_params(("parallel",), 32),
        name=name,
    )(proj, mem_k, mem_v)


def _rglru_kernel(u_ref, gate_ref, cw_ref, cb_ref, wg_ref, bg_ref, lam_ref, y_ref, ubuf, hcar, *, T, LB):
    i = pl.program_id(1)

    @pl.when(i == 0)
    def _():
        ubuf[0:SUBLANES, :] = jnp.zeros((SUBLANES, LB), F32)
        hcar[...] = jnp.zeros_like(hcar)

    u = u_ref[...]
    ubuf[SUBLANES:T + SUBLANES, :] = u
    cw = cw_ref[...]
    xc = cb_ref[...] + (ubuf[SUBLANES - 3:T + SUBLANES - 3, :] * cw[0:1, :]
                        + ubuf[SUBLANES - 2:T + SUBLANES - 2, :] * cw[1:2, :]
                        + ubuf[SUBLANES - 1:T + SUBLANES - 1, :] * cw[2:3, :]
                        + u * cw[3:4, :])
    ubuf[0:SUBLANES, :] = u[T - SUBLANES:T, :]

    lam = lam_ref[...]
    sp = jnp.maximum(-lam, 0.0) + jnp.log1p(jnp.exp(-jnp.abs(lam)))
    row = lax.broadcasted_iota(jnp.int32, (T // SUBLANES, SUBLANES, LANES), 1)

    for c in range(LB // LANES):
        cs = slice(c * LANES, (c + 1) * LANES)
        xcc = xc[:, cs]
        z = jnp.dot(xcc.astype(BF16), wg_ref[c], preferred_element_type=F32) + bg_ref[c]
        r = jax.nn.sigmoid(z[:, :LANES])
        ig = jax.nn.sigmoid(z[:, LANES:])
        log_a = (-RG_C * r) * sp[:, cs]
        a = jnp.exp(log_a)
        b = jnp.sqrt(1.0 - a * a) * (ig * xcc)

        a3 = a.reshape(T // SUBLANES, SUBLANES, LANES)
        b3 = b.reshape(T // SUBLANES, SUBLANES, LANES)
        for s in (1, 2, 4):
            a_sh = pltpu.roll(a3, s, axis=1)
            b_sh = pltpu.roll(b3, s, axis=1)
            keep = row >= s
            b3 = jnp.where(keep, a3 * b_sh + b3, b3)
            a3 = jnp.where(keep, a3 * a_sh, a3)

        hprev = hcar[:, cs]
        hs = []
        for grp in range(T // SUBLANES):
            hg = a3[grp] * hprev + b3[grp]
            hprev = hg[SUBLANES - 1:SUBLANES, :]
            hs.append(hg)
        hcar[:, cs] = hprev
        h = jnp.concatenate(hs, axis=0)

        gt = gate_ref[:, cs]
        gelu = 0.5 * gt * (1.0 + jnp.tanh(math.sqrt(2.0 / math.pi) * (gt + 0.044715 * (gt * gt * gt))))
        y_ref[:, cs] = (h * gelu).astype(y_ref.dtype)


def _rglru(proj, conv_w, conv_b, wg, bg, lam, name, T=512, LB=512):
    S = proj.shape[0]
    W = LRU_WIDTH
    ncb = W // LB
    kern = functools.partial(_rglru_kernel, T=T, LB=LB)
    return pl.pallas_call(
        kern,
        grid=(ncb, S // T),
        in_specs=[
            pl.BlockSpec((T, LB), lambda n, i: (i, n)),
            pl.BlockSpec((T, LB), lambda n, i: (i, ncb + n)),
            pl.BlockSpec((CONV_WIDTH, LB), lambda n, i: (0, n)),
            pl.BlockSpec((1, LB), lambda n, i: (0, n)),
            pl.BlockSpec((LB // LANES, LRU_BLOCK_WIDTH, 2 * LRU_BLOCK_WIDTH), lambda n, i: (n, 0, 0)),
            pl.BlockSpec((LB // LANES, 1, 2 * LRU_BLOCK_WIDTH), lambda n, i: (n, 0, 0)),
            pl.BlockSpec((1, LB), lambda n, i: (0, n)),
        ],
        out_specs=pl.BlockSpec((T, LB), lambda n, i: (i, n)),
        out_shape=jax.ShapeDtypeStruct((S, W), BF16),
        scratch_shapes=[pltpu.VMEM((T + SUBLANES, LB), F32), pltpu.VMEM((1, LB), F32)],
        compiler_params=_params(("arbitrary", "arbitrary"), 32),
        name=name,
    )(proj, proj, conv_w, conv_b.reshape(1, W), wg, bg, lam.reshape(1, W))


def _cast_weight_once(w_ref, wb_ref):
    @pl.when(pl.program_id(0) == 0)
    def _():
        wb_ref[...] = w_ref[...].astype(BF16)


def _residual_and_norm(x_ref, acc, g_ref, o_ref, hn_ref):
    xn = x_ref[...] + acc
    o_ref[...] = xn
    hn_ref[...] = _rms_rows(xn, g_ref[...]).astype(BF16)


def _out_a_kernel(x_ref, y_ref, m_ref, w_ref, g_ref, o_ref, hn_ref, wb_ref):
    _cast_weight_once(w_ref, wb_ref)
    acc = jnp.dot(y_ref[...], wb_ref[0:LRU_WIDTH, :], preferred_element_type=F32)
    acc += jnp.dot(m_ref[...], wb_ref[LRU_WIDTH:LRU_WIDTH + MEM_WIDTH, :], preferred_element_type=F32)
    _residual_and_norm(x_ref, acc, g_ref, o_ref, hn_ref)


def _out_a(x, y, mem_out, w, g_mlp, name, tm=512):
    S, D = x.shape
    K = LRU_WIDTH + MEM_WIDTH
    row = pl.BlockSpec((tm, D), lambda i: (i, 0))
    return pl.pallas_call(
        _out_a_kernel,
        grid=(S // tm,),
        in_specs=[
            row,
            pl.BlockSpec((tm, LRU_WIDTH), lambda i: (i, 0)),
            pl.BlockSpec((tm, MEM_WIDTH), lambda i: (i, 0)),
            pl.BlockSpec((K, D), lambda i: (0, 0), pipeline_mode=pl.Buffered(1)),
            pl.BlockSpec((1, D), lambda i: (0, 0)),
        ],
        out_specs=[row, row],
        out_shape=[jax.ShapeDtypeStruct((S, D), F32), jax.ShapeDtypeStruct((S, D), BF16)],
        scratch_shapes=[pltpu.VMEM((K, D), BF16)],
        compiler_params=_params(("arbitrary",), 56),
        name=name,
    )(x, y, mem_out, w, g_mlp.reshape(1, D))


def _bucket_tables():
    qi = np.arange(Q_BLOCK)[:, None]
    ki = np.arange(2 * Q_BLOCK)[None, :]
    u = qi + Q_BLOCK - ki
    tables, valids = [], []
    for window, dilation in DIL_PATTERNS:
        n = np.maximum(u * dilation, 0)
        nf = np.maximum(n, 1).astype(np.float32)
        large = REL_MAX_EXACT + (np.log(nf / np.float32(REL_MAX_EXACT))
                                 / np.float32(math.log(REL_MAX_DISTANCE / REL_MAX_EXACT))
                                 * np.float32(REL_BUCKETS - REL_MAX_EXACT)).astype(np.int32)
        large = np.minimum(large, REL_BUCKETS - 1)
        tables.append(np.where(n < REL_MAX_EXACT, n, large).astype(np.int32))
        valids.append(((u >= 0) & (u <= window // dilation)).astype(np.int32))
    return np.stack(tables), np.stack(valids)


def _bias_kernel(tab_ref, idx_ref, valid_ref, o_ref):
    g = pl.program_id(0)
    idx = idx_ref[...]
    valid = valid_ref[...] > 0
    for h in range(DIL_HEADS):
        acc = jnp.zeros(idx.shape, F32)
        for b in range(REL_BUCKETS):
            acc = jnp.where(idx == b, tab_ref[b, g * DIL_HEADS + h], acc)
        o_ref[h] = jnp.where(valid, acc, NEG_INF)


def _bias_expand(rel_bias):
    idx, valid = _bucket_tables()
    blk = pl.BlockSpec((None, Q_BLOCK, 2 * Q_BLOCK), lambda g: (g, 0, 0))
    return pl.pallas_call(
        _bias_kernel,
        grid=(DIL_GROUPS,),
        in_specs=[pl.BlockSpec(memory_space=pltpu.SMEM), blk, blk],
        out_specs=pl.BlockSpec((DIL_HEADS, Q_BLOCK, 2 * Q_BLOCK), lambda g: (g, 0, 0)),
        out_shape=jax.ShapeDtypeStruct((DIL_GROUPS * DIL_HEADS, Q_BLOCK, 2 * Q_BLOCK), F32),
        compiler_params=_params(("arbitrary",), 32),
        name="bias_expand",
    )(rel_bias, jnp.asarray(idx), jnp.asarray(valid))


def _dil_kernel(q_ref, kc_ref, kp_ref, vc_ref, vp_ref, b_ref, o_ref, *, LQ):
    is_first = pl.program_id(1) == 0
    for h in range(DIL_HEADS):
        hs = slice(h * HEAD_DIM, (h + 1) * HEAD_DIM)
        for sb in range(LQ // Q_BLOCK):
            rows = slice(sb * Q_BLOCK, (sb + 1) * Q_BLOCK)
            q = q_ref[rows, hs]
            if sb == 0:
                kk = jnp.concatenate([kp_ref[:, hs], kc_ref[0:Q_BLOCK, hs]], axis=0)
                vv = jnp.concatenate([vp_ref[:, hs], vc_ref[0:Q_BLOCK, hs]], axis=0)
            else:
                kk = kc_ref[(sb - 1) * Q_BLOCK:(sb + 1) * Q_BLOCK, hs]
                vv = vc_ref[(sb - 1) * Q_BLOCK:(sb + 1) * Q_BLOCK, hs]
            s = lax.dot_general(q, kk, (((1,), (1,)), ((), ())), preferred_element_type=F32) + b_ref[h]
            if sb == 0:
                col = lax.broadcasted_iota(jnp.int32, s.shape, 1)
                s = jnp.where(jnp.logical_and(is_first, col < Q_BLOCK), NEG_INF, s)
            m = jnp.max(s, axis=-1, keepdims=True)
            p = jnp.exp(s - m)
            den = jnp.sum(p, axis=-1, keepdims=True)
            o = jnp.dot(p.astype(BF16), vv, preferred_element_type=F32) / den
            lse = m + jnp.log(den)
            o_ref[rows, hs] = o
            o_ref[rows, DIL_OUT_WIDTH + h * HEAD_DIM:DIL_OUT_WIDTH + (h + 1) * HEAD_DIM] = (
                jnp.broadcast_to(lse, (Q_BLOCK, HEAD_DIM)))


def _dil_attn(qb, kvb, bias, g, name, LQ=512):
    S = qb.shape[0]
    d = DIL_PATTERNS[g][1]
    L = S // d
    qw, kvw, ow = qb.shape[1], kvb.shape[1], 2 * DIL_OUT_WIDTH
    qv = qb.reshape(L, d * qw)
    kvv = kvb.reshape(L, d * kvw)
    qcb, kvcb = qw // HEAD_BLOCK, kvw // HEAD_BLOCK
    ratio = LQ // Q_BLOCK
    kern = functools.partial(_dil_kernel, LQ=LQ)
    prev_row = lambda n: jnp.maximum(n * ratio - 1, 0)
    out = pl.pallas_call(
        kern,
        grid=(d, L // LQ),
        in_specs=[
            pl.BlockSpec((LQ, HEAD_BLOCK), lambda r, n: (n, r * qcb + g)),
            pl.BlockSpec((LQ, HEAD_BLOCK), lambda r, n: (n, r * kvcb + g)),
            pl.BlockSpec((Q_BLOCK, HEAD_BLOCK), lambda r, n: (prev_row(n), r * kvcb + g)),
            pl.BlockSpec((LQ, HEAD_BLOCK), lambda r, n: (n, r * kvcb + DIL_GROUPS + g)),
            pl.BlockSpec((Q_BLOCK, HEAD_BLOCK), lambda r, n: (prev_row(n), r * kvcb + DIL_GROUPS + g)),
            pl.BlockSpec((DIL_HEADS, Q_BLOCK, 2 * Q_BLOCK), lambda r, n: (g, 0, 0)),
        ],
        out_specs=pl.BlockSpec((LQ, ow), lambda r, n: (n, r)),
        out_shape=jax.ShapeDtypeStruct((L, d * ow), F32),
        compiler_params=_params(("arbitrary", "arbitrary"), 32),
        name=name,
    )(qv, kvv, kvv, kvv, kvv, bias)
    return out.reshape(S, ow)


def _out_b_kernel(x_ref, o0_ref, o1_ref, o2_ref, m_ref, w_ref, g_ref, o_ref, hn_ref, wb_ref):
    _cast_weight_once(w_ref, wb_ref)
    refs = (o0_ref, o1_ref, o2_ref)
    parts = []
    for h in range(DIL_HEADS):
        hs = slice(h * HEAD_DIM, (h + 1) * HEAD_DIM)
        ls = slice(DIL_OUT_WIDTH + h * HEAD_DIM, DIL_OUT_WIDTH + (h + 1) * HEAD_DIM)
        lses = [r[:, ls] for r in refs]
        mx = jnp.maximum(jnp.maximum(lses[0], lses[1]), lses[2])
        es = [jnp.exp(l - mx) for l in lses]
        den = es[0] + es[1] + es[2]
        num = es[0] * refs[0][:, hs] + es[1] * refs[1][:, hs] + es[2] * refs[2][:, hs]
        parts.append((num / den).astype(BF16))
    dil = jnp.concatenate(parts, axis=-1)
    acc = jnp.dot(dil, wb_ref[0:DIL_OUT_WIDTH, :], preferred_element_type=F32)
    acc += jnp.dot(m_ref[...], wb_ref[DIL_OUT_WIDTH:DIL_OUT_WIDTH + MEM_WIDTH, :], preferred_element_type=F32)
    _residual_and_norm(x_ref, acc, g_ref, o_ref, hn_ref)


def _out_b(x, ogs, mem_out, w, g_mlp, name, tm=512):
    S, D = x.shape
    ow = 2 * DIL_OUT_WIDTH
    K = DIL_OUT_WIDTH + MEM_WIDTH
    og_spec = pl.BlockSpec((tm, ow), lambda i: (i, 0))
    row = pl.BlockSpec((tm, D), lambda i: (i, 0))
    return pl.pallas_call(
        _out_b_kernel,
        grid=(S // tm,),
        in_specs=[
            row,
            og_spec, og_spec, og_spec,
            pl.BlockSpec((tm, MEM_WIDTH), lambda i: (i, 0)),
            pl.BlockSpec((K, D), lambda i: (0, 0), pipeline_mode=pl.Buffered(1)),
            pl.BlockSpec((1, D), lambda i: (0, 0)),
        ],
        out_specs=[row, row],
        out_shape=[jax.ShapeDtypeStruct((S, D), F32), jax.ShapeDtypeStruct((S, D), BF16)],
        scratch_shapes=[pltpu.VMEM((K, D), BF16)],
        compiler_params=_params(("arbitrary",), 56),
        name=name,
    )(x, ogs[0], ogs[1], ogs[2], mem_out, w, g_mlp.reshape(1, D))


def kernel(x, mem, norm_mix_g, norm_mlp_g, mlp_w1, mlp_w2, mem_norm_g, mem_w_kv, mem_q_norm_g, mem_k_norm_g, a_w_in, a_conv_w, a_conv_b, a_gate_r_w, a_gate_r_b, a_gate_i_w, a_gate_i_b, a_lambda, a_w_out, kv_norm_g, kv_w, k_norm_g, rel_bias, b_w_q, b_q_norm_g, b_w_out):
    B, S, D = x.shape
    assert (B, S, D) == (1, SEQ, D_MODEL) and mem.shape == (1, MEM_TOKENS, D_MODEL)
    xs = x[0]
    ones = jnp.ones((1, HEAD_DIM), F32)

    mem_k, mem_v = _mem_kv(mem[0], mem_norm_g, mem_w_kv, mem_k_norm_g)
    bias = _bias_expand(rel_bias)
    kvb = None
    (hn,) = _rms(xs, norm_mix_g[0:1], "rms_first")

    for layer in range(DEPTH):
        mq_gain = mem_q_norm_g[layer][None, :]
        if layer < N_A_LAYERS:
            i = layer
            nblk = A_IN_WIDTH // HEAD_BLOCK
            gains = jnp.concatenate([jnp.tile(ones, (nblk - 1, 1)), mq_gain], axis=0)
            proj = _proj(hn, a_w_in[i], gains, nblk - 1, nblk, ATTN_SCALE, F32, f"a{i}_in_proj")
            mem_out = _mem_attn(proj, nblk - 1, mem_k, mem_v, layer, f"a{i}_mem_attn")
            wg = jnp.concatenate([a_gate_r_w[i], a_gate_i_w[i]], axis=-1).astype(BF16)
            bg = jnp.concatenate([a_gate_r_b[i], a_gate_i_b[i]], axis=-1)[:, None, :]
            y = _rglru(proj, a_conv_w[i], a_conv_b[i], wg, bg, a_lambda[i], f"a{i}_rglru")
            xs, hn = _out_a(xs, y, mem_out, a_w_out[i], norm_mlp_g[layer], f"a{i}_out_proj")
        else:
            j = layer - N_A_LAYERS
            if j == 0:
                (hn_kv,) = _rms(xs, kv_norm_g[None, :], "rms_kv")
                kv_gains = jnp.concatenate([k_norm_g, jnp.tile(ones, (DIL_GROUPS, 1))], axis=0)
                kvb = _proj(hn_kv, kv_w, kv_gains, 0, DIL_GROUPS, 1.0, BF16, "shared_kv")
            q_gains = jnp.concatenate([b_q_norm_g[j], mq_gain], axis=0)
            qb = _proj(hn, b_w_q[j], q_gains, 0, DIL_GROUPS + 1, ATTN_SCALE, BF16, f"b{j}_q_proj")
            mem_out = _mem_attn(qb, DIL_GROUPS, mem_k, mem_v, layer, f"b{j}_mem_attn")
            ogs = [_dil_attn(qb, kvb, bias, g, f"b{j}_dil{g}") for g in range(DIL_GROUPS)]
            xs, hn = _out_b(xs, ogs, mem_out, b_w_out[j], norm_mlp_g[layer], f"b{j}_out_proj")
        g_next = norm_mix_g[layer + 1] if layer + 1 < DEPTH else None
        xs, hn = _mlp(hn, xs, mlp_w1[layer], mlp_w2[layer], g_next, f"mlp{layer}")
    return xs[None]
```

```python
import functools
import math

import numpy as np
import jax
import jax.numpy as jnp
from jax import lax
from jax.experimental import pallas as pl
from jax.experimental.pallas import tpu as pltpu

F32 = jnp.float32
BF16 = jnp.bfloat16

D_MODEL = 2048
SEQ = 8192
DEPTH = 4
N_A_LAYERS = DEPTH // 2
HEAD_DIM = 128
D_FF = 4 * D_MODEL
LRU_WIDTH = 3 * D_MODEL // 4
LRU_BLOCK_WIDTH = 128
LRU_BLOCKS = LRU_WIDTH // LRU_BLOCK_WIDTH
CONV_WIDTH = 4
RG_C = 8.0
MEM_TOKENS = 256
MEM_HEADS = 4
MEM_WIDTH = MEM_HEADS * HEAD_DIM
DIL_PATTERNS = ((128, 1), (512, 4), (2048, 16))
DIL_GROUPS = len(DIL_PATTERNS)
DIL_HEADS = 4
DIL_WIDTH = DIL_GROUPS * DIL_HEADS * HEAD_DIM
DIL_OUT_WIDTH = DIL_HEADS * HEAD_DIM
Q_BLOCK = 128
REL_BUCKETS = 32
REL_MAX_EXACT = REL_BUCKETS // 2
REL_MAX_DISTANCE = 2048
A_IN_WIDTH = 2 * LRU_WIDTH + MEM_WIDTH
NORM_EPS = 1e-6
NEG_INF = -1e30
ATTN_SCALE = HEAD_DIM ** -0.5

V7X_VMEM_BYTES = 64 * 1024 * 1024
LANES = 128
SUBLANES = 8
HEAD_BLOCK = DIL_HEADS * HEAD_DIM


def _params(semantics, vmem_mib):
    assert vmem_mib * 1024 * 1024 < V7X_VMEM_BYTES
    return pltpu.CompilerParams(dimension_semantics=semantics,
                                vmem_limit_bytes=vmem_mib * 1024 * 1024)


def _rms_rows(x, g):
    ms = jnp.mean(x * x, axis=-1, keepdims=True)
    return x * lax.rsqrt(ms + NORM_EPS) * g


def _head_norm_cols(a, g, scale):
    parts = []
    for h in range(a.shape[1] // HEAD_DIM):
        ah = a[:, h * HEAD_DIM:(h + 1) * HEAD_DIM]
        y = ah * lax.rsqrt(jnp.mean(ah * ah, axis=-1, keepdims=True) + NORM_EPS) * g
        parts.append(y * scale if scale != 1.0 else y)
    return jnp.concatenate(parts, axis=-1)


def _rms_kernel(x_ref, g_ref, *o_refs):
    x = x_ref[...]
    xh = x * lax.rsqrt(jnp.mean(x * x, axis=-1, keepdims=True) + NORM_EPS)
    for k, o_ref in enumerate(o_refs):
        o_ref[...] = (xh * g_ref[k:k + 1, :]).astype(BF16)


def _rms(x, gains, name, tm=512):
    S, D = x.shape
    n = gains.shape[0]
    spec = pl.BlockSpec((tm, D), lambda i: (i, 0))
    return pl.pallas_call(
        _rms_kernel,
        grid=(S // tm,),
        in_specs=[spec, pl.BlockSpec((n, D), lambda i: (0, 0))],
        out_specs=[spec] * n,
        out_shape=[jax.ShapeDtypeStruct((S, D), BF16)] * n,
        compiler_params=_params(("parallel",), 32),
        name=name,
    )(x, gains)


def _proj_kernel(hn_ref, w_ref, hg_ref, o_ref, *, norm_lo, norm_hi, nblk, scale):
    j = pl.program_id(1)
    acc = jnp.dot(hn_ref[...], w_ref[...].astype(BF16), preferred_element_type=F32)

    def store_normed():
        o_ref[...] = _head_norm_cols(acc, hg_ref[...], scale).astype(o_ref.dtype)

    def store_raw():
        o_ref[...] = acc.astype(o_ref.dtype)

    if norm_lo == 0 and norm_hi == nblk:
        store_normed()
    elif norm_lo == norm_hi:
        store_raw()
    else:
        is_normed = jnp.logical_and(j >= norm_lo, j < norm_hi)
        pl.when(is_normed)(store_normed)
        pl.when(jnp.logical_not(is_normed))(store_raw)


def _proj(hn, w, widx, head_gains, norm_lo, norm_hi, scale, out_dtype, name, tm=2048):
    S, D = hn.shape
    N = w.shape[-1]
    tn = HEAD_BLOCK
    nblk = N // tn
    if widx is None:
        w_spec = pl.BlockSpec((D, tn), lambda i, j: (0, j))
    else:
        w_spec = pl.BlockSpec((None, D, tn), lambda i, j: (widx, 0, j))
    kern = functools.partial(_proj_kernel, norm_lo=norm_lo, norm_hi=norm_hi, nblk=nblk, scale=scale)
    return pl.pallas_call(
        kern,
        grid=(S // tm, nblk),
        in_specs=[
            pl.BlockSpec((tm, D), lambda i, j: (i, 0)),
            w_spec,
            pl.BlockSpec((None, 1, HEAD_DIM), lambda i, j: (j, 0, 0)),
        ],
        out_specs=pl.BlockSpec((tm, tn), lambda i, j: (i, j)),
        out_shape=jax.ShapeDtypeStruct((S, N), out_dtype),
        compiler_params=_params(("parallel", "arbitrary"), 48),
        name=name,
    )(hn, w, head_gains.reshape(nblk, 1, HEAD_DIM))


def _mlp_kernel(hn_ref, x_ref, w1_ref, w2_ref, *rest, n_steps, has_next):
    if has_next:
        gn_ref, w1n_ref, w2n_ref, o_ref, hno_ref, w1o_ref, w2o_ref = rest
    else:
        (o_ref,) = rest
    j = pl.program_id(1)

    @pl.when(j == 0)
    def _():
        o_ref[...] = x_ref[...]

    h = jnp.dot(hn_ref[...], w1_ref[...], preferred_element_type=F32)
    h = jnp.maximum(h, 0.0)
    h = (h * h).astype(BF16)
    o_ref[...] += jnp.dot(h, w2_ref[...], preferred_element_type=F32)

    if has_next:
        w1o_ref[...] = w1n_ref[...].astype(BF16)
        w2o_ref[...] = w2n_ref[...].astype(BF16)

        @pl.when(j == n_steps - 1)
        def _():
            hno_ref[...] = _rms_rows(o_ref[...], gn_ref[...]).astype(BF16)


def _mlp(hn, x, w1b, w2b, nxt, name, tm=1024, tf=512):
    S, D = x.shape
    F = w1b.shape[1]
    ni, nj = S // tm, F // tf
    has_next = nxt is not None
    row_spec = lambda **kw: pl.BlockSpec((tm, D), lambda i, j: (i, 0), **kw)
    in_specs = [row_spec(pipeline_mode=pl.Buffered(1)), row_spec(),
                pl.BlockSpec((D, tf), lambda i, j: (0, j)),
                pl.BlockSpec((tf, D), lambda i, j: (j, 0))]
    args = [hn, x, w1b, w2b]
    out_specs = [row_spec()]
    out_shape = [jax.ShapeDtypeStruct((S, D), F32)]
    if has_next:
        g_next, w1s, w2s, nl = nxt
        r1, r2 = D // (ni * nj), F // (ni * nj)
        in_specs += [pl.BlockSpec((1, D), lambda i, j: (0, 0)),
                     pl.BlockSpec((None, r1, F), lambda i, j: (nl, i * nj + j, 0)),
                     pl.BlockSpec((None, r2, D), lambda i, j: (nl, i * nj + j, 0))]
        args += [g_next.reshape(1, D), w1s, w2s]
        out_specs += [row_spec(pipeline_mode=pl.Buffered(1)),
                      pl.BlockSpec((r1, F), lambda i, j: (i * nj + j, 0)),
                      pl.BlockSpec((r2, D), lambda i, j: (i * nj + j, 0))]
        out_shape += [jax.ShapeDtypeStruct((S, D), BF16),
                      jax.ShapeDtypeStruct((D, F), BF16),
                      jax.ShapeDtypeStruct((F, D), BF16)]
    kern = functools.partial(_mlp_kernel, n_steps=nj, has_next=has_next)
    outs = pl.pallas_call(
        kern,
        grid=(ni, nj),
        in_specs=in_specs,
        out_specs=out_specs,
        out_shape=out_shape,
        compiler_params=_params(("parallel", "arbitrary"), 60),
        name=name,
    )(*args)
    return tuple(outs) if has_next else (outs[0], None, None, None)


def _mem_kv_kernel(mem_ref, g_ref, w_ref, kg_ref, k_ref, v_ref):
    hn = _rms_rows(mem_ref[...], g_ref[...]).astype(BF16)
    kv = jnp.dot(hn, w_ref[...].astype(BF16), preferred_element_type=F32)
    k_ref[...] = _head_norm_cols(kv[:, :MEM_WIDTH], kg_ref[...], 1.0).astype(BF16)
    v_ref[...] = kv[:, MEM_WIDTH:].astype(BF16)


def _mem_kv(mem, mem_norm_g, mem_w_kv, mem_k_norm_g):
    M, D = mem.shape
    out = jax.ShapeDtypeStruct((DEPTH, M, MEM_WIDTH), BF16)
    return pl.pallas_call(
        _mem_kv_kernel,
        grid=(DEPTH,),
        in_specs=[
            pl.BlockSpec((M, D), lambda l: (0, 0)),
            pl.BlockSpec((None, 1, D), lambda l: (l, 0, 0)),
            pl.BlockSpec((None, D, 2 * MEM_WIDTH), lambda l: (l, 0, 0)),
            pl.BlockSpec((None, 1, HEAD_DIM), lambda l: (l, 0, 0)),
        ],
        out_specs=[pl.BlockSpec((None, M, MEM_WIDTH), lambda l: (l, 0, 0)),
                   pl.BlockSpec((None, M, MEM_WIDTH), lambda l: (l, 0, 0))],
        out_shape=[out, out],
        compiler_params=_params(("arbitrary",), 40),
        name="mem_kv",
    )(mem, mem_norm_g.reshape(DEPTH, 1, D), mem_w_kv, mem_k_norm_g.reshape(DEPTH, 1, HEAD_DIM))


def _mem_attn_kernel(q_ref, k_ref, v_ref, o_ref):
    for h in range(MEM_HEADS):
        hs = slice(h * HEAD_DIM, (h + 1) * HEAD_DIM)
        q = q_ref[:, hs].astype(BF16)
        s = lax.dot_general(q, k_ref[:, hs], (((1,), (1,)), ((), ())), preferred_element_type=F32)
        m = jnp.max(s, axis=-1, keepdims=True)
        p = jnp.exp(s - m)
        den = jnp.sum(p, axis=-1, keepdims=True)
        o = jnp.dot(p.astype(BF16), v_ref[:, hs], preferred_element_type=F32) / den
        o_ref[:, hs] = o.astype(o_ref.dtype)


def _mem_attn(proj, q_col_block, mem_k, mem_v, layer, name, tm=512):
    S = proj.shape[0]
    return pl.pallas_call(
        _mem_attn_kernel,
        grid=(S // tm,),
        in_specs=[
            pl.BlockSpec((tm, MEM_WIDTH), lambda i: (i, q_col_block)),
            pl.BlockSpec((None, MEM_TOKENS, MEM_WIDTH), lambda i: (layer, 0, 0)),
            pl.BlockSpec((None, MEM_TOKENS, MEM_WIDTH), lambda i: (layer, 0, 0)),
        ],
        out_specs=pl.BlockSpec((tm, MEM_WIDTH), lambda i: (i, 0)),
        out_shape=jax.ShapeDtypeStruct((S, MEM_WIDTH), BF16),
        compiler_params=_params(("parallel",), 32),
        name=name,
    )(proj, mem_k, mem_v)


def _rglru_kernel(u_ref, gate_ref, cw_ref, cb_ref, wg_ref, bg_ref, lam_ref, y_ref, ubuf, hcar, *, T, LB):
    i = pl.program_id(1)

    @pl.when(i == 0)
    def _():
        ubuf[0:SUBLANES, :] = jnp.zeros((SUBLANES, LB), F32)
        hcar[...] = jnp.zeros_like(hcar)

    u = u_ref[...]
    ubuf[SUBLANES:T + SUBLANES, :] = u
    cw = cw_ref[...]
    xc = cb_ref[...] + (ubuf[SUBLANES - 3:T + SUBLANES - 3, :] * cw[0:1, :]
                        + ubuf[SUBLANES - 2:T + SUBLANES - 2, :] * cw[1:2, :]
                        + ubuf[SUBLANES - 1:T + SUBLANES - 1, :] * cw[2:3, :]
                        + u * cw[3:4, :])
    ubuf[0:SUBLANES, :] = u[T - SUBLANES:T, :]

    lam = lam_ref[...]
    sp = jnp.maximum(-lam, 0.0) + jnp.log1p(jnp.exp(-jnp.abs(lam)))
    row = lax.broadcasted_iota(jnp.int32, (T // SUBLANES, SUBLANES, LANES), 1)

    for c in range(LB // LANES):
        cs = slice(c * LANES, (c + 1) * LANES)
        xcc = xc[:, cs]
        z = jnp.dot(xcc.astype(BF16), wg_ref[c], preferred_element_type=F32) + bg_ref[c]
        r = jax.nn.sigmoid(z[:, :LANES])
        ig = jax.nn.sigmoid(z[:, LANES:])
        log_a = (-RG_C * r) * sp[:, cs]
        a = jnp.exp(log_a)
        b = jnp.sqrt(1.0 - a * a) * (ig * xcc)

        a3 = a.reshape(T // SUBLANES, SUBLANES, LANES)
        b3 = b.reshape(T // SUBLANES, SUBLANES, LANES)
        for s in (1, 2, 4):
            a_sh = pltpu.roll(a3, s, axis=1)
            b_sh = pltpu.roll(b3, s, axis=1)
            keep = row >= s
            b3 = jnp.where(keep, a3 * b_sh + b3, b3)
            a3 = jnp.where(keep, a3 * a_sh, a3)

        hprev = hcar[:, cs]
        hs = []
        for grp in range(T // SUBLANES):
            hg = a3[grp] * hprev + b3[grp]
            hprev = hg[SUBLANES - 1:SUBLANES, :]
            hs.append(hg)
        hcar[:, cs] = hprev
        h = jnp.concatenate(hs, axis=0)

        gt = gate_ref[:, cs]
        gelu = 0.5 * gt * (1.0 + jnp.tanh(math.sqrt(2.0 / math.pi) * (gt + 0.044715 * (gt * gt * gt))))
        y_ref[:, cs] = (h * gelu).astype(y_ref.dtype)


def _rglru(proj, conv_w, conv_b, wg, bg, lam, name, T=512, LB=512):
    S = proj.shape[0]
    W = LRU_WIDTH
    ncb = W // LB
    kern = functools.partial(_rglru_kernel, T=T, LB=LB)
    return pl.pallas_call(
        kern,
        grid=(ncb, S // T),
        in_specs=[
            pl.BlockSpec((T, LB), lambda n, i: (i, n)),
            pl.BlockSpec((T, LB), lambda n, i: (i, ncb + n)),
            pl.BlockSpec((CONV_WIDTH, LB), lambda n, i: (0, n)),
            pl.BlockSpec((1, LB), lambda n, i: (0, n)),
            pl.BlockSpec((LB // LANES, LRU_BLOCK_WIDTH, 2 * LRU_BLOCK_WIDTH), lambda n, i: (n, 0, 0)),
            pl.BlockSpec((LB // LANES, 1, 2 * LRU_BLOCK_WIDTH), lambda n, i: (n, 0, 0)),
            pl.BlockSpec((1, LB), lambda n, i: (0, n)),
        ],
        out_specs=pl.BlockSpec((T, LB), lambda n, i: (i, n)),
        out_shape=jax.ShapeDtypeStruct((S, W), BF16),
        scratch_shapes=[pltpu.VMEM((T + SUBLANES, LB), F32), pltpu.VMEM((1, LB), F32)],
        compiler_params=_params(("arbitrary", "arbitrary"), 32),
        name=name,
    )(proj, proj, conv_w, conv_b.reshape(1, W), wg, bg, lam.reshape(1, W))


def _cast_weight_once(w_ref, wb_ref):
    @pl.when(pl.program_id(0) == 0)
    def _():
        wb_ref[...] = w_ref[...].astype(BF16)


def _residual_and_norm(x_ref, acc, g_ref, o_ref, hn_ref):
    xn = x_ref[...] + acc
    o_ref[...] = xn
    hn_ref[...] = _rms_rows(xn, g_ref[...]).astype(BF16)


def _out_a_kernel(x_ref, y_ref, m_ref, w_ref, g_ref, o_ref, hn_ref, wb_ref):
    _cast_weight_once(w_ref, wb_ref)
    acc = jnp.dot(y_ref[...], wb_ref[0:LRU_WIDTH, :], preferred_element_type=F32)
    acc += jnp.dot(m_ref[...], wb_ref[LRU_WIDTH:LRU_WIDTH + MEM_WIDTH, :], preferred_element_type=F32)
    _residual_and_norm(x_ref, acc, g_ref, o_ref, hn_ref)


def _out_a(x, y, mem_out, w, widx, g_mlp, name, tm=512):
    S, D = x.shape
    K = LRU_WIDTH + MEM_WIDTH
    row = pl.BlockSpec((tm, D), lambda i: (i, 0))
    return pl.pallas_call(
        _out_a_kernel,
        grid=(S // tm,),
        in_specs=[
            row,
            pl.BlockSpec((tm, LRU_WIDTH), lambda i: (i, 0)),
            pl.BlockSpec((tm, MEM_WIDTH), lambda i: (i, 0)),
            pl.BlockSpec((None, K, D), lambda i: (widx, 0, 0), pipeline_mode=pl.Buffered(1)),
            pl.BlockSpec((1, D), lambda i: (0, 0)),
        ],
        out_specs=[row, row],
        out_shape=[jax.ShapeDtypeStruct((S, D), F32), jax.ShapeDtypeStruct((S, D), BF16)],
        scratch_shapes=[pltpu.VMEM((K, D), BF16)],
        compiler_params=_params(("arbitrary",), 56),
        name=name,
    )(x, y, mem_out, w, g_mlp.reshape(1, D))


def _bucket_tables():
    qi = np.arange(Q_BLOCK)[:, None]
    ki = np.arange(2 * Q_BLOCK)[None, :]
    u = qi + Q_BLOCK - ki
    tables, valids = [], []
    for window, dilation in DIL_PATTERNS:
        n = np.maximum(u * dilation, 0)
        nf = np.maximum(n, 1).astype(np.float32)
        large = REL_MAX_EXACT + (np.log(nf / np.float32(REL_MAX_EXACT))
                                 / np.float32(math.log(REL_MAX_DISTANCE / REL_MAX_EXACT))
                                 * np.float32(REL_BUCKETS - REL_MAX_EXACT)).astype(np.int32)
        large = np.minimum(large, REL_BUCKETS - 1)
        tables.append(np.where(n < REL_MAX_EXACT, n, large).astype(np.int32))
        valids.append(((u >= 0) & (u <= window // dilation)).astype(np.int32))
    return np.stack(tables), np.stack(valids)


def _bias_kernel(tab_ref, idx_ref, valid_ref, o_ref):
    g = pl.program_id(0)
    idx = idx_ref[...]
    valid = valid_ref[...] > 0
    for h in range(DIL_HEADS):
        acc = jnp.zeros(idx.shape, F32)
        for b in range(REL_BUCKETS):
            acc = jnp.where(idx == b, tab_ref[b, g * DIL_HEADS + h], acc)
        o_ref[h] = jnp.where(valid, acc, NEG_INF)


def _bias_expand(rel_bias):
    idx, valid = _bucket_tables()
    blk = pl.BlockSpec((None, Q_BLOCK, 2 * Q_BLOCK), lambda g: (g, 0, 0))
    return pl.pallas_call(
        _bias_kernel,
        grid=(DIL_GROUPS,),
        in_specs=[pl.BlockSpec(memory_space=pltpu.SMEM), blk, blk],
        out_specs=pl.BlockSpec((DIL_HEADS, Q_BLOCK, 2 * Q_BLOCK), lambda g: (g, 0, 0)),
        out_shape=jax.ShapeDtypeStruct((DIL_GROUPS * DIL_HEADS, Q_BLOCK, 2 * Q_BLOCK), F32),
        compiler_params=_params(("arbitrary",), 32),
        name="bias_expand",
    )(rel_bias, jnp.asarray(idx), jnp.asarray(valid))


def _dil_kernel(q_ref, kc_ref, kp_ref, vc_ref, vp_ref, b_ref, o_ref, *, LQ):
    is_first = pl.program_id(1) == 0
    for h in range(DIL_HEADS):
        hs = slice(h * HEAD_DIM, (h + 1) * HEAD_DIM)
        for sb in range(LQ // Q_BLOCK):
            rows = slice(sb * Q_BLOCK, (sb + 1) * Q_BLOCK)
            q = q_ref[rows, hs]
            if sb == 0:
                kk = jnp.concatenate([kp_ref[:, hs], kc_ref[0:Q_BLOCK, hs]], axis=0)
                vv = jnp.concatenate([vp_ref[:, hs], vc_ref[0:Q_BLOCK, hs]], axis=0)
            else:
                kk = kc_ref[(sb - 1) * Q_BLOCK:(sb + 1) * Q_BLOCK, hs]
                vv = vc_ref[(sb - 1) * Q_BLOCK:(sb + 1) * Q_BLOCK, hs]
            s = lax.dot_general(q, kk, (((1,), (1,)), ((), ())), preferred_element_type=F32) + b_ref[h]
            if sb == 0:
                col = lax.broadcasted_iota(jnp.int32, s.shape, 1)
                s = jnp.where(jnp.logical_and(is_first, col < Q_BLOCK), NEG_INF, s)
            m = jnp.max(s, axis=-1, keepdims=True)
            p = jnp.exp(s - m)
            den = jnp.sum(p, axis=-1, keepdims=True)
            o = jnp.dot(p.astype(BF16), vv, preferred_element_type=F32) / den
            lse = m + jnp.log(den)
            o_ref[rows, hs] = o
            o_ref[rows, DIL_OUT_WIDTH + h * HEAD_DIM:DIL_OUT_WIDTH + (h + 1) * HEAD_DIM] = (
                jnp.broadcast_to(lse, (Q_BLOCK, HEAD_DIM)))


def _dil_attn(qb, kvb, bias, g, name, LQ=512):
    S = qb.shape[0]
    d = DIL_PATTERNS[g][1]
    L = S // d
    qw, kvw, ow = qb.shape[1], kvb.shape[1], 2 * DIL_OUT_WIDTH
    qv = qb.reshape(L, d * qw)
    kvv = kvb.reshape(L, d * kvw)
    qcb, kvcb = qw // HEAD_BLOCK, kvw // HEAD_BLOCK
    ratio = LQ // Q_BLOCK
    kern = functools.partial(_dil_kernel, LQ=LQ)
    prev_row = lambda n: jnp.maximum(n * ratio - 1, 0)
    out = pl.pallas_call(
        kern,
        grid=(d, L // LQ),
        in_specs=[
            pl.BlockSpec((LQ, HEAD_BLOCK), lambda r, n: (n, r * qcb + g)),
            pl.BlockSpec((LQ, HEAD_BLOCK), lambda r, n: (n, r * kvcb + g)),
            pl.BlockSpec((Q_BLOCK, HEAD_BLOCK), lambda r, n: (prev_row(n), r * kvcb + g)),
            pl.BlockSpec((LQ, HEAD_BLOCK), lambda r, n: (n, r * kvcb + DIL_GROUPS + g)),
            pl.BlockSpec((Q_BLOCK, HEAD_BLOCK), lambda r, n: (prev_row(n), r * kvcb + DIL_GROUPS + g)),
            pl.BlockSpec((DIL_HEADS, Q_BLOCK, 2 * Q_BLOCK), lambda r, n: (g, 0, 0)),
        ],
        out_specs=pl.BlockSpec((LQ, ow), lambda r, n: (n, r)),
        out_shape=jax.ShapeDtypeStruct((L, d * ow), F32),
        compiler_params=_params(("arbitrary", "arbitrary"), 32),
        name=name,
    )(qv, kvv, kvv, kvv, kvv, bias)
    return out.reshape(S, ow)


def _out_b_kernel(x_ref, o0_ref, o1_ref, o2_ref, m_ref, w_ref, g_ref, o_ref, hn_ref, wb_ref):
    _cast_weight_once(w_ref, wb_ref)
    refs = (o0_ref, o1_ref, o2_ref)
    parts = []
    for h in range(DIL_HEADS):
        hs = slice(h * HEAD_DIM, (h + 1) * HEAD_DIM)
        ls = slice(DIL_OUT_WIDTH + h * HEAD_DIM, DIL_OUT_WIDTH + (h + 1) * HEAD_DIM)
        lses = [r[:, ls] for r in refs]
        mx = jnp.maximum(jnp.maximum(lses[0], lses[1]), lses[2])
        es = [jnp.exp(l - mx) for l in lses]
        den = es[0] + es[1] + es[2]
        num = es[0] * refs[0][:, hs] + es[1] * refs[1][:, hs] + es[2] * refs[2][:, hs]
        parts.append((num / den).astype(BF16))
    dil = jnp.concatenate(parts, axis=-1)
    acc = jnp.dot(dil, wb_ref[0:DIL_OUT_WIDTH, :], preferred_element_type=F32)
    acc += jnp.dot(m_ref[...], wb_ref[DIL_OUT_WIDTH:DIL_OUT_WIDTH + MEM_WIDTH, :], preferred_element_type=F32)
    _residual_and_norm(x_ref, acc, g_ref, o_ref, hn_ref)


def _out_b(x, ogs, mem_out, w, widx, g_mlp, name, tm=512):
    S, D = x.shape
    ow = 2 * DIL_OUT_WIDTH
    K = DIL_OUT_WIDTH + MEM_WIDTH
    og_spec = pl.BlockSpec((tm, ow), lambda i: (i, 0))
    row = pl.BlockSpec((tm, D), lambda i: (i, 0))
    return pl.pallas_call(
        _out_b_kernel,
        grid=(S // tm,),
        in_specs=[
            row,
            og_spec, og_spec, og_spec,
            pl.BlockSpec((tm, MEM_WIDTH), lambda i: (i, 0)),
            pl.BlockSpec((None, K, D), lambda i: (widx, 0, 0), pipeline_mode=pl.Buffered(1)),
            pl.BlockSpec((1, D), lambda i: (0, 0)),
        ],
        out_specs=[row, row],
        out_shape=[jax.ShapeDtypeStruct((S, D), F32), jax.ShapeDtypeStruct((S, D), BF16)],
        scratch_shapes=[pltpu.VMEM((K, D), BF16)],
        compiler_params=_params(("arbitrary",), 56),
        name=name,
    )(x, ogs[0], ogs[1], ogs[2], mem_out, w, g_mlp.reshape(1, D))


def kernel(x, mem, norm_mix_g, norm_mlp_g, mlp_w1, mlp_w2, mem_norm_g, mem_w_kv, mem_q_norm_g, mem_k_norm_g, a_w_in, a_conv_w, a_conv_b, a_gate_r_w, a_gate_r_b, a_gate_i_w, a_gate_i_b, a_lambda, a_w_out, kv_norm_g, kv_w, k_norm_g, rel_bias, b_w_q, b_q_norm_g, b_w_out):
    B, S, D = x.shape
    assert (B, S, D) == (1, SEQ, D_MODEL) and mem.shape == (1, MEM_TOKENS, D_MODEL)
    xs = x[0]
    ones = jnp.ones((1, HEAD_DIM), F32)

    mem_k, mem_v = _mem_kv(mem[0], mem_norm_g, mem_w_kv, mem_k_norm_g)
    bias = _bias_expand(rel_bias)
    kvb = None
    (hn,) = _rms(xs, norm_mix_g[0:1], "rms_first")
    w1b, w2b = mlp_w1[0].astype(BF16), mlp_w2[0].astype(BF16)

    for layer in range(DEPTH):
        mq_gain = mem_q_norm_g[layer][None, :]
        if layer < N_A_LAYERS:
            i = layer
            nblk = A_IN_WIDTH // HEAD_BLOCK
            gains = jnp.concatenate([jnp.tile(ones, (nblk - 1, 1)), mq_gain], axis=0)
            proj = _proj(hn, a_w_in, i, gains, nblk - 1, nblk, ATTN_SCALE, F32, f"a{i}_in_proj")
            mem_out = _mem_attn(proj, nblk - 1, mem_k, mem_v, layer, f"a{i}_mem_attn")
            wg = jnp.concatenate([a_gate_r_w[i], a_gate_i_w[i]], axis=-1).astype(BF16)
            bg = jnp.concatenate([a_gate_r_b[i], a_gate_i_b[i]], axis=-1)[:, None, :]
            y = _rglru(proj, a_conv_w[i], a_conv_b[i], wg, bg, a_lambda[i], f"a{i}_rglru")
            xs, hn = _out_a(xs, y, mem_out, a_w_out, i, norm_mlp_g[layer], f"a{i}_out_proj")
        else:
            j = layer - N_A_LAYERS
            if j == 0:
                (hn_kv,) = _rms(xs, kv_norm_g[None, :], "rms_kv")
                kv_gains = jnp.concatenate([k_norm_g, jnp.tile(ones, (DIL_GROUPS, 1))], axis=0)
                kvb = _proj(hn_kv, kv_w, None, kv_gains, 0, DIL_GROUPS, 1.0, BF16, "shared_kv")
            q_gains = jnp.concatenate([b_q_norm_g[j], mq_gain], axis=0)
            qb = _proj(hn, b_w_q, j, q_gains, 0, DIL_GROUPS + 1, ATTN_SCALE, BF16, f"b{j}_q_proj")
            mem_out = _mem_attn(qb, DIL_GROUPS, mem_k, mem_v, layer, f"b{j}_mem_attn")
            ogs = [_dil_attn(qb, kvb, bias, g, f"b{j}_dil{g}") for g in range(DIL_GROUPS)]
            xs, hn = _out_b(xs, ogs, mem_out, b_w_out, j, norm_mlp_g[layer], f"b{j}_out_proj")
        nxt = (norm_mix_g[layer + 1], mlp_w1, mlp_w2, layer + 1) if layer + 1 < DEPTH else None
        xs, hn, w1b, w2b = _mlp(hn, xs, w1b, w2b, nxt, f"mlp{layer}")
    return xs[None]
```

```python
import functools
import math

import numpy as np
import jax
import jax.numpy as jnp
from jax import lax
from jax.experimental import pallas as pl
from jax.experimental.pallas import tpu as pltpu

F32 = jnp.float32
BF16 = jnp.bfloat16

D_MODEL = 2048
SEQ = 8192
DEPTH = 4
N_A_LAYERS = DEPTH // 2
HEAD_DIM = 128
D_FF = 4 * D_MODEL
LRU_WIDTH = 3 * D_MODEL // 4
LRU_BLOCK_WIDTH = 128
LRU_BLOCKS = LRU_WIDTH // LRU_BLOCK_WIDTH
CONV_WIDTH = 4
RG_C = 8.0
MEM_TOKENS = 256
MEM_HEADS = 4
MEM_WIDTH = MEM_HEADS * HEAD_DIM
DIL_PATTERNS = ((128, 1), (512, 4), (2048, 16))
DIL_GROUPS = len(DIL_PATTERNS)
DIL_HEADS = 4
DIL_WIDTH = DIL_GROUPS * DIL_HEADS * HEAD_DIM
DIL_OUT_WIDTH = DIL_HEADS * HEAD_DIM
Q_BLOCK = 128
REL_BUCKETS = 32
REL_MAX_EXACT = REL_BUCKETS // 2
REL_MAX_DISTANCE = 2048
A_IN_WIDTH = 2 * LRU_WIDTH + MEM_WIDTH
NORM_EPS = 1e-6
NEG_INF = -1e30
ATTN_SCALE = HEAD_DIM ** -0.5

V7X_VMEM_BYTES = 64 * 1024 * 1024
LANES = 128
SUBLANES = 8
HEAD_BLOCK = DIL_HEADS * HEAD_DIM


def _params(semantics, vmem_mib):
    assert vmem_mib * 1024 * 1024 < V7X_VMEM_BYTES
    return pltpu.CompilerParams(dimension_semantics=semantics,
                                vmem_limit_bytes=vmem_mib * 1024 * 1024)


def _rms_rows(x, g):
    ms = jnp.mean(x * x, axis=-1, keepdims=True)
    return x * lax.rsqrt(ms + NORM_EPS) * g


def _head_norm_cols(a, g, scale):
    parts = []
    for h in range(a.shape[1] // HEAD_DIM):
        ah = a[:, h * HEAD_DIM:(h + 1) * HEAD_DIM]
        y = ah * lax.rsqrt(jnp.mean(ah * ah, axis=-1, keepdims=True) + NORM_EPS) * g
        parts.append(y * scale if scale != 1.0 else y)
    return jnp.concatenate(parts, axis=-1)


def _rms_kernel(x_ref, g_ref, *o_refs):
    x = x_ref[...]
    xh = x * lax.rsqrt(jnp.mean(x * x, axis=-1, keepdims=True) + NORM_EPS)
    for k, o_ref in enumerate(o_refs):
        o_ref[...] = (xh * g_ref[k:k + 1, :]).astype(BF16)


def _rms(x, gains, name, tm=512):
    S, D = x.shape
    n = gains.shape[0]
    spec = pl.BlockSpec((tm, D), lambda i: (i, 0))
    return pl.pallas_call(
        _rms_kernel,
        grid=(S // tm,),
        in_specs=[spec, pl.BlockSpec((n, D), lambda i: (0, 0))],
        out_specs=[spec] * n,
        out_shape=[jax.ShapeDtypeStruct((S, D), BF16)] * n,
        compiler_params=_params(("parallel",), 32),
        name=name,
    )(x, gains)


def _proj_kernel(hn_ref, w_ref, hg_ref, o_ref, *, norm_lo, norm_hi, nblk, scale):
    j = pl.program_id(1)
    acc = jnp.dot(hn_ref[...], w_ref[...].astype(BF16), preferred_element_type=F32)

    def store_normed():
        o_ref[...] = _head_norm_cols(acc, hg_ref[...], scale).astype(o_ref.dtype)

    def store_raw():
        o_ref[...] = acc.astype(o_ref.dtype)

    if norm_lo == 0 and norm_hi == nblk:
        store_normed()
    elif norm_lo == norm_hi:
        store_raw()
    else:
        is_normed = jnp.logical_and(j >= norm_lo, j < norm_hi)
        pl.when(is_normed)(store_normed)
        pl.when(jnp.logical_not(is_normed))(store_raw)


def _proj(hn, w, widx, head_gains, norm_lo, norm_hi, scale, out_dtype, name, tm=2048):
    S, D = hn.shape
    N = w.shape[-1]
    tn = HEAD_BLOCK
    nblk = N // tn
    if widx is None:
        w_spec = pl.BlockSpec((D, tn), lambda i, j: (0, j))
    else:
        w_spec = pl.BlockSpec((None, D, tn), lambda i, j: (widx, 0, j))
    kern = functools.partial(_proj_kernel, norm_lo=norm_lo, norm_hi=norm_hi, nblk=nblk, scale=scale)
    return pl.pallas_call(
        kern,
        grid=(S // tm, nblk),
        in_specs=[
            pl.BlockSpec((tm, D), lambda i, j: (i, 0)),
            w_spec,
            pl.BlockSpec((None, 1, HEAD_DIM), lambda i, j: (j, 0, 0)),
        ],
        out_specs=pl.BlockSpec((tm, tn), lambda i, j: (i, j)),
        out_shape=jax.ShapeDtypeStruct((S, N), out_dtype),
        compiler_params=_params(("parallel", "arbitrary"), 48),
        name=name,
    )(hn, w, head_gains.reshape(nblk, 1, HEAD_DIM))


def _proj_split_kernel(hn_ref, w_ref, hg_ref, *rest, dils, normed, scale, tm):
    o_refs, scr = rest[:len(dils)], rest[len(dils)]
    for jb, (d, o_ref) in enumerate(zip(dils, o_refs)):
        acc = jnp.dot(hn_ref[...], w_ref[:, jb * HEAD_BLOCK:(jb + 1) * HEAD_BLOCK],
                      preferred_element_type=F32)
        if normed[jb]:
            acc = _head_norm_cols(acc, hg_ref[jb:jb + 1, :], scale)
        if d == 1:
            o_ref[0] = acc.astype(o_ref.dtype)
        else:
            for h in range(HEAD_BLOCK // LANES):
                scr[h] = acc[:, h * LANES:(h + 1) * LANES]
            for r in range(d):
                for h in range(HEAD_BLOCK // LANES):
                    o_ref[r, :, h * LANES:(h + 1) * LANES] = (
                        scr[h, pl.ds(r, tm // d, stride=d), :].astype(o_ref.dtype))


def _proj_split(hn, w, head_gains, dils, normed, scale, name, tm=1024):
    S, D = hn.shape
    N = w.shape[1]
    nblk = N // HEAD_BLOCK
    assert len(dils) == len(normed) == nblk
    kern = functools.partial(_proj_split_kernel, dils=dils, normed=normed, scale=scale, tm=tm)
    return pl.pallas_call(
        kern,
        grid=(S // tm,),
        in_specs=[
            pl.BlockSpec((tm, D), lambda i: (i, 0)),
            pl.BlockSpec((D, N), lambda i: (0, 0), pipeline_mode=pl.Buffered(1)),
            pl.BlockSpec((nblk, HEAD_DIM), lambda i: (0, 0)),
        ],
        out_specs=[pl.BlockSpec((d, tm // d, HEAD_BLOCK), lambda i: (0, i, 0)) for d in dils],
        out_shape=[jax.ShapeDtypeStruct((d, S // d, HEAD_BLOCK), BF16) for d in dils],
        scratch_shapes=[pltpu.VMEM((HEAD_BLOCK // LANES, tm, LANES), F32)],
        compiler_params=_params(("parallel",), 48),
        name=name,
    )(hn, w, head_gains)


def _mlp_kernel(hn_ref, x_ref, w1_ref, w2_ref, *rest, n_steps, has_next):
    if has_next:
        gn_ref, w1n_ref, w2n_ref, o_ref, hno_ref, w1o_ref, w2o_ref = rest
    else:
        (o_ref,) = rest
    j = pl.program_id(1)

    @pl.when(j == 0)
    def _():
        o_ref[...] = x_ref[...]

    h = jnp.dot(hn_ref[...], w1_ref[...], preferred_element_type=F32)
    h = jnp.maximum(h, 0.0)
    h = (h * h).astype(BF16)
    o_ref[...] += jnp.dot(h, w2_ref[...], preferred_element_type=F32)

    if has_next:
        w1o_ref[...] = w1n_ref[...].astype(BF16)
        w2o_ref[...] = w2n_ref[...].astype(BF16)

        @pl.when(j == n_steps - 1)
        def _():
            hno_ref[...] = _rms_rows(o_ref[...], gn_ref[...]).astype(BF16)


def _mlp(hn, x, w1b, w2b, nxt, name, tm=1024, tf=512):
    S, D = x.shape
    F = w1b.shape[1]
    ni, nj = S // tm, F // tf
    has_next = nxt is not None
    row_spec = lambda **kw: pl.BlockSpec((tm, D), lambda i, j: (i, 0), **kw)
    in_specs = [row_spec(pipeline_mode=pl.Buffered(1)), row_spec(),
                pl.BlockSpec((D, tf), lambda i, j: (0, j)),
                pl.BlockSpec((tf, D), lambda i, j: (j, 0))]
    args = [hn, x, w1b, w2b]
    out_specs = [row_spec()]
    out_shape = [jax.ShapeDtypeStruct((S, D), F32)]
    if has_next:
        g_next, w1s, w2s, nl = nxt
        r1, r2 = D // (ni * nj), F // (ni * nj)
        in_specs += [pl.BlockSpec((1, D), lambda i, j: (0, 0)),
                     pl.BlockSpec((None, r1, F), lambda i, j: (nl, i * nj + j, 0)),
                     pl.BlockSpec((None, r2, D), lambda i, j: (nl, i * nj + j, 0))]
        args += [g_next.reshape(1, D), w1s, w2s]
        out_specs += [row_spec(pipeline_mode=pl.Buffered(1)),
                      pl.BlockSpec((r1, F), lambda i, j: (i * nj + j, 0)),
                      pl.BlockSpec((r2, D), lambda i, j: (i * nj + j, 0))]
        out_shape += [jax.ShapeDtypeStruct((S, D), BF16),
                      jax.ShapeDtypeStruct((D, F), BF16),
                      jax.ShapeDtypeStruct((F, D), BF16)]
    kern = functools.partial(_mlp_kernel, n_steps=nj, has_next=has_next)
    outs = pl.pallas_call(
        kern,
        grid=(ni, nj),
        in_specs=in_specs,
        out_specs=out_specs,
        out_shape=out_shape,
        compiler_params=_params(("parallel", "arbitrary"), 60),
        name=name,
    )(*args)
    return tuple(outs) if has_next else (outs[0], None, None, None)


def _mem_kv_kernel(mem_ref, g_ref, w_ref, kg_ref, k_ref, v_ref):
    hn = _rms_rows(mem_ref[...], g_ref[...]).astype(BF16)
    kv = jnp.dot(hn, w_ref[...].astype(BF16), preferred_element_type=F32)
    k_ref[...] = _head_norm_cols(kv[:, :MEM_WIDTH], kg_ref[...], 1.0).astype(BF16)
    v_ref[...] = kv[:, MEM_WIDTH:].astype(BF16)


def _mem_kv(mem, mem_norm_g, mem_w_kv, mem_k_norm_g):
    M, D = mem.shape
    out = jax.ShapeDtypeStruct((DEPTH, M, MEM_WIDTH), BF16)
    return pl.pallas_call(
        _mem_kv_kernel,
        grid=(DEPTH,),
        in_specs=[
            pl.BlockSpec((M, D), lambda l: (0, 0)),
            pl.BlockSpec((None, 1, D), lambda l: (l, 0, 0)),
            pl.BlockSpec((None, D, 2 * MEM_WIDTH), lambda l: (l, 0, 0)),
            pl.BlockSpec((None, 1, HEAD_DIM), lambda l: (l, 0, 0)),
        ],
        out_specs=[pl.BlockSpec((None, M, MEM_WIDTH), lambda l: (l, 0, 0)),
                   pl.BlockSpec((None, M, MEM_WIDTH), lambda l: (l, 0, 0))],
        out_shape=[out, out],
        compiler_params=_params(("arbitrary",), 40),
        name="mem_kv",
    )(mem, mem_norm_g.reshape(DEPTH, 1, D), mem_w_kv, mem_k_norm_g.reshape(DEPTH, 1, HEAD_DIM))


def _mem_attn_kernel(q_ref, k_ref, v_ref, o_ref):
    for h in range(MEM_HEADS):
        hs = slice(h * HEAD_DIM, (h + 1) * HEAD_DIM)
        q = q_ref[:, hs].astype(BF16)
        s = lax.dot_general(q, k_ref[:, hs], (((1,), (1,)), ((), ())), preferred_element_type=F32)
        m = jnp.max(s, axis=-1, keepdims=True)
        p = jnp.exp(s - m)
        den = jnp.sum(p, axis=-1, keepdims=True)
        o = jnp.dot(p.astype(BF16), v_ref[:, hs], preferred_element_type=F32) / den
        o_ref[:, hs] = o.astype(o_ref.dtype)


def _mem_attn(proj, q_col_block, mem_k, mem_v, layer, name, tm=512):
    S = proj.shape[0]
    return pl.pallas_call(
        _mem_attn_kernel,
        grid=(S // tm,),
        in_specs=[
            pl.BlockSpec((tm, MEM_WIDTH), lambda i: (i, q_col_block)),
            pl.BlockSpec((None, MEM_TOKENS, MEM_WIDTH), lambda i: (layer, 0, 0)),
            pl.BlockSpec((None, MEM_TOKENS, MEM_WIDTH), lambda i: (layer, 0, 0)),
        ],
        out_specs=pl.BlockSpec((tm, MEM_WIDTH), lambda i: (i, 0)),
        out_shape=jax.ShapeDtypeStruct((S, MEM_WIDTH), BF16),
        compiler_params=_params(("parallel",), 32),
        name=name,
    )(proj, mem_k, mem_v)


def _rglru_kernel(u_ref, gate_ref, cw_ref, cb_ref, wg_ref, bg_ref, lam_ref, y_ref, ubuf, hcar, *, T, LB):
    i = pl.program_id(1)

    @pl.when(i == 0)
    def _():
        ubuf[0:SUBLANES, :] = jnp.zeros((SUBLANES, LB), F32)
        hcar[...] = jnp.zeros_like(hcar)

    u = u_ref[...]
    ubuf[SUBLANES:T + SUBLANES, :] = u
    cw = cw_ref[...]
    xc = cb_ref[...] + (ubuf[SUBLANES - 3:T + SUBLANES - 3, :] * cw[0:1, :]
                        + ubuf[SUBLANES - 2:T + SUBLANES - 2, :] * cw[1:2, :]
                        + ubuf[SUBLANES - 1:T + SUBLANES - 1, :] * cw[2:3, :]
                        + u * cw[3:4, :])
    ubuf[0:SUBLANES, :] = u[T - SUBLANES:T, :]

    lam = lam_ref[...]
    sp = jnp.maximum(-lam, 0.0) + jnp.log1p(jnp.exp(-jnp.abs(lam)))
    row = lax.broadcasted_iota(jnp.int32, (T // SUBLANES, SUBLANES, LANES), 1)

    for c in range(LB // LANES):
        cs = slice(c * LANES, (c + 1) * LANES)
        xcc = xc[:, cs]
        z = jnp.dot(xcc.astype(BF16), wg_ref[c], preferred_element_type=F32) + bg_ref[c]
        r = jax.nn.sigmoid(z[:, :LANES])
        ig = jax.nn.sigmoid(z[:, LANES:])
        log_a = (-RG_C * r) * sp[:, cs]
        a = jnp.exp(log_a)
        b = jnp.sqrt(1.0 - a * a) * (ig * xcc)

        a3 = a.reshape(T // SUBLANES, SUBLANES, LANES)
        b3 = b.reshape(T // SUBLANES, SUBLANES, LANES)
        for s in (1, 2, 4):
            a_sh = pltpu.roll(a3, s, axis=1)
            b_sh = pltpu.roll(b3, s, axis=1)
            keep = row >= s
            b3 = jnp.where(keep, a3 * b_sh + b3, b3)
            a3 = jnp.where(keep, a3 * a_sh, a3)

        hprev = hcar[:, cs]
        hs = []
        for grp in range(T // SUBLANES):
            hg = a3[grp] * hprev + b3[grp]
            hprev = hg[SUBLANES - 1:SUBLANES, :]
            hs.append(hg)
        hcar[:, cs] = hprev
        h = jnp.concatenate(hs, axis=0)

        gt = gate_ref[:, cs]
        gelu = 0.5 * gt * (1.0 + jnp.tanh(math.sqrt(2.0 / math.pi) * (gt + 0.044715 * (gt * gt * gt))))
        y_ref[:, cs] = (h * gelu).astype(y_ref.dtype)


def _rglru(proj, conv_w, conv_b, wg, bg, lam, name, T=512, LB=512):
    S = proj.shape[0]
    W = LRU_WIDTH
    ncb = W // LB
    kern = functools.partial(_rglru_kernel, T=T, LB=LB)
    return pl.pallas_call(
        kern,
        grid=(ncb, S // T),
        in_specs=[
            pl.BlockSpec((T, LB), lambda n, i: (i, n)),
            pl.BlockSpec((T, LB), lambda n, i: (i, ncb + n)),
            pl.BlockSpec((CONV_WIDTH, LB), lambda n, i: (0, n)),
            pl.BlockSpec((1, LB), lambda n, i: (0, n)),
            pl.BlockSpec((LB // LANES, LRU_BLOCK_WIDTH, 2 * LRU_BLOCK_WIDTH), lambda n, i: (n, 0, 0)),
            pl.BlockSpec((LB // LANES, 1, 2 * LRU_BLOCK_WIDTH), lambda n, i: (n, 0, 0)),
            pl.BlockSpec((1, LB), lambda n, i: (0, n)),
        ],
        out_specs=pl.BlockSpec((T, LB), lambda n, i: (i, n)),
        out_shape=jax.ShapeDtypeStruct((S, W), BF16),
        scratch_shapes=[pltpu.VMEM((T + SUBLANES, LB), F32), pltpu.VMEM((1, LB), F32)],
        compiler_params=_params(("arbitrary", "arbitrary"), 32),
        name=name,
    )(proj, proj, conv_w, conv_b.reshape(1, W), wg, bg, lam.reshape(1, W))


def _cast_weight_once(w_ref, wb_ref):
    @pl.when(pl.program_id(0) == 0)
    def _():
        wb_ref[...] = w_ref[...].astype(BF16)


def _residual_and_norm(x_ref, acc, g_ref, o_ref, hn_ref):
    xn = x_ref[...] + acc
    o_ref[...] = xn
    hn_ref[...] = _rms_rows(xn, g_ref[...]).astype(BF16)


def _out_a_kernel(x_ref, y_ref, m_ref, w_ref, g_ref, o_ref, hn_ref, wb_ref):
    _cast_weight_once(w_ref, wb_ref)
    acc = jnp.dot(y_ref[...], wb_ref[0:LRU_WIDTH, :], preferred_element_type=F32)
    acc += jnp.dot(m_ref[...], wb_ref[LRU_WIDTH:LRU_WIDTH + MEM_WIDTH, :], preferred_element_type=F32)
    _residual_and_norm(x_ref, acc, g_ref, o_ref, hn_ref)


def _out_a(x, y, mem_out, w, widx, g_mlp, name, tm=512):
    S, D = x.shape
    K = LRU_WIDTH + MEM_WIDTH
    row = pl.BlockSpec((tm, D), lambda i: (i, 0))
    return pl.pallas_call(
        _out_a_kernel,
        grid=(S // tm,),
        in_specs=[
            row,
            pl.BlockSpec((tm, LRU_WIDTH), lambda i: (i, 0)),
            pl.BlockSpec((tm, MEM_WIDTH), lambda i: (i, 0)),
            pl.BlockSpec((None, K, D), lambda i: (widx, 0, 0), pipeline_mode=pl.Buffered(1)),
            pl.BlockSpec((1, D), lambda i: (0, 0)),
        ],
        out_specs=[row, row],
        out_shape=[jax.ShapeDtypeStruct((S, D), F32), jax.ShapeDtypeStruct((S, D), BF16)],
        scratch_shapes=[pltpu.VMEM((K, D), BF16)],
        compiler_params=_params(("arbitrary",), 56),
        name=name,
    )(x, y, mem_out, w, g_mlp.reshape(1, D))


def _bucket_tables():
    qi = np.arange(Q_BLOCK)[:, None]
    ki = np.arange(2 * Q_BLOCK)[None, :]
    u = qi + Q_BLOCK - ki
    tables, valids = [], []
    for window, dilation in DIL_PATTERNS:
        n = np.maximum(u * dilation, 0)
        nf = np.maximum(n, 1).astype(np.float32)
        large = REL_MAX_EXACT + (np.log(nf / np.float32(REL_MAX_EXACT))
                                 / np.float32(math.log(REL_MAX_DISTANCE / REL_MAX_EXACT))
                                 * np.float32(REL_BUCKETS - REL_MAX_EXACT)).astype(np.int32)
        large = np.minimum(large, REL_BUCKETS - 1)
        tables.append(np.where(n < REL_MAX_EXACT, n, large).astype(np.int32))
        valids.append(((u >= 0) & (u <= window // dilation)).astype(np.int32))
    return np.stack(tables), np.stack(valids)


def _bias_kernel(tab_ref, idx_ref, valid_ref, o_ref):
    g = pl.program_id(0)
    idx = idx_ref[...]
    valid = valid_ref[...] > 0
    for h in range(DIL_HEADS):
        acc = jnp.zeros(idx.shape, F32)
        for b in range(REL_BUCKETS):
            acc = jnp.where(idx == b, tab_ref[b, g * DIL_HEADS + h], acc)
        o_ref[h] = jnp.where(valid, acc, NEG_INF)


def _bias_expand(rel_bias):
    idx, valid = _bucket_tables()
    blk = pl.BlockSpec((None, Q_BLOCK, 2 * Q_BLOCK), lambda g: (g, 0, 0))
    return pl.pallas_call(
        _bias_kernel,
        grid=(DIL_GROUPS,),
        in_specs=[pl.BlockSpec(memory_space=pltpu.SMEM), blk, blk],
        out_specs=pl.BlockSpec((DIL_HEADS, Q_BLOCK, 2 * Q_BLOCK), lambda g: (g, 0, 0)),
        out_shape=jax.ShapeDtypeStruct((DIL_GROUPS * DIL_HEADS, Q_BLOCK, 2 * Q_BLOCK), F32),
        compiler_params=_params(("arbitrary",), 32),
        name="bias_expand",
    )(rel_bias, jnp.asarray(idx), jnp.asarray(valid))


def _dil_kernel(q_ref, kc_ref, kp_ref, vc_ref, vp_ref, b_ref, o_ref, *, LQ):
    is_first = pl.program_id(1) == 0
    for h in range(DIL_HEADS):
        hs = slice(h * HEAD_DIM, (h + 1) * HEAD_DIM)
        for sb in range(LQ // Q_BLOCK):
            rows = slice(sb * Q_BLOCK, (sb + 1) * Q_BLOCK)
            q = q_ref[rows, hs]
            if sb == 0:
                kk = jnp.concatenate([kp_ref[:, hs], kc_ref[0:Q_BLOCK, hs]], axis=0)
                vv = jnp.concatenate([vp_ref[:, hs], vc_ref[0:Q_BLOCK, hs]], axis=0)
            else:
                kk = kc_ref[(sb - 1) * Q_BLOCK:(sb + 1) * Q_BLOCK, hs]
                vv = vc_ref[(sb - 1) * Q_BLOCK:(sb + 1) * Q_BLOCK, hs]
            s = lax.dot_general(q, kk, (((1,), (1,)), ((), ())), preferred_element_type=F32) + b_ref[h]
            if sb == 0:
                col = lax.broadcasted_iota(jnp.int32, s.shape, 1)
                s = jnp.where(jnp.logical_and(is_first, col < Q_BLOCK), NEG_INF, s)
            m = jnp.max(s, axis=-1, keepdims=True)
            p = jnp.exp(s - m)
            den = jnp.sum(p, axis=-1, keepdims=True)
            o = jnp.dot(p.astype(BF16), vv, preferred_element_type=F32) / den
            lse = m + jnp.log(den)
            o_ref[rows, hs] = o
            o_ref[rows, DIL_OUT_WIDTH + h * HEAD_DIM:DIL_OUT_WIDTH + (h + 1) * HEAD_DIM] = (
                jnp.broadcast_to(lse, (Q_BLOCK, HEAD_DIM)))


def _dil_attn(q, k, v, bias, g, name, LQ=512):
    d, L, _ = q.shape
    assert d == DIL_PATTERNS[g][1]
    ow = 2 * DIL_OUT_WIDTH
    ratio = LQ // Q_BLOCK
    kern = functools.partial(_dil_kernel, LQ=LQ)
    cur = pl.BlockSpec((None, LQ, HEAD_BLOCK), lambda r, n: (r, n, 0))
    prev = pl.BlockSpec((None, Q_BLOCK, HEAD_BLOCK), lambda r, n: (r, jnp.maximum(n * ratio - 1, 0), 0))
    return pl.pallas_call(
        kern,
        grid=(d, L // LQ),
        in_specs=[cur, cur, prev, cur, prev,
                  pl.BlockSpec((DIL_HEADS, Q_BLOCK, 2 * Q_BLOCK), lambda r, n: (g, 0, 0))],
        out_specs=pl.BlockSpec((None, LQ, ow), lambda r, n: (r, n, 0)),
        out_shape=jax.ShapeDtypeStruct((d, L, ow), F32),
        compiler_params=_params(("arbitrary", "arbitrary"), 32),
        name=name,
    )(q, k, k, v, v, bias)


def _out_b_kernel(x_ref, o0_ref, o1_ref, o2_ref, m_ref, w_ref, g_ref, o_ref, hn_ref, wb_ref, s1_ref, s2_ref,
                  *, tm):
    _cast_weight_once(w_ref, wb_ref)
    nslab = 2 * DIL_HEADS
    for og_ref, s_ref in ((o1_ref, s1_ref), (o2_ref, s2_ref)):
        d = og_ref.shape[0]
        for r in range(d):
            for c in range(nslab):
                s_ref[c, pl.ds(r, tm // d, stride=d), :] = og_ref[r, :, c * LANES:(c + 1) * LANES]
    parts = []
    for h in range(DIL_HEADS):
        hs = slice(h * HEAD_DIM, (h + 1) * HEAD_DIM)
        ls = slice(DIL_OUT_WIDTH + h * HEAD_DIM, DIL_OUT_WIDTH + (h + 1) * HEAD_DIM)
        lses = [o0_ref[0, :, ls], s1_ref[DIL_HEADS + h], s2_ref[DIL_HEADS + h]]
        outs = [o0_ref[0, :, hs], s1_ref[h], s2_ref[h]]
        mx = jnp.maximum(jnp.maximum(lses[0], lses[1]), lses[2])
        es = [jnp.exp(l - mx) for l in lses]
        den = es[0] + es[1] + es[2]
        num = es[0] * outs[0] + es[1] * outs[1] + es[2] * outs[2]
        parts.append((num / den).astype(BF16))
    dil = jnp.concatenate(parts, axis=-1)
    acc = jnp.dot(dil, wb_ref[0:DIL_OUT_WIDTH, :], preferred_element_type=F32)
    acc += jnp.dot(m_ref[...], wb_ref[DIL_OUT_WIDTH:DIL_OUT_WIDTH + MEM_WIDTH, :], preferred_element_type=F32)
    _residual_and_norm(x_ref, acc, g_ref, o_ref, hn_ref)


def _out_b(x, ogs, mem_out, w, widx, g_mlp, name, tm=512):
    S, D = x.shape
    ow = 2 * DIL_OUT_WIDTH
    K = DIL_OUT_WIDTH + MEM_WIDTH
    og_specs = [pl.BlockSpec((og.shape[0], tm // og.shape[0], ow), lambda i: (0, i, 0)) for og in ogs]
    row = pl.BlockSpec((tm, D), lambda i: (i, 0))
    return pl.pallas_call(
        functools.partial(_out_b_kernel, tm=tm),
        grid=(S // tm,),
        in_specs=[
            row,
            *og_specs,
            pl.BlockSpec((tm, MEM_WIDTH), lambda i: (i, 0)),
            pl.BlockSpec((None, K, D), lambda i: (widx, 0, 0), pipeline_mode=pl.Buffered(1)),
            pl.BlockSpec((1, D), lambda i: (0, 0)),
        ],
        out_specs=[row, row],
        out_shape=[jax.ShapeDtypeStruct((S, D), F32), jax.ShapeDtypeStruct((S, D), BF16)],
        scratch_shapes=[pltpu.VMEM((K, D), BF16), pltpu.VMEM((ow // LANES, tm, LANES), F32),
                        pltpu.VMEM((ow // LANES, tm, LANES), F32)],
        compiler_params=_params(("arbitrary",), 56),
        name=name,
    )(x, ogs[0], ogs[1], ogs[2], mem_out, w, g_mlp.reshape(1, D))


def kernel(x, mem, norm_mix_g, norm_mlp_g, mlp_w1, mlp_w2, mem_norm_g, mem_w_kv, mem_q_norm_g, mem_k_norm_g, a_w_in, a_conv_w, a_conv_b, a_gate_r_w, a_gate_r_b, a_gate_i_w, a_gate_i_b, a_lambda, a_w_out, kv_norm_g, kv_w, k_norm_g, rel_bias, b_w_q, b_q_norm_g, b_w_out):
    B, S, D = x.shape
    assert (B, S, D) == (1, SEQ, D_MODEL) and mem.shape == (1, MEM_TOKENS, D_MODEL)
    xs = x[0]
    ones = jnp.ones((1, HEAD_DIM), F32)

    mem_k, mem_v = _mem_kv(mem[0], mem_norm_g, mem_w_kv, mem_k_norm_g)
    bias = _bias_expand(rel_bias)
    kvb = None
    (hn,) = _rms(xs, norm_mix_g[0:1], "rms_first")
    w1b, w2b = mlp_w1[0].astype(BF16), mlp_w2[0].astype(BF16)

    for layer in range(DEPTH):
        mq_gain = mem_q_norm_g[layer][None, :]
        if layer < N_A_LAYERS:
            i = layer
            nblk = A_IN_WIDTH // HEAD_BLOCK
            gains = jnp.concatenate([jnp.tile(ones, (nblk - 1, 1)), mq_gain], axis=0)
            proj = _proj(hn, a_w_in, i, gains, nblk - 1, nblk, ATTN_SCALE, F32, f"a{i}_in_proj")
            mem_out = _mem_attn(proj, nblk - 1, mem_k, mem_v, layer, f"a{i}_mem_attn")
            wg = jnp.concatenate([a_gate_r_w[i], a_gate_i_w[i]], axis=-1).astype(BF16)
            bg = jnp.concatenate([a_gate_r_b[i], a_gate_i_b[i]], axis=-1)[:, None, :]
            y = _rglru(proj, a_conv_w[i], a_conv_b[i], wg, bg, a_lambda[i], f"a{i}_rglru")
            xs, hn = _out_a(xs, y, mem_out, a_w_out, i, norm_mlp_g[layer], f"a{i}_out_proj")
        else:
            j = layer - N_A_LAYERS
            dils = tuple(d for _, d in DIL_PATTERNS)
            if j == 0:
                (hn_kv,) = _rms(xs, kv_norm_g[None, :], "rms_kv")
                kv_gains = jnp.concatenate([k_norm_g, jnp.tile(ones, (DIL_GROUPS, 1))], axis=0)
                kvs = _proj_split(hn_kv, kv_w.astype(BF16), kv_gains, dils + dils,
                                  (True,) * DIL_GROUPS + (False,) * DIL_GROUPS, 1.0, "shared_kv")
            q_gains = jnp.concatenate([b_q_norm_g[j], mq_gain], axis=0)
            qs = _proj_split(hn, b_w_q[j].astype(BF16), q_gains, dils + (1,), (True,) * (DIL_GROUPS + 1),
                             ATTN_SCALE, f"b{j}_q_proj")
            mem_out = _mem_attn(qs[DIL_GROUPS].reshape(S, MEM_WIDTH), 0, mem_k, mem_v, layer, f"b{j}_mem_attn")
            ogs = [_dil_attn(qs[g], kvs[g], kvs[DIL_GROUPS + g], bias, g, f"b{j}_dil{g}")
                   for g in range(DIL_GROUPS)]
            xs, hn = _out_b(xs, ogs, mem_out, b_w_out, j, norm_mlp_g[layer], f"b{j}_out_proj")
        nxt = (norm_mix_g[layer + 1], mlp_w1, mlp_w2, layer + 1) if layer + 1 < DEPTH else None
        xs, hn, w1b, w2b = _mlp(hn, xs, w1b, w2b, nxt, f"mlp{layer}")
    return xs[None]
```

```python
import functools
import math

import numpy as np
import jax
import jax.numpy as jnp
from jax import lax
from jax.experimental import pallas as pl
from jax.experimental.pallas import tpu as pltpu

F32 = jnp.float32
BF16 = jnp.bfloat16

D_MODEL = 2048
SEQ = 8192
DEPTH = 4
N_A_LAYERS = DEPTH // 2
HEAD_DIM = 128
D_FF = 4 * D_MODEL
LRU_WIDTH = 3 * D_MODEL // 4
LRU_BLOCK_WIDTH = 128
LRU_BLOCKS = LRU_WIDTH // LRU_BLOCK_WIDTH
CONV_WIDTH = 4
RG_C = 8.0
MEM_TOKENS = 256
MEM_HEADS = 4
MEM_WIDTH = MEM_HEADS * HEAD_DIM
DIL_PATTERNS = ((128, 1), (512, 4), (2048, 16))
DIL_GROUPS = len(DIL_PATTERNS)
DIL_HEADS = 4
DIL_WIDTH = DIL_GROUPS * DIL_HEADS * HEAD_DIM
DIL_OUT_WIDTH = DIL_HEADS * HEAD_DIM
Q_BLOCK = 128
REL_BUCKETS = 32
REL_MAX_EXACT = REL_BUCKETS // 2
REL_MAX_DISTANCE = 2048
A_IN_WIDTH = 2 * LRU_WIDTH + MEM_WIDTH
NORM_EPS = 1e-6
NEG_INF = -1e30
ATTN_SCALE = HEAD_DIM ** -0.5

V7X_VMEM_BYTES = 64 * 1024 * 1024
LANES = 128
SUBLANES = 8
HEAD_BLOCK = DIL_HEADS * HEAD_DIM


def _params(semantics, vmem_mib):
    assert vmem_mib * 1024 * 1024 < V7X_VMEM_BYTES
    return pltpu.CompilerParams(dimension_semantics=semantics,
                                vmem_limit_bytes=vmem_mib * 1024 * 1024)


def _rms_rows(x, g):
    ms = jnp.mean(x * x, axis=-1, keepdims=True)
    return x * lax.rsqrt(ms + NORM_EPS) * g


def _head_norm_cols(a, g, scale):
    parts = []
    for h in range(a.shape[1] // HEAD_DIM):
        ah = a[:, h * HEAD_DIM:(h + 1) * HEAD_DIM]
        y = ah * lax.rsqrt(jnp.mean(ah * ah, axis=-1, keepdims=True) + NORM_EPS) * g
        parts.append(y * scale if scale != 1.0 else y)
    return jnp.concatenate(parts, axis=-1)


def _proj_kernel(a_ref, *rest, groups, normed, scale, tm, has_norm, has_scr):
    rest = list(rest)
    gn_ref = rest.pop(0) if has_norm else None
    w_ref, hg_ref = rest.pop(0), rest.pop(0)
    o_refs = [rest.pop(0) for _ in groups]
    hn_s = rest.pop(0) if has_norm else None
    scr = rest.pop(0) if has_scr else None
    if has_norm:
        hn_s[...] = _rms_rows(a_ref[...], gn_ref[...]).astype(BF16)
        lhs_ref = hn_s
    else:
        lhs_ref = a_ref
    jb = 0
    for (nb, d, _), o_ref in zip(groups, o_refs):
        for k in range(nb):
            acc = jnp.dot(lhs_ref[...], w_ref[:, jb * HEAD_BLOCK:(jb + 1) * HEAD_BLOCK],
                          preferred_element_type=F32)
            if normed[jb]:
                acc = _head_norm_cols(acc, hg_ref[jb:jb + 1, :], scale)
            if d == 1:
                o_ref[0, :, k * HEAD_BLOCK:(k + 1) * HEAD_BLOCK] = acc.astype(o_ref.dtype)
            else:
                for h in range(HEAD_BLOCK // LANES):
                    scr[h] = acc[:, h * LANES:(h + 1) * LANES]
                for r in range(d):
                    for h in range(HEAD_BLOCK // LANES):
                        o_ref[r, :, h * LANES:(h + 1) * LANES] = (
                            scr[h, pl.ds(r, tm // d, stride=d), :].astype(o_ref.dtype))
            jb += 1


def _proj(a, norm_gain, w, head_gains, groups, normed, scale, name, tm, vmem_mib):
    S, D = a.shape
    N = w.shape[1]
    nblk = N // HEAD_BLOCK
    assert sum(nb for nb, _, _ in groups) == len(normed) == nblk
    assert all(nb == 1 for nb, d, _ in groups if d > 1)
    has_norm = norm_gain is not None
    has_scr = any(d > 1 for _, d, _ in groups)
    kern = functools.partial(_proj_kernel, groups=groups, normed=normed, scale=scale, tm=tm,
                             has_norm=has_norm, has_scr=has_scr)
    in_specs = [pl.BlockSpec((tm, D), lambda i: (i, 0))]
    args = [a]
    if has_norm:
        in_specs.append(pl.BlockSpec((1, D), lambda i: (0, 0)))
        args.append(norm_gain.reshape(1, D))
    in_specs += [pl.BlockSpec((D, N), lambda i: (0, 0), pipeline_mode=pl.Buffered(1)),
                 pl.BlockSpec((nblk, HEAD_DIM), lambda i: (0, 0))]
    args += [w, head_gains]
    scratch = []
    if has_norm:
        scratch.append(pltpu.VMEM((tm, D), BF16))
    if has_scr:
        scratch.append(pltpu.VMEM((HEAD_BLOCK // LANES, tm, LANES), F32))
    return pl.pallas_call(
        kern,
        grid=(S // tm,),
        in_specs=in_specs,
        out_specs=[pl.BlockSpec((d, tm // d, nb * HEAD_BLOCK), lambda i: (0, i, 0)) for nb, d, _ in groups],
        out_shape=[jax.ShapeDtypeStruct((d, S // d, nb * HEAD_BLOCK), dt) for nb, d, dt in groups],
        scratch_shapes=scratch,
        compiler_params=_params(("parallel",), vmem_mib),
        name=name,
    )(*args)


def _mlp_kernel(hn_ref, x_ref, w1_ref, w2_ref, *rest, n_steps, has_next):
    if has_next:
        gn_ref, w1n_ref, w2n_ref, o_ref, hno_ref, w1o_ref, w2o_ref = rest
    else:
        (o_ref,) = rest
    j = pl.program_id(1)

    @pl.when(j == 0)
    def _():
        o_ref[...] = x_ref[...]

    h = jnp.dot(hn_ref[...], w1_ref[...], preferred_element_type=F32)
    h = jnp.maximum(h, 0.0)
    h = (h * h).astype(BF16)
    o_ref[...] += jnp.dot(h, w2_ref[...], preferred_element_type=F32)

    if has_next:
        w1o_ref[...] = w1n_ref[...].astype(BF16)
        w2o_ref[...] = w2n_ref[...].astype(BF16)

        @pl.when(j == n_steps - 1)
        def _():
            hno_ref[...] = _rms_rows(o_ref[...], gn_ref[...]).astype(BF16)


def _mlp(hn, x, w1b, w2b, nxt, name, tm=1024, tf=512):
    S, D = x.shape
    F = w1b.shape[1]
    ni, nj = S // tm, F // tf
    has_next = nxt is not None
    row_spec = lambda **kw: pl.BlockSpec((tm, D), lambda i, j: (i, 0), **kw)
    in_specs = [row_spec(pipeline_mode=pl.Buffered(1)), row_spec(),
                pl.BlockSpec((D, tf), lambda i, j: (0, j)),
                pl.BlockSpec((tf, D), lambda i, j: (j, 0))]
    args = [hn, x, w1b, w2b]
    out_specs = [row_spec()]
    out_shape = [jax.ShapeDtypeStruct((S, D), F32)]
    if has_next:
        g_next, w1s, w2s, nl = nxt
        r1, r2 = D // (ni * nj), F // (ni * nj)
        in_specs += [pl.BlockSpec((1, D), lambda i, j: (0, 0)),
                     pl.BlockSpec((None, r1, F), lambda i, j: (nl, i * nj + j, 0)),
                     pl.BlockSpec((None, r2, D), lambda i, j: (nl, i * nj + j, 0))]
        args += [g_next.reshape(1, D), w1s, w2s]
        out_specs += [row_spec(pipeline_mode=pl.Buffered(1)),
                      pl.BlockSpec((r1, F), lambda i, j: (i * nj + j, 0)),
                      pl.BlockSpec((r2, D), lambda i, j: (i * nj + j, 0))]
        out_shape += [jax.ShapeDtypeStruct((S, D), BF16),
                      jax.ShapeDtypeStruct((D, F), BF16),
                      jax.ShapeDtypeStruct((F, D), BF16)]
    kern = functools.partial(_mlp_kernel, n_steps=nj, has_next=has_next)
    outs = pl.pallas_call(
        kern,
        grid=(ni, nj),
        in_specs=in_specs,
        out_specs=out_specs,
        out_shape=out_shape,
        compiler_params=_params(("parallel", "arbitrary"), 60),
        name=name,
    )(*args)
    return tuple(outs) if has_next else (outs[0], None, None, None)


def _mem_kv_kernel(mem_ref, g_ref, w_ref, kg_ref, k_ref, v_ref):
    hn = _rms_rows(mem_ref[...], g_ref[...]).astype(BF16)
    kv = jnp.dot(hn, w_ref[...].astype(BF16), preferred_element_type=F32)
    k_ref[...] = _head_norm_cols(kv[:, :MEM_WIDTH], kg_ref[...], 1.0).astype(BF16)
    v_ref[...] = kv[:, MEM_WIDTH:].astype(BF16)


def _mem_kv(mem, mem_norm_g, mem_w_kv, mem_k_norm_g):
    M, D = mem.shape
    out = jax.ShapeDtypeStruct((DEPTH, M, MEM_WIDTH), BF16)
    return pl.pallas_call(
        _mem_kv_kernel,
        grid=(DEPTH,),
        in_specs=[
            pl.BlockSpec((M, D), lambda l: (0, 0)),
            pl.BlockSpec((None, 1, D), lambda l: (l, 0, 0)),
            pl.BlockSpec((None, D, 2 * MEM_WIDTH), lambda l: (l, 0, 0)),
            pl.BlockSpec((None, 1, HEAD_DIM), lambda l: (l, 0, 0)),
        ],
        out_specs=[pl.BlockSpec((None, M, MEM_WIDTH), lambda l: (l, 0, 0)),
                   pl.BlockSpec((None, M, MEM_WIDTH), lambda l: (l, 0, 0))],
        out_shape=[out, out],
        compiler_params=_params(("arbitrary",), 40),
        name="mem_kv",
    )(mem, mem_norm_g.reshape(DEPTH, 1, D), mem_w_kv, mem_k_norm_g.reshape(DEPTH, 1, HEAD_DIM))


def _mem_attn_kernel(q_ref, k_ref, v_ref, o_ref):
    for h in range(MEM_HEADS):
        hs = slice(h * HEAD_DIM, (h + 1) * HEAD_DIM)
        q = q_ref[:, hs].astype(BF16)
        s = lax.dot_general(q, k_ref[:, hs], (((1,), (1,)), ((), ())), preferred_element_type=F32)
        m = jnp.max(s, axis=-1, keepdims=True)
        p = jnp.exp(s - m)
        den = jnp.sum(p, axis=-1, keepdims=True)
        o = jnp.dot(p.astype(BF16), v_ref[:, hs], preferred_element_type=F32) / den
        o_ref[:, hs] = o.astype(o_ref.dtype)


def _mem_attn(proj, q_col_block, mem_k, mem_v, layer, name, tm=512):
    S = proj.shape[0]
    return pl.pallas_call(
        _mem_attn_kernel,
        grid=(S // tm,),
        in_specs=[
            pl.BlockSpec((tm, MEM_WIDTH), lambda i: (i, q_col_block)),
            pl.BlockSpec((None, MEM_TOKENS, MEM_WIDTH), lambda i: (layer, 0, 0)),
            pl.BlockSpec((None, MEM_TOKENS, MEM_WIDTH), lambda i: (layer, 0, 0)),
        ],
        out_specs=pl.BlockSpec((tm, MEM_WIDTH), lambda i: (i, 0)),
        out_shape=jax.ShapeDtypeStruct((S, MEM_WIDTH), BF16),
        compiler_params=_params(("parallel",), 32),
        name=name,
    )(proj, mem_k, mem_v)


def _rglru_kernel(u_ref, gate_ref, cw_ref, cb_ref, wg_ref, bg_ref, lam_ref, y_ref, ubuf, hcar, *, T, LB):
    i = pl.program_id(1)

    @pl.when(i == 0)
    def _():
        ubuf[0:SUBLANES, :] = jnp.zeros((SUBLANES, LB), F32)
        hcar[...] = jnp.zeros_like(hcar)

    u = u_ref[...]
    ubuf[SUBLANES:T + SUBLANES, :] = u
    cw = cw_ref[...]
    xc = cb_ref[...] + (ubuf[SUBLANES - 3:T + SUBLANES - 3, :] * cw[0:1, :]
                        + ubuf[SUBLANES - 2:T + SUBLANES - 2, :] * cw[1:2, :]
                        + ubuf[SUBLANES - 1:T + SUBLANES - 1, :] * cw[2:3, :]
                        + u * cw[3:4, :])
    ubuf[0:SUBLANES, :] = u[T - SUBLANES:T, :]

    lam = lam_ref[...]
    sp = jnp.maximum(-lam, 0.0) + jnp.log1p(jnp.exp(-jnp.abs(lam)))
    row = lax.broadcasted_iota(jnp.int32, (T // SUBLANES, SUBLANES, LANES), 1)

    for c in range(LB // LANES):
        cs = slice(c * LANES, (c + 1) * LANES)
        xcc = xc[:, cs]
        z = jnp.dot(xcc.astype(BF16), wg_ref[c], preferred_element_type=F32) + bg_ref[c]
        r = jax.nn.sigmoid(z[:, :LANES])
        ig = jax.nn.sigmoid(z[:, LANES:])
        log_a = (-RG_C * r) * sp[:, cs]
        a = jnp.exp(log_a)
        om = 1.0 - a * a
        root = jnp.where(om > 0.0, om * lax.rsqrt(om), 0.0)
        b = root * (ig * xcc)

        a3 = a.reshape(T // SUBLANES, SUBLANES, LANES)
        b3 = b.reshape(T // SUBLANES, SUBLANES, LANES)
        for s in (1, 2, 4):
            a_sh = pltpu.roll(a3, s, axis=1)
            b_sh = pltpu.roll(b3, s, axis=1)
            keep = row >= s
            b3 = jnp.where(keep, a3 * b_sh + b3, b3)
            a3 = jnp.where(keep, a3 * a_sh, a3)

        hprev = hcar[:, cs]
        hs = []
        for grp in range(T // SUBLANES):
            hg = a3[grp] * hprev + b3[grp]
            hprev = hg[SUBLANES - 1:SUBLANES, :]
            hs.append(hg)
        hcar[:, cs] = hprev
        h = jnp.concatenate(hs, axis=0)

        gt = gate_ref[:, cs]
        gelu = 0.5 * gt * (1.0 + jnp.tanh(math.sqrt(2.0 / math.pi) * (gt + 0.044715 * (gt * gt * gt))))
        y_ref[:, cs] = (h * gelu).astype(y_ref.dtype)


def _rglru(u, gate, conv_w, conv_b, wg, bg, lam, name, T=512, LB=512):
    S = u.shape[0]
    W = LRU_WIDTH
    ncb = W // LB
    kern = functools.partial(_rglru_kernel, T=T, LB=LB)
    return pl.pallas_call(
        kern,
        grid=(ncb, S // T),
        in_specs=[
            pl.BlockSpec((T, LB), lambda n, i: (i, n)),
            pl.BlockSpec((T, LB), lambda n, i: (i, n)),
            pl.BlockSpec((CONV_WIDTH, LB), lambda n, i: (0, n)),
            pl.BlockSpec((1, LB), lambda n, i: (0, n)),
            pl.BlockSpec((LB // LANES, LRU_BLOCK_WIDTH, 2 * LRU_BLOCK_WIDTH), lambda n, i: (n, 0, 0)),
            pl.BlockSpec((LB // LANES, 1, 2 * LRU_BLOCK_WIDTH), lambda n, i: (n, 0, 0)),
            pl.BlockSpec((1, LB), lambda n, i: (0, n)),
        ],
        out_specs=pl.BlockSpec((T, LB), lambda n, i: (i, n)),
        out_shape=jax.ShapeDtypeStruct((S, W), BF16),
        scratch_shapes=[pltpu.VMEM((T + SUBLANES, LB), F32), pltpu.VMEM((1, LB), F32)],
        compiler_params=_params(("arbitrary", "arbitrary"), 32),
        name=name,
    )(u, gate, conv_w, conv_b.reshape(1, W), wg, bg, lam.reshape(1, W))


def _cast_weight_once(w_ref, wb_ref):
    @pl.when(pl.program_id(0) == 0)
    def _():
        wb_ref[...] = w_ref[...].astype(BF16)


def _residual_and_norm(x_ref, acc, g_ref, o_ref, hn_ref):
    xn = x_ref[...] + acc
    o_ref[...] = xn
    hn_ref[...] = _rms_rows(xn, g_ref[...]).astype(BF16)


def _out_a_kernel(x_ref, y_ref, m_ref, w_ref, g_ref, o_ref, hn_ref, wb_ref):
    _cast_weight_once(w_ref, wb_ref)
    acc = jnp.dot(y_ref[...], wb_ref[0:LRU_WIDTH, :], preferred_element_type=F32)
    acc += jnp.dot(m_ref[...], wb_ref[LRU_WIDTH:LRU_WIDTH + MEM_WIDTH, :], preferred_element_type=F32)
    _residual_and_norm(x_ref, acc, g_ref, o_ref, hn_ref)


def _out_a(x, y, mem_out, w, widx, g_mlp, name, tm=512):
    S, D = x.shape
    K = LRU_WIDTH + MEM_WIDTH
    row = pl.BlockSpec((tm, D), lambda i: (i, 0))
    return pl.pallas_call(
        _out_a_kernel,
        grid=(S // tm,),
        in_specs=[
            row,
            pl.BlockSpec((tm, LRU_WIDTH), lambda i: (i, 0)),
            pl.BlockSpec((tm, MEM_WIDTH), lambda i: (i, 0)),
            pl.BlockSpec((None, K, D), lambda i: (widx, 0, 0), pipeline_mode=pl.Buffered(1)),
            pl.BlockSpec((1, D), lambda i: (0, 0)),
        ],
        out_specs=[row, row],
        out_shape=[jax.ShapeDtypeStruct((S, D), F32), jax.ShapeDtypeStruct((S, D), BF16)],
        scratch_shapes=[pltpu.VMEM((K, D), BF16)],
        compiler_params=_params(("arbitrary",), 56),
        name=name,
    )(x, y, mem_out, w, g_mlp.reshape(1, D))


def _bucket_tables():
    qi = np.arange(Q_BLOCK)[:, None]
    ki = np.arange(2 * Q_BLOCK)[None, :]
    u = qi + Q_BLOCK - ki
    tables, valids = [], []
    for window, dilation in DIL_PATTERNS:
        n = np.maximum(u * dilation, 0)
        nf = np.maximum(n, 1).astype(np.float32)
        large = REL_MAX_EXACT + (np.log(nf / np.float32(REL_MAX_EXACT))
                                 / np.float32(math.log(REL_MAX_DISTANCE / REL_MAX_EXACT))
                                 * np.float32(REL_BUCKETS - REL_MAX_EXACT)).astype(np.int32)
        large = np.minimum(large, REL_BUCKETS - 1)
        tables.append(np.where(n < REL_MAX_EXACT, n, large).astype(np.int32))
        valids.append(((u >= 0) & (u <= window // dilation)).astype(np.int32))
    return np.stack(tables), np.stack(valids)


def _bias_kernel(tab_ref, idx_ref, valid_ref, o_ref):
    g = pl.program_id(0)
    idx = idx_ref[...]
    valid = valid_ref[...] > 0
    for h in range(DIL_HEADS):
        acc = jnp.zeros(idx.shape, F32)
        for b in range(REL_BUCKETS):
            acc = jnp.where(idx == b, tab_ref[b, g * DIL_HEADS + h], acc)
        o_ref[h] = jnp.where(valid, acc, NEG_INF)


def _bias_expand(rel_bias):
    idx, valid = _bucket_tables()
    blk = pl.BlockSpec((None, Q_BLOCK, 2 * Q_BLOCK), lambda g: (g, 0, 0))
    return pl.pallas_call(
        _bias_kernel,
        grid=(DIL_GROUPS,),
        in_specs=[pl.BlockSpec(memory_space=pltpu.SMEM), blk, blk],
        out_specs=pl.BlockSpec((DIL_HEADS, Q_BLOCK, 2 * Q_BLOCK), lambda g: (g, 0, 0)),
        out_shape=jax.ShapeDtypeStruct((DIL_GROUPS * DIL_HEADS, Q_BLOCK, 2 * Q_BLOCK), F32),
        compiler_params=_params(("arbitrary",), 32),
        name="bias_expand",
    )(rel_bias, jnp.asarray(idx), jnp.asarray(valid))


def _dil_kernel(q_ref, kc_ref, kp_ref, vc_ref, vp_ref, b_ref, o_ref, *, LQ):
    is_first = pl.program_id(1) == 0
    for h in range(DIL_HEADS):
        hs = slice(h * HEAD_DIM, (h + 1) * HEAD_DIM)
        for sb in range(LQ // Q_BLOCK):
            rows = slice(sb * Q_BLOCK, (sb + 1) * Q_BLOCK)
            q = q_ref[rows, hs]
            if sb == 0:
                kk = jnp.concatenate([kp_ref[:, hs], kc_ref[0:Q_BLOCK, hs]], axis=0)
                vv = jnp.concatenate([vp_ref[:, hs], vc_ref[0:Q_BLOCK, hs]], axis=0)
            else:
                kk = kc_ref[(sb - 1) * Q_BLOCK:(sb + 1) * Q_BLOCK, hs]
                vv = vc_ref[(sb - 1) * Q_BLOCK:(sb + 1) * Q_BLOCK, hs]
            s = lax.dot_general(q, kk, (((1,), (1,)), ((), ())), preferred_element_type=F32) + b_ref[h]
            if sb == 0:
                col = lax.broadcasted_iota(jnp.int32, s.shape, 1)
                s = jnp.where(jnp.logical_and(is_first, col < Q_BLOCK), NEG_INF, s)
            m = jnp.max(s, axis=-1, keepdims=True)
            p = jnp.exp(s - m)
            den = jnp.sum(p, axis=-1, keepdims=True)
            o = jnp.dot(p.astype(BF16), vv, preferred_element_type=F32) / den
            lse = m + jnp.log(den)
            o_ref[rows, hs] = o
            o_ref[rows, DIL_OUT_WIDTH + h * HEAD_DIM:DIL_OUT_WIDTH + (h + 1) * HEAD_DIM] = (
                jnp.broadcast_to(lse, (Q_BLOCK, HEAD_DIM)))


def _dil_attn(q, k, v, bias, g, name, LQ=512):
    d, L, _ = q.shape
    assert d == DIL_PATTERNS[g][1]
    ow = 2 * DIL_OUT_WIDTH
    ratio = LQ // Q_BLOCK
    kern = functools.partial(_dil_kernel, LQ=LQ)
    cur = pl.BlockSpec((None, LQ, HEAD_BLOCK), lambda r, n: (r, n, 0))
    prev = pl.BlockSpec((None, Q_BLOCK, HEAD_BLOCK), lambda r, n: (r, jnp.maximum(n * ratio - 1, 0), 0))
    return pl.pallas_call(
        kern,
        grid=(d, L // LQ),
        in_specs=[cur, cur, prev, cur, prev,
                  pl.BlockSpec((DIL_HEADS, Q_BLOCK, 2 * Q_BLOCK), lambda r, n: (g, 0, 0))],
        out_specs=pl.BlockSpec((None, LQ, ow), lambda r, n: (r, n, 0)),
        out_shape=jax.ShapeDtypeStruct((d, L, ow), F32),
        compiler_params=_params(("arbitrary", "arbitrary"), 32),
        name=name,
    )(q, k, k, v, v, bias)


def _out_b_kernel(x_ref, o0_ref, o1_ref, o2_ref, m_ref, w_ref, g_ref, o_ref, hn_ref, wb_ref, s1_ref, s2_ref,
                  *, tm):
    _cast_weight_once(w_ref, wb_ref)
    nslab = 2 * DIL_HEADS
    for og_ref, s_ref in ((o1_ref, s1_ref), (o2_ref, s2_ref)):
        d = og_ref.shape[0]
        for r in range(d):
            for c in range(nslab):
                s_ref[c, pl.ds(r, tm // d, stride=d), :] = og_ref[r, :, c * LANES:(c + 1) * LANES]
    parts = []
    for h in range(DIL_HEADS):
        hs = slice(h * HEAD_DIM, (h + 1) * HEAD_DIM)
        ls = slice(DIL_OUT_WIDTH + h * HEAD_DIM, DIL_OUT_WIDTH + (h + 1) * HEAD_DIM)
        lses = [o0_ref[0, :, ls], s1_ref[DIL_HEADS + h], s2_ref[DIL_HEADS + h]]
        outs = [o0_ref[0, :, hs], s1_ref[h], s2_ref[h]]
        mx = jnp.maximum(jnp.maximum(lses[0], lses[1]), lses[2])
        es = [jnp.exp(l - mx) for l in lses]
        den = es[0] + es[1] + es[2]
        num = es[0] * outs[0] + es[1] * outs[1] + es[2] * outs[2]
        parts.append((num / den).astype(BF16))
    dil = jnp.concatenate(parts, axis=-1)
    acc = jnp.dot(dil, wb_ref[0:DIL_OUT_WIDTH, :], preferred_element_type=F32)
    acc += jnp.dot(m_ref[...], wb_ref[DIL_OUT_WIDTH:DIL_OUT_WIDTH + MEM_WIDTH, :], preferred_element_type=F32)
    _residual_and_norm(x_ref, acc, g_ref, o_ref, hn_ref)


def _out_b(x, ogs, mem_out, w, widx, g_mlp, name, tm=512):
    S, D = x.shape
    ow = 2 * DIL_OUT_WIDTH
    K = DIL_OUT_WIDTH + MEM_WIDTH
    og_specs = [pl.BlockSpec((og.shape[0], tm // og.shape[0], ow), lambda i: (0, i, 0)) for og in ogs]
    row = pl.BlockSpec((tm, D), lambda i: (i, 0))
    return pl.pallas_call(
        functools.partial(_out_b_kernel, tm=tm),
        grid=(S // tm,),
        in_specs=[
            row,
            *og_specs,
            pl.BlockSpec((tm, MEM_WIDTH), lambda i: (i, 0)),
            pl.BlockSpec((None, K, D), lambda i: (widx, 0, 0), pipeline_mode=pl.Buffered(1)),
            pl.BlockSpec((1, D), lambda i: (0, 0)),
        ],
        out_specs=[row, row],
        out_shape=[jax.ShapeDtypeStruct((S, D), F32), jax.ShapeDtypeStruct((S, D), BF16)],
        scratch_shapes=[pltpu.VMEM((K, D), BF16), pltpu.VMEM((ow // LANES, tm, LANES), F32),
                        pltpu.VMEM((ow // LANES, tm, LANES), F32)],
        compiler_params=_params(("arbitrary",), 56),
        name=name,
    )(x, ogs[0], ogs[1], ogs[2], mem_out, w, g_mlp.reshape(1, D))


def kernel(x, mem, norm_mix_g, norm_mlp_g, mlp_w1, mlp_w2, mem_norm_g, mem_w_kv, mem_q_norm_g, mem_k_norm_g, a_w_in, a_conv_w, a_conv_b, a_gate_r_w, a_gate_r_b, a_gate_i_w, a_gate_i_b, a_lambda, a_w_out, kv_norm_g, kv_w, k_norm_g, rel_bias, b_w_q, b_q_norm_g, b_w_out):
    B, S, D = x.shape
    assert (B, S, D) == (1, SEQ, D_MODEL) and mem.shape == (1, MEM_TOKENS, D_MODEL)
    xs = x[0]
    ones = jnp.ones((1, HEAD_DIM), F32)

    mem_k, mem_v = _mem_kv(mem[0], mem_norm_g, mem_w_kv, mem_k_norm_g)
    bias = _bias_expand(rel_bias)
    w1b, w2b = mlp_w1[0].astype(BF16), mlp_w2[0].astype(BF16)
    hn = None

    for layer in range(DEPTH):
        mq_gain = mem_q_norm_g[layer][None, :]
        a_in = xs if hn is None else hn
        a_gain = norm_mix_g[layer] if hn is None else None
        if layer < N_A_LAYERS:
            i = layer
            nblk = A_IN_WIDTH // HEAD_BLOCK
            gains = jnp.concatenate([jnp.tile(ones, (nblk - 1, 1)), mq_gain], axis=0)
            u, gate, mq = _proj(a_in, a_gain, a_w_in[i].astype(BF16), gains,
                                ((3, 1, F32), (3, 1, F32), (1, 1, BF16)), (False,) * (nblk - 1) + (True,),
                                ATTN_SCALE, f"a{i}_in_proj", tm=512, vmem_mib=56)
            mem_out = _mem_attn(mq.reshape(S, MEM_WIDTH), 0, mem_k, mem_v, layer, f"a{i}_mem_attn")
            wg = jnp.concatenate([a_gate_r_w[i], a_gate_i_w[i]], axis=-1).astype(BF16)
            bg = jnp.concatenate([a_gate_r_b[i], a_gate_i_b[i]], axis=-1)[:, None, :]
            y = _rglru(u.reshape(S, LRU_WIDTH), gate.reshape(S, LRU_WIDTH), a_conv_w[i], a_conv_b[i], wg, bg,
                       a_lambda[i], f"a{i}_rglru")
            xs, hn = _out_a(xs, y, mem_out, a_w_out, i, norm_mlp_g[layer], f"a{i}_out_proj")
        else:
            j = layer - N_A_LAYERS
            dils = tuple(d for _, d in DIL_PATTERNS)
            dil_groups = tuple((1, d, BF16) for d in dils)
            if j == 0:
                kv_gains = jnp.concatenate([k_norm_g, jnp.tile(ones, (DIL_GROUPS, 1))], axis=0)
                kvs = _proj(xs, kv_norm_g, kv_w.astype(BF16), kv_gains, dil_groups + dil_groups,
                            (True,) * DIL_GROUPS + (False,) * DIL_GROUPS, 1.0, "shared_kv", tm=1024, vmem_mib=56)
            q_gains = jnp.concatenate([b_q_norm_g[j], mq_gain], axis=0)
            qs = _proj(a_in, a_gain, b_w_q[j].astype(BF16), q_gains, dil_groups + ((1, 1, BF16),),
                       (True,) * (DIL_GROUPS + 1), ATTN_SCALE, f"b{j}_q_proj", tm=1024, vmem_mib=48)
            mem_out = _mem_attn(qs[DIL_GROUPS].reshape(S, MEM_WIDTH), 0, mem_k, mem_v, layer, f"b{j}_mem_attn")
            ogs = [_dil_attn(qs[g], kvs[g], kvs[DIL_GROUPS + g], bias, g, f"b{j}_dil{g}")
                   for g in range(DIL_GROUPS)]
            xs, hn = _out_b(xs, ogs, mem_out, b_w_out, j, norm_mlp_g[layer], f"b{j}_out_proj")
        nxt = (norm_mix_g[layer + 1], mlp_w1, mlp_w2, layer + 1) if layer + 1 < DEPTH else None
        xs, hn, w1b, w2b = _mlp(hn, xs, w1b, w2b, nxt, f"mlp{layer}")
    return xs[None]
```

```python
import functools
import math

import numpy as np
import jax
import jax.numpy as jnp
from jax import lax
from jax.experimental import pallas as pl
from jax.experimental.pallas import tpu as pltpu

F32 = jnp.float32
BF16 = jnp.bfloat16

D_MODEL = 2048
SEQ = 8192
DEPTH = 4
N_A_LAYERS = DEPTH // 2
HEAD_DIM = 128
D_FF = 4 * D_MODEL
LRU_WIDTH = 3 * D_MODEL // 4
LRU_BLOCK_WIDTH = 128
LRU_BLOCKS = LRU_WIDTH // LRU_BLOCK_WIDTH
CONV_WIDTH = 4
RG_C = 8.0
MEM_TOKENS = 256
MEM_HEADS = 4
MEM_WIDTH = MEM_HEADS * HEAD_DIM
DIL_PATTERNS = ((128, 1), (512, 4), (2048, 16))
DIL_GROUPS = len(DIL_PATTERNS)
DIL_HEADS = 4
DIL_WIDTH = DIL_GROUPS * DIL_HEADS * HEAD_DIM
DIL_OUT_WIDTH = DIL_HEADS * HEAD_DIM
Q_BLOCK = 128
REL_BUCKETS = 32
REL_MAX_EXACT = REL_BUCKETS // 2
REL_MAX_DISTANCE = 2048
A_IN_WIDTH = 2 * LRU_WIDTH + MEM_WIDTH
NORM_EPS = 1e-6
NEG_INF = -1e30
ATTN_SCALE = HEAD_DIM ** -0.5

V7X_VMEM_BYTES = 64 * 1024 * 1024
LANES = 128
SUBLANES = 8
HEAD_BLOCK = DIL_HEADS * HEAD_DIM


def _params(semantics, vmem_mib):
    assert vmem_mib * 1024 * 1024 < V7X_VMEM_BYTES
    return pltpu.CompilerParams(dimension_semantics=semantics,
                                vmem_limit_bytes=vmem_mib * 1024 * 1024)


def _rms_rows(x, g):
    ms = jnp.mean(x * x, axis=-1, keepdims=True)
    return x * lax.rsqrt(ms + NORM_EPS) * g


def _head_norm_cols(a, g, scale):
    parts = []
    for h in range(a.shape[1] // HEAD_DIM):
        ah = a[:, h * HEAD_DIM:(h + 1) * HEAD_DIM]
        y = ah * lax.rsqrt(jnp.mean(ah * ah, axis=-1, keepdims=True) + NORM_EPS) * g
        parts.append(y * scale if scale != 1.0 else y)
    return jnp.concatenate(parts, axis=-1)


def _proj_kernel(a_ref, *rest, groups, normed, scale, tm, has_norm, has_scr):
    rest = list(rest)
    gn_ref = rest.pop(0) if has_norm else None
    w_ref, hg_ref = rest.pop(0), rest.pop(0)
    o_refs = [rest.pop(0) for _ in groups]
    hn_s = rest.pop(0) if has_norm else None
    scr = rest.pop(0) if has_scr else None
    if has_norm:
        hn_s[...] = _rms_rows(a_ref[...], gn_ref[...]).astype(BF16)
        lhs_ref = hn_s
    else:
        lhs_ref = a_ref
    jb = 0
    for (nb, d, _), o_ref in zip(groups, o_refs):
        for k in range(nb):
            acc = jnp.dot(lhs_ref[...], w_ref[:, jb * HEAD_BLOCK:(jb + 1) * HEAD_BLOCK],
                          preferred_element_type=F32)
            if normed[jb]:
                acc = _head_norm_cols(acc, hg_ref[jb:jb + 1, :], scale)
            if d == 1:
                o_ref[0, :, k * HEAD_BLOCK:(k + 1) * HEAD_BLOCK] = acc.astype(o_ref.dtype)
            else:
                for h in range(HEAD_BLOCK // LANES):
                    scr[h] = acc[:, h * LANES:(h + 1) * LANES]
                for r in range(d):
                    for h in range(HEAD_BLOCK // LANES):
                        o_ref[r, :, h * LANES:(h + 1) * LANES] = (
                            scr[h, pl.ds(r, tm // d, stride=d), :].astype(o_ref.dtype))
            jb += 1


def _proj(a, norm_gain, w, head_gains, groups, normed, scale, name, tm, vmem_mib):
    S, D = a.shape
    N = w.shape[1]
    nblk = N // HEAD_BLOCK
    assert sum(nb for nb, _, _ in groups) == len(normed) == nblk
    assert all(nb == 1 for nb, d, _ in groups if d > 1)
    has_norm = norm_gain is not None
    has_scr = any(d > 1 for _, d, _ in groups)
    kern = functools.partial(_proj_kernel, groups=groups, normed=normed, scale=scale, tm=tm,
                             has_norm=has_norm, has_scr=has_scr)
    in_specs = [pl.BlockSpec((tm, D), lambda i: (i, 0))]
    args = [a]
    if has_norm:
        in_specs.append(pl.BlockSpec((1, D), lambda i: (0, 0)))
        args.append(norm_gain.reshape(1, D))
    in_specs += [pl.BlockSpec((D, N), lambda i: (0, 0), pipeline_mode=pl.Buffered(1)),
                 pl.BlockSpec((nblk, HEAD_DIM), lambda i: (0, 0))]
    args += [w, head_gains]
    scratch = []
    if has_norm:
        scratch.append(pltpu.VMEM((tm, D), BF16))
    if has_scr:
        scratch.append(pltpu.VMEM((HEAD_BLOCK // LANES, tm, LANES), F32))
    return pl.pallas_call(
        kern,
        grid=(S // tm,),
        in_specs=in_specs,
        out_specs=[pl.BlockSpec((d, tm // d, nb * HEAD_BLOCK), lambda i: (0, i, 0)) for nb, d, _ in groups],
        out_shape=[jax.ShapeDtypeStruct((d, S // d, nb * HEAD_BLOCK), dt) for nb, d, dt in groups],
        scratch_shapes=scratch,
        compiler_params=_params(("parallel",), vmem_mib),
        name=name,
    )(*args)


def _mlp_kernel(x_ref, g_ref, w1_ref, w2_ref, *rest, has_next):
    if has_next:
        w1n_ref, w2n_ref, o_ref, w1o_ref, w2o_ref, hn_ref = rest
    else:
        o_ref, hn_ref = rest

    @pl.when(pl.program_id(1) == 0)
    def _():
        x = x_ref[...]
        hn_ref[...] = _rms_rows(x, g_ref[...]).astype(BF16)
        o_ref[...] = x

    h = jnp.dot(hn_ref[...], w1_ref[...], preferred_element_type=F32)
    h = jnp.maximum(h, 0.0)
    h = (h * h).astype(BF16)
    o_ref[...] += jnp.dot(h, w2_ref[...], preferred_element_type=F32)

    if has_next:
        w1o_ref[...] = w1n_ref[...].astype(BF16)
        w2o_ref[...] = w2n_ref[...].astype(BF16)


def _mlp(x, g, w1b, w2b, nxt, name, tm=1024, tf=512):
    S, D = x.shape
    F = w1b.shape[1]
    ni, nj = S // tm, F // tf
    has_next = nxt is not None
    row_spec = pl.BlockSpec((tm, D), lambda i, j: (i, 0))
    in_specs = [row_spec, pl.BlockSpec((1, D), lambda i, j: (0, 0)),
                pl.BlockSpec((D, tf), lambda i, j: (0, j)),
                pl.BlockSpec((tf, D), lambda i, j: (j, 0))]
    args = [x, g.reshape(1, D), w1b, w2b]
    out_specs = [row_spec]
    out_shape = [jax.ShapeDtypeStruct((S, D), F32)]
    if has_next:
        w1s, w2s, nl = nxt
        r1, r2 = D // (ni * nj), F // (ni * nj)
        in_specs += [pl.BlockSpec((None, r1, F), lambda i, j: (nl, i * nj + j, 0)),
                     pl.BlockSpec((None, r2, D), lambda i, j: (nl, i * nj + j, 0))]
        args += [w1s, w2s]
        out_specs += [pl.BlockSpec((r1, F), lambda i, j: (i * nj + j, 0)),
                      pl.BlockSpec((r2, D), lambda i, j: (i * nj + j, 0))]
        out_shape += [jax.ShapeDtypeStruct((D, F), BF16),
                      jax.ShapeDtypeStruct((F, D), BF16)]
    kern = functools.partial(_mlp_kernel, has_next=has_next)
    outs = pl.pallas_call(
        kern,
        grid=(ni, nj),
        in_specs=in_specs,
        out_specs=out_specs,
        out_shape=out_shape,
        scratch_shapes=[pltpu.VMEM((tm, D), BF16)],
        compiler_params=_params(("parallel", "arbitrary"), 60),
        name=name,
    )(*args)
    return tuple(outs) if has_next else (outs[0], None, None)


def _mem_kv_kernel(mem_ref, g_ref, w_ref, kg_ref, k_ref, v_ref):
    hn = _rms_rows(mem_ref[...], g_ref[...]).astype(BF16)
    kv = jnp.dot(hn, w_ref[...].astype(BF16), preferred_element_type=F32)
    k_ref[...] = _head_norm_cols(kv[:, :MEM_WIDTH], kg_ref[...], 1.0).astype(BF16)
    v_ref[...] = kv[:, MEM_WIDTH:].astype(BF16)


def _mem_kv(mem, mem_norm_g, mem_w_kv, mem_k_norm_g):
    M, D = mem.shape
    out = jax.ShapeDtypeStruct((DEPTH, M, MEM_WIDTH), BF16)
    return pl.pallas_call(
        _mem_kv_kernel,
        grid=(DEPTH,),
        in_specs=[
            pl.BlockSpec((M, D), lambda l: (0, 0)),
            pl.BlockSpec((None, 1, D), lambda l: (l, 0, 0)),
            pl.BlockSpec((None, D, 2 * MEM_WIDTH), lambda l: (l, 0, 0)),
            pl.BlockSpec((None, 1, HEAD_DIM), lambda l: (l, 0, 0)),
        ],
        out_specs=[pl.BlockSpec((None, M, MEM_WIDTH), lambda l: (l, 0, 0)),
                   pl.BlockSpec((None, M, MEM_WIDTH), lambda l: (l, 0, 0))],
        out_shape=[out, out],
        compiler_params=_params(("arbitrary",), 40),
        name="mem_kv",
    )(mem, mem_norm_g.reshape(DEPTH, 1, D), mem_w_kv, mem_k_norm_g.reshape(DEPTH, 1, HEAD_DIM))


def _mem_attn_kernel(q_ref, k_ref, v_ref, o_ref):
    for h in range(MEM_HEADS):
        hs = slice(h * HEAD_DIM, (h + 1) * HEAD_DIM)
        q = q_ref[:, hs].astype(BF16)
        s = lax.dot_general(q, k_ref[:, hs], (((1,), (1,)), ((), ())), preferred_element_type=F32)
        m = jnp.max(s, axis=-1, keepdims=True)
        p = jnp.exp(s - m)
        den = jnp.sum(p, axis=-1, keepdims=True)
        o = jnp.dot(p.astype(BF16), v_ref[:, hs], preferred_element_type=F32) / den
        o_ref[:, hs] = o.astype(o_ref.dtype)


def _mem_attn(proj, q_col_block, mem_k, mem_v, layer, name, tm=512):
    S = proj.shape[0]
    return pl.pallas_call(
        _mem_attn_kernel,
        grid=(S // tm,),
        in_specs=[
            pl.BlockSpec((tm, MEM_WIDTH), lambda i: (i, q_col_block)),
            pl.BlockSpec((None, MEM_TOKENS, MEM_WIDTH), lambda i: (layer, 0, 0)),
            pl.BlockSpec((None, MEM_TOKENS, MEM_WIDTH), lambda i: (layer, 0, 0)),
        ],
        out_specs=pl.BlockSpec((tm, MEM_WIDTH), lambda i: (i, 0)),
        out_shape=jax.ShapeDtypeStruct((S, MEM_WIDTH), BF16),
        compiler_params=_params(("parallel",), 32),
        name=name,
    )(proj, mem_k, mem_v)


def _rglru_kernel(u_ref, gate_ref, cw_ref, cb_ref, wg_ref, bg_ref, lam_ref, y_ref, ubuf, hcar, *, T, LB):
    i = pl.program_id(1)

    @pl.when(i == 0)
    def _():
        ubuf[0:SUBLANES, :] = jnp.zeros((SUBLANES, LB), F32)
        hcar[...] = jnp.zeros_like(hcar)

    u = u_ref[...]
    ubuf[SUBLANES:T + SUBLANES, :] = u
    cw = cw_ref[...]
    xc = cb_ref[...] + (ubuf[SUBLANES - 3:T + SUBLANES - 3, :] * cw[0:1, :]
                        + ubuf[SUBLANES - 2:T + SUBLANES - 2, :] * cw[1:2, :]
                        + ubuf[SUBLANES - 1:T + SUBLANES - 1, :] * cw[2:3, :]
                        + u * cw[3:4, :])
    ubuf[0:SUBLANES, :] = u[T - SUBLANES:T, :]

    lam = lam_ref[...]
    sp = jnp.maximum(-lam, 0.0) + jnp.log1p(jnp.exp(-jnp.abs(lam)))
    row = lax.broadcasted_iota(jnp.int32, (T // SUBLANES, SUBLANES, LANES), 1)

    for c in range(LB // LANES):
        cs = slice(c * LANES, (c + 1) * LANES)
        xcc = xc[:, cs]
        z = jnp.dot(xcc.astype(BF16), wg_ref[c], preferred_element_type=F32) + bg_ref[c]
        r = jax.nn.sigmoid(z[:, :LANES])
        ig = jax.nn.sigmoid(z[:, LANES:])
        log_a = (-RG_C * r) * sp[:, cs]
        a = jnp.exp(log_a)
        om = 1.0 - a * a
        root = jnp.where(om > 0.0, om * lax.rsqrt(om), 0.0)
        b = root * (ig * xcc)

        a3 = a.reshape(T // SUBLANES, SUBLANES, LANES)
        b3 = b.reshape(T // SUBLANES, SUBLANES, LANES)
        for s in (1, 2, 4):
            a_sh = pltpu.roll(a3, s, axis=1)
            b_sh = pltpu.roll(b3, s, axis=1)
            keep = row >= s
            b3 = jnp.where(keep, a3 * b_sh + b3, b3)
            a3 = jnp.where(keep, a3 * a_sh, a3)

        hprev = hcar[:, cs]
        hs = []
        for grp in range(T // SUBLANES):
            hg = a3[grp] * hprev + b3[grp]
            hprev = hg[SUBLANES - 1:SUBLANES, :]
            hs.append(hg)
        hcar[:, cs] = hprev
        h = jnp.concatenate(hs, axis=0)

        gt = gate_ref[:, cs]
        gelu = 0.5 * gt * (1.0 + jnp.tanh(math.sqrt(2.0 / math.pi) * (gt + 0.044715 * (gt * gt * gt))))
        y_ref[:, cs] = (h * gelu).astype(y_ref.dtype)


def _rglru(u, gate, conv_w, conv_b, wg, bg, lam, name, T=512, LB=512):
    S = u.shape[0]
    W = LRU_WIDTH
    ncb = W // LB
    kern = functools.partial(_rglru_kernel, T=T, LB=LB)
    return pl.pallas_call(
        kern,
        grid=(ncb, S // T),
        in_specs=[
            pl.BlockSpec((T, LB), lambda n, i: (i, n)),
            pl.BlockSpec((T, LB), lambda n, i: (i, n)),
            pl.BlockSpec((CONV_WIDTH, LB), lambda n, i: (0, n)),
            pl.BlockSpec((1, LB), lambda n, i: (0, n)),
            pl.BlockSpec((LB // LANES, LRU_BLOCK_WIDTH, 2 * LRU_BLOCK_WIDTH), lambda n, i: (n, 0, 0)),
            pl.BlockSpec((LB // LANES, 1, 2 * LRU_BLOCK_WIDTH), lambda n, i: (n, 0, 0)),
            pl.BlockSpec((1, LB), lambda n, i: (0, n)),
        ],
        out_specs=pl.BlockSpec((T, LB), lambda n, i: (i, n)),
        out_shape=jax.ShapeDtypeStruct((S, W), BF16),
        scratch_shapes=[pltpu.VMEM((T + SUBLANES, LB), F32), pltpu.VMEM((1, LB), F32)],
        compiler_params=_params(("arbitrary", "arbitrary"), 32),
        name=name,
    )(u, gate, conv_w, conv_b.reshape(1, W), wg, bg, lam.reshape(1, W))


def _cast_weight_once(w_ref, wb_ref):
    @pl.when(pl.program_id(0) == 0)
    def _():
        wb_ref[...] = w_ref[...].astype(BF16)


def _out_a_kernel(x_ref, y_ref, m_ref, w_ref, o_ref, wb_ref):
    _cast_weight_once(w_ref, wb_ref)
    acc = jnp.dot(y_ref[...], wb_ref[0:LRU_WIDTH, :], preferred_element_type=F32)
    acc += jnp.dot(m_ref[...], wb_ref[LRU_WIDTH:LRU_WIDTH + MEM_WIDTH, :], preferred_element_type=F32)
    o_ref[...] = x_ref[...] + acc


def _out_a(x, y, mem_out, w, widx, name, tm=512):
    S, D = x.shape
    K = LRU_WIDTH + MEM_WIDTH
    row = pl.BlockSpec((tm, D), lambda i: (i, 0))
    return pl.pallas_call(
        _out_a_kernel,
        grid=(S // tm,),
        in_specs=[
            row,
            pl.BlockSpec((tm, LRU_WIDTH), lambda i: (i, 0)),
            pl.BlockSpec((tm, MEM_WIDTH), lambda i: (i, 0)),
            pl.BlockSpec((None, K, D), lambda i: (widx, 0, 0), pipeline_mode=pl.Buffered(1)),
        ],
        out_specs=row,
        out_shape=jax.ShapeDtypeStruct((S, D), F32),
        scratch_shapes=[pltpu.VMEM((K, D), BF16)],
        compiler_params=_params(("arbitrary",), 56),
        name=name,
    )(x, y, mem_out, w)


def _bucket_tables():
    qi = np.arange(Q_BLOCK)[:, None]
    ki = np.arange(2 * Q_BLOCK)[None, :]
    u = qi + Q_BLOCK - ki
    tables, valids = [], []
    for window, dilation in DIL_PATTERNS:
        n = np.maximum(u * dilation, 0)
        nf = np.maximum(n, 1).astype(np.float32)
        large = REL_MAX_EXACT + (np.log(nf / np.float32(REL_MAX_EXACT))
                                 / np.float32(math.log(REL_MAX_DISTANCE / REL_MAX_EXACT))
                                 * np.float32(REL_BUCKETS - REL_MAX_EXACT)).astype(np.int32)
        large = np.minimum(large, REL_BUCKETS - 1)
        tables.append(np.where(n < REL_MAX_EXACT, n, large).astype(np.int32))
        valids.append(((u >= 0) & (u <= window // dilation)).astype(np.int32))
    return np.stack(tables), np.stack(valids)


def _bias_kernel(tab_ref, idx_ref, valid_ref, o_ref):
    g = pl.program_id(0)
    idx = idx_ref[...]
    valid = valid_ref[...] > 0
    for h in range(DIL_HEADS):
        acc = jnp.zeros(idx.shape, F32)
        for b in range(REL_BUCKETS):
            acc = jnp.where(idx == b, tab_ref[b, g * DIL_HEADS + h], acc)
        o_ref[h] = jnp.where(valid, acc, NEG_INF)


def _bias_expand(rel_bias):
    idx, valid = _bucket_tables()
    blk = pl.BlockSpec((None, Q_BLOCK, 2 * Q_BLOCK), lambda g: (g, 0, 0))
    return pl.pallas_call(
        _bias_kernel,
        grid=(DIL_GROUPS,),
        in_specs=[pl.BlockSpec(memory_space=pltpu.SMEM), blk, blk],
        out_specs=pl.BlockSpec((DIL_HEADS, Q_BLOCK, 2 * Q_BLOCK), lambda g: (g, 0, 0)),
        out_shape=jax.ShapeDtypeStruct((DIL_GROUPS * DIL_HEADS, Q_BLOCK, 2 * Q_BLOCK), F32),
        compiler_params=_params(("arbitrary",), 32),
        name="bias_expand",
    )(rel_bias, jnp.asarray(idx), jnp.asarray(valid))


def _dil_kernel(q_ref, kc_ref, kp_ref, vc_ref, vp_ref, b_ref, o_ref, *, LQ):
    is_first = pl.program_id(1) == 0
    for h in range(DIL_HEADS):
        hs = slice(h * HEAD_DIM, (h + 1) * HEAD_DIM)
        for sb in range(LQ // Q_BLOCK):
            rows = slice(sb * Q_BLOCK, (sb + 1) * Q_BLOCK)
            q = q_ref[rows, hs]
            if sb == 0:
                kk = jnp.concatenate([kp_ref[:, hs], kc_ref[0:Q_BLOCK, hs]], axis=0)
                vv = jnp.concatenate([vp_ref[:, hs], vc_ref[0:Q_BLOCK, hs]], axis=0)
            else:
                kk = kc_ref[(sb - 1) * Q_BLOCK:(sb + 1) * Q_BLOCK, hs]
                vv = vc_ref[(sb - 1) * Q_BLOCK:(sb + 1) * Q_BLOCK, hs]
            s = lax.dot_general(q, kk, (((1,), (1,)), ((), ())), preferred_element_type=F32) + b_ref[h]
            if sb == 0:
                col = lax.broadcasted_iota(jnp.int32, s.shape, 1)
                s = jnp.where(jnp.logical_and(is_first, col < Q_BLOCK), NEG_INF, s)
            m = jnp.max(s, axis=-1, keepdims=True)
            p = jnp.exp(s - m)
            den = jnp.sum(p, axis=-1, keepdims=True)
            o = jnp.dot(p.astype(BF16), vv, preferred_element_type=F32) / den
            lse = m + jnp.log(den)
            o_ref[rows, hs] = o
            o_ref[rows, DIL_OUT_WIDTH + h * HEAD_DIM:DIL_OUT_WIDTH + (h + 1) * HEAD_DIM] = (
                jnp.broadcast_to(lse, (Q_BLOCK, HEAD_DIM)))


def _dil_attn(q, k, v, bias, g, name, LQ=512):
    d, L, _ = q.shape
    assert d == DIL_PATTERNS[g][1]
    ow = 2 * DIL_OUT_WIDTH
    ratio = LQ // Q_BLOCK
    kern = functools.partial(_dil_kernel, LQ=LQ)
    cur = pl.BlockSpec((None, LQ, HEAD_BLOCK), lambda r, n: (r, n, 0))
    prev = pl.BlockSpec((None, Q_BLOCK, HEAD_BLOCK), lambda r, n: (r, jnp.maximum(n * ratio - 1, 0), 0))
    return pl.pallas_call(
        kern,
        grid=(d, L // LQ),
        in_specs=[cur, cur, prev, cur, prev,
                  pl.BlockSpec((DIL_HEADS, Q_BLOCK, 2 * Q_BLOCK), lambda r, n: (g, 0, 0))],
        out_specs=pl.BlockSpec((None, LQ, ow), lambda r, n: (r, n, 0)),
        out_shape=jax.ShapeDtypeStruct((d, L, ow), F32),
        compiler_params=_params(("arbitrary", "arbitrary"), 32),
        name=name,
    )(q, k, k, v, v, bias)


def _out_b_kernel(x_ref, o0_ref, o1_ref, o2_ref, m_ref, w_ref, o_ref, wb_ref, s1_ref, s2_ref, *, tm):
    _cast_weight_once(w_ref, wb_ref)
    nslab = 2 * DIL_HEADS
    for og_ref, s_ref in ((o1_ref, s1_ref), (o2_ref, s2_ref)):
        d = og_ref.shape[0]
        for r in range(d):
            for c in range(nslab):
                s_ref[c, pl.ds(r, tm // d, stride=d), :] = og_ref[r, :, c * LANES:(c + 1) * LANES]
    parts = []
    for h in range(DIL_HEADS):
        hs = slice(h * HEAD_DIM, (h + 1) * HEAD_DIM)
        ls = slice(DIL_OUT_WIDTH + h * HEAD_DIM, DIL_OUT_WIDTH + (h + 1) * HEAD_DIM)
        lses = [o0_ref[0, :, ls], s1_ref[DIL_HEADS + h], s2_ref[DIL_HEADS + h]]
        outs = [o0_ref[0, :, hs], s1_ref[h], s2_ref[h]]
        mx = jnp.maximum(jnp.maximum(lses[0], lses[1]), lses[2])
        es = [jnp.exp(l - mx) for l in lses]
        den = es[0] + es[1] + es[2]
        num = es[0] * outs[0] + es[1] * outs[1] + es[2] * outs[2]
        parts.append((num / den).astype(BF16))
    dil = jnp.concatenate(parts, axis=-1)
    acc = jnp.dot(dil, wb_ref[0:DIL_OUT_WIDTH, :], preferred_element_type=F32)
    acc += jnp.dot(m_ref[...], wb_ref[DIL_OUT_WIDTH:DIL_OUT_WIDTH + MEM_WIDTH, :], preferred_element_type=F32)
    o_ref[...] = x_ref[...] + acc


def _out_b(x, ogs, mem_out, w, widx, name, tm=512):
    S, D = x.shape
    ow = 2 * DIL_OUT_WIDTH
    K = DIL_OUT_WIDTH + MEM_WIDTH
    og_specs = [pl.BlockSpec((og.shape[0], tm // og.shape[0], ow), lambda i: (0, i, 0)) for og in ogs]
    row = pl.BlockSpec((tm, D), lambda i: (i, 0))
    return pl.pallas_call(
        functools.partial(_out_b_kernel, tm=tm),
        grid=(S // tm,),
        in_specs=[
            row,
            *og_specs,
            pl.BlockSpec((tm, MEM_WIDTH), lambda i: (i, 0)),
            pl.BlockSpec((None, K, D), lambda i: (widx, 0, 0), pipeline_mode=pl.Buffered(1)),
        ],
        out_specs=row,
        out_shape=jax.ShapeDtypeStruct((S, D), F32),
        scratch_shapes=[pltpu.VMEM((K, D), BF16), pltpu.VMEM((ow // LANES, tm, LANES), F32),
                        pltpu.VMEM((ow // LANES, tm, LANES), F32)],
        compiler_params=_params(("arbitrary",), 56),
        name=name,
    )(x, ogs[0], ogs[1], ogs[2], mem_out, w)


def kernel(x, mem, norm_mix_g, norm_mlp_g, mlp_w1, mlp_w2, mem_norm_g, mem_w_kv, mem_q_norm_g, mem_k_norm_g, a_w_in, a_conv_w, a_conv_b, a_gate_r_w, a_gate_r_b, a_gate_i_w, a_gate_i_b, a_lambda, a_w_out, kv_norm_g, kv_w, k_norm_g, rel_bias, b_w_q, b_q_norm_g, b_w_out):
    B, S, D = x.shape
    assert (B, S, D) == (1, SEQ, D_MODEL) and mem.shape == (1, MEM_TOKENS, D_MODEL)
    xs = x[0]
    ones = jnp.ones((1, HEAD_DIM), F32)

    mem_k, mem_v = _mem_kv(mem[0], mem_norm_g, mem_w_kv, mem_k_norm_g)
    bias = _bias_expand(rel_bias)
    w1b, w2b = mlp_w1[0].astype(BF16), mlp_w2[0].astype(BF16)

    for layer in range(DEPTH):
        mq_gain = mem_q_norm_g[layer][None, :]
        a_in, a_gain = xs, norm_mix_g[layer]
        if layer < N_A_LAYERS:
            i = layer
            nblk = A_IN_WIDTH // HEAD_BLOCK
            gains = jnp.concatenate([jnp.tile(ones, (nblk - 1, 1)), mq_gain], axis=0)
            u, gate, mq = _proj(a_in, a_gain, a_w_in[i].astype(BF16), gains,
                                ((3, 1, F32), (3, 1, F32), (1, 1, BF16)), (False,) * (nblk - 1) + (True,),
                                ATTN_SCALE, f"a{i}_in_proj", tm=512, vmem_mib=56)
            mem_out = _mem_attn(mq.reshape(S, MEM_WIDTH), 0, mem_k, mem_v, layer, f"a{i}_mem_attn")
            wg = jnp.concatenate([a_gate_r_w[i], a_gate_i_w[i]], axis=-1).astype(BF16)
            bg = jnp.concatenate([a_gate_r_b[i], a_gate_i_b[i]], axis=-1)[:, None, :]
            y = _rglru(u.reshape(S, LRU_WIDTH), gate.reshape(S, LRU_WIDTH), a_conv_w[i], a_conv_b[i], wg, bg,
                       a_lambda[i], f"a{i}_rglru")
            xs = _out_a(xs, y, mem_out, a_w_out, i, f"a{i}_out_proj")
        else:
            j = layer - N_A_LAYERS
            dils = tuple(d for _, d in DIL_PATTERNS)
            dil_groups = tuple((1, d, BF16) for d in dils)
            if j == 0:
                kv_gains = jnp.concatenate([k_norm_g, jnp.tile(ones, (DIL_GROUPS, 1))], axis=0)
                kvs = _proj(xs, kv_norm_g, kv_w.astype(BF16), kv_gains, dil_groups + dil_groups,
                            (True,) * DIL_GROUPS + (False,) * DIL_GROUPS, 1.0, "shared_kv", tm=1024, vmem_mib=56)
            q_gains = jnp.concatenate([b_q_norm_g[j], mq_gain], axis=0)
            qs = _proj(a_in, a_gain, b_w_q[j].astype(BF16), q_gains, dil_groups + ((1, 1, BF16),),
                       (True,) * (DIL_GROUPS + 1), ATTN_SCALE, f"b{j}_q_proj", tm=1024, vmem_mib=48)
            mem_out = _mem_attn(qs[DIL_GROUPS].reshape(S, MEM_WIDTH), 0, mem_k, mem_v, layer, f"b{j}_mem_attn")
            ogs = [_dil_attn(qs[g], kvs[g], kvs[DIL_GROUPS + g], bias, g, f"b{j}_dil{g}")
                   for g in range(DIL_GROUPS)]
            xs = _out_b(xs, ogs, mem_out, b_w_out, j, f"b{j}_out_proj")
        nxt = (mlp_w1, mlp_w2, layer + 1) if layer + 1 < DEPTH else None
        xs, w1b, w2b = _mlp(xs, norm_mlp_g[layer], w1b, w2b, nxt, f"mlp{layer}")
    return xs[None]
```

```python
import functools
import math

import numpy as np
import jax
import jax.numpy as jnp
from jax import lax
from jax.experimental import pallas as pl
from jax.experimental.pallas import tpu as pltpu

F32 = jnp.float32
BF16 = jnp.bfloat16

D_MODEL = 2048
SEQ = 8192
DEPTH = 4
N_A_LAYERS = DEPTH // 2
HEAD_DIM = 128
D_FF = 4 * D_MODEL
LRU_WIDTH = 3 * D_MODEL // 4
LRU_BLOCK_WIDTH = 128
LRU_BLOCKS = LRU_WIDTH // LRU_BLOCK_WIDTH
CONV_WIDTH = 4
RG_C = 8.0
MEM_TOKENS = 256
MEM_HEADS = 4
MEM_WIDTH = MEM_HEADS * HEAD_DIM
DIL_PATTERNS = ((128, 1), (512, 4), (2048, 16))
DIL_GROUPS = len(DIL_PATTERNS)
DIL_HEADS = 4
DIL_WIDTH = DIL_GROUPS * DIL_HEADS * HEAD_DIM
DIL_OUT_WIDTH = DIL_HEADS * HEAD_DIM
Q_BLOCK = 128
REL_BUCKETS = 32
REL_MAX_EXACT = REL_BUCKETS // 2
REL_MAX_DISTANCE = 2048
A_IN_WIDTH = 2 * LRU_WIDTH + MEM_WIDTH
NORM_EPS = 1e-6
NEG_INF = -1e30
ATTN_SCALE = HEAD_DIM ** -0.5

V7X_VMEM_BYTES = 64 * 1024 * 1024
LANES = 128
SUBLANES = 8
HEAD_BLOCK = DIL_HEADS * HEAD_DIM


def _params(semantics, vmem_mib):
    assert vmem_mib * 1024 * 1024 < V7X_VMEM_BYTES
    return pltpu.CompilerParams(dimension_semantics=semantics,
                                vmem_limit_bytes=vmem_mib * 1024 * 1024)


def _rms_rows(x, g):
    ms = jnp.mean(x * x, axis=-1, keepdims=True)
    return x * lax.rsqrt(ms + NORM_EPS) * g


def _head_norm_cols(a, g, scale):
    parts = []
    for h in range(a.shape[1] // HEAD_DIM):
        ah = a[:, h * HEAD_DIM:(h + 1) * HEAD_DIM]
        y = ah * lax.rsqrt(jnp.mean(ah * ah, axis=-1, keepdims=True) + NORM_EPS) * g
        parts.append(y * scale if scale != 1.0 else y)
    return jnp.concatenate(parts, axis=-1)


def _proj_kernel(a_ref, *rest, groups, normed, scale, tm, has_norm, has_scr):
    rest = list(rest)
    gn_ref = rest.pop(0) if has_norm else None
    w_ref, hg_ref = rest.pop(0), rest.pop(0)
    o_refs = [rest.pop(0) for _ in groups]
    hn_s = rest.pop(0) if has_norm else None
    scr = rest.pop(0) if has_scr else None
    if has_norm:
        hn_s[...] = _rms_rows(a_ref[...], gn_ref[...]).astype(BF16)
        lhs_ref = hn_s
    else:
        lhs_ref = a_ref
    jb = 0
    for (nb, d, _), o_ref in zip(groups, o_refs):
        for k in range(nb):
            acc = jnp.dot(lhs_ref[...], w_ref[:, jb * HEAD_BLOCK:(jb + 1) * HEAD_BLOCK],
                          preferred_element_type=F32)
            if normed[jb]:
                acc = _head_norm_cols(acc, hg_ref[jb:jb + 1, :], scale)
            if d == 1:
                o_ref[0, :, k * HEAD_BLOCK:(k + 1) * HEAD_BLOCK] = acc.astype(o_ref.dtype)
            else:
                for h in range(HEAD_BLOCK // LANES):
                    scr[h] = acc[:, h * LANES:(h + 1) * LANES]
                for r in range(d):
                    for h in range(HEAD_BLOCK // LANES):
                        o_ref[r, :, h * LANES:(h + 1) * LANES] = (
                            scr[h, pl.ds(r, tm // d, stride=d), :].astype(o_ref.dtype))
            jb += 1


def _proj(a, norm_gain, w, head_gains, groups, normed, scale, name, tm, vmem_mib):
    S, D = a.shape
    N = w.shape[1]
    nblk = N // HEAD_BLOCK
    assert sum(nb for nb, _, _ in groups) == len(normed) == nblk
    assert all(nb == 1 for nb, d, _ in groups if d > 1)
    has_norm = norm_gain is not None
    has_scr = any(d > 1 for _, d, _ in groups)
    kern = functools.partial(_proj_kernel, groups=groups, normed=normed, scale=scale, tm=tm,
                             has_norm=has_norm, has_scr=has_scr)
    in_specs = [pl.BlockSpec((tm, D), lambda i: (i, 0))]
    args = [a]
    if has_norm:
        in_specs.append(pl.BlockSpec((1, D), lambda i: (0, 0)))
        args.append(norm_gain.reshape(1, D))
    in_specs += [pl.BlockSpec((D, N), lambda i: (0, 0), pipeline_mode=pl.Buffered(1)),
                 pl.BlockSpec((nblk, HEAD_DIM), lambda i: (0, 0))]
    args += [w, head_gains]
    scratch = []
    if has_norm:
        scratch.append(pltpu.VMEM((tm, D), BF16))
    if has_scr:
        scratch.append(pltpu.VMEM((HEAD_BLOCK // LANES, tm, LANES), F32))
    return pl.pallas_call(
        kern,
        grid=(S // tm,),
        in_specs=in_specs,
        out_specs=[pl.BlockSpec((d, tm // d, nb * HEAD_BLOCK), lambda i: (0, i, 0)) for nb, d, _ in groups],
        out_shape=[jax.ShapeDtypeStruct((d, S // d, nb * HEAD_BLOCK), dt) for nb, d, dt in groups],
        scratch_shapes=scratch,
        compiler_params=_params(("parallel",), vmem_mib),
        name=name,
    )(*args)


def _mlp_kernel(x_ref, g_ref, w1_ref, w2_ref, *rest, has_next):
    if has_next:
        w1n_ref, w2n_ref, o_ref, w1o_ref, w2o_ref, hn_ref = rest
    else:
        o_ref, hn_ref = rest

    @pl.when(pl.program_id(1) == 0)
    def _():
        x = x_ref[...]
        hn_ref[...] = _rms_rows(x, g_ref[...]).astype(BF16)
        o_ref[...] = x

    h = jnp.dot(hn_ref[...], w1_ref[...], preferred_element_type=F32)
    h = jnp.maximum(h, 0.0)
    h = (h * h).astype(BF16)
    o_ref[...] += jnp.dot(h, w2_ref[...], preferred_element_type=F32)

    if has_next:
        w1o_ref[...] = w1n_ref[...].astype(BF16)
        w2o_ref[...] = w2n_ref[...].astype(BF16)


def _mlp(x, g, w1b, w2b, nxt, name, tm=1024, tf=512):
    S, D = x.shape
    F = w1b.shape[1]
    ni, nj = S // tm, F // tf
    has_next = nxt is not None
    row_spec = pl.BlockSpec((tm, D), lambda i, j: (i, 0))
    in_specs = [row_spec, pl.BlockSpec((1, D), lambda i, j: (0, 0)),
                pl.BlockSpec((D, tf), lambda i, j: (0, j)),
                pl.BlockSpec((tf, D), lambda i, j: (j, 0))]
    args = [x, g.reshape(1, D), w1b, w2b]
    out_specs = [row_spec]
    out_shape = [jax.ShapeDtypeStruct((S, D), F32)]
    if has_next:
        w1s, w2s, nl = nxt
        r1, r2 = D // (ni * nj), F // (ni * nj)
        in_specs += [pl.BlockSpec((None, r1, F), lambda i, j: (nl, i * nj + j, 0)),
                     pl.BlockSpec((None, r2, D), lambda i, j: (nl, i * nj + j, 0))]
        args += [w1s, w2s]
        out_specs += [pl.BlockSpec((r1, F), lambda i, j: (i * nj + j, 0)),
                      pl.BlockSpec((r2, D), lambda i, j: (i * nj + j, 0))]
        out_shape += [jax.ShapeDtypeStruct((D, F), BF16),
                      jax.ShapeDtypeStruct((F, D), BF16)]
    kern = functools.partial(_mlp_kernel, has_next=has_next)
    outs = pl.pallas_call(
        kern,
        grid=(ni, nj),
        in_specs=in_specs,
        out_specs=out_specs,
        out_shape=out_shape,
        scratch_shapes=[pltpu.VMEM((tm, D), BF16)],
        compiler_params=_params(("parallel", "arbitrary"), 60),
        name=name,
    )(*args)
    return tuple(outs) if has_next else (outs[0], None, None)


def _mem_kv_kernel(mem_ref, g_ref, w_ref, kg_ref, k_ref, v_ref):
    hn = _rms_rows(mem_ref[...], g_ref[...]).astype(BF16)
    kv = jnp.dot(hn, w_ref[...].astype(BF16), preferred_element_type=F32)
    k_ref[...] = _head_norm_cols(kv[:, :MEM_WIDTH], kg_ref[...], 1.0).astype(BF16)
    v_ref[...] = kv[:, MEM_WIDTH:].astype(BF16)


def _mem_kv(mem, mem_norm_g, mem_w_kv, mem_k_norm_g):
    M, D = mem.shape
    out = jax.ShapeDtypeStruct((DEPTH, M, MEM_WIDTH), BF16)
    return pl.pallas_call(
        _mem_kv_kernel,
        grid=(DEPTH,),
        in_specs=[
            pl.BlockSpec((M, D), lambda l: (0, 0)),
            pl.BlockSpec((None, 1, D), lambda l: (l, 0, 0)),
            pl.BlockSpec((None, D, 2 * MEM_WIDTH), lambda l: (l, 0, 0)),
            pl.BlockSpec((None, 1, HEAD_DIM), lambda l: (l, 0, 0)),
        ],
        out_specs=[pl.BlockSpec((None, M, MEM_WIDTH), lambda l: (l, 0, 0)),
                   pl.BlockSpec((None, M, MEM_WIDTH), lambda l: (l, 0, 0))],
        out_shape=[out, out],
        compiler_params=_params(("arbitrary",), 40),
        name="mem_kv",
    )(mem, mem_norm_g.reshape(DEPTH, 1, D), mem_w_kv, mem_k_norm_g.reshape(DEPTH, 1, HEAD_DIM))


def _mem_attn_kernel(q_ref, k_ref, v_ref, o_ref):
    for h in range(MEM_HEADS):
        hs = slice(h * HEAD_DIM, (h + 1) * HEAD_DIM)
        q = q_ref[:, hs].astype(BF16)
        s = lax.dot_general(q, k_ref[:, hs], (((1,), (1,)), ((), ())), preferred_element_type=F32)
        m = jnp.max(s, axis=-1, keepdims=True)
        p = jnp.exp(s - m)
        den = jnp.sum(p, axis=-1, keepdims=True)
        o = jnp.dot(p.astype(BF16), v_ref[:, hs], preferred_element_type=F32) / den
        o_ref[:, hs] = o.astype(o_ref.dtype)


def _mem_attn(proj, q_col_block, mem_k, mem_v, layer, name, tm=512):
    S = proj.shape[0]
    return pl.pallas_call(
        _mem_attn_kernel,
        grid=(S // tm,),
        in_specs=[
            pl.BlockSpec((tm, MEM_WIDTH), lambda i: (i, q_col_block)),
            pl.BlockSpec((None, MEM_TOKENS, MEM_WIDTH), lambda i: (layer, 0, 0)),
            pl.BlockSpec((None, MEM_TOKENS, MEM_WIDTH), lambda i: (layer, 0, 0)),
        ],
        out_specs=pl.BlockSpec((tm, MEM_WIDTH), lambda i: (i, 0)),
        out_shape=jax.ShapeDtypeStruct((S, MEM_WIDTH), BF16),
        compiler_params=_params(("parallel",), 32),
        name=name,
    )(proj, mem_k, mem_v)


def _rglru_kernel(u_ref, gate_ref, cw_ref, cb_ref, wg_ref, bg_ref, lam_ref, y_ref, us, hs, utail, hcar,
                  *, T, LB, pitch):
    nseg = SUBLANES
    seg = T // nseg

    @pl.when(pl.program_id(1) == 0)
    def _():
        utail[...] = jnp.zeros_like(utail)
        hcar[...] = jnp.zeros_like(hcar)

    lam = lam_ref[...]
    sp = jnp.maximum(-lam, 0.0) + jnp.log1p(jnp.exp(-jnp.abs(lam)))
    first_seg = lax.broadcasted_iota(jnp.int32, (nseg, LANES), 0) == 0

    for c in range(LB // LANES):
        cs = slice(c * LANES, (c + 1) * LANES)
        for s in range(nseg):
            us[c, s * pitch:s * pitch + seg, :] = u_ref[s * seg:(s + 1) * seg, cs]
        U = [us[c, pl.ds(p, nseg, stride=pitch), :] for p in range(seg)]

        before = {}
        for k in range(1, CONV_WIDTH):
            cur = pltpu.roll(U[seg - k], 1, axis=0)
            prev = pltpu.roll(utail[k - 1, :, cs], 1, axis=0)
            before[-k] = jnp.where(first_seg, prev, cur)
        for k in range(1, CONV_WIDTH):
            utail[k - 1, :, cs] = U[seg - k]
        tap = lambda p: U[p] if p >= 0 else before[p]

        cwb = [jnp.broadcast_to(cw_ref[k:k + 1, cs], (nseg, LANES)) for k in range(CONV_WIDTH)]
        cbb = jnp.broadcast_to(cb_ref[:, cs], (nseg, LANES))
        xc = jnp.stack([cbb + (tap(p - 3) * cwb[0] + tap(p - 2) * cwb[1] + tap(p - 1) * cwb[2] + U[p] * cwb[3])
                        for p in range(seg)])

        z = jnp.dot(xc.reshape(T, LANES).astype(BF16), wg_ref[c], preferred_element_type=F32) + bg_ref[c]
        r = jax.nn.sigmoid(z[:, :LANES]).reshape(seg, nseg, LANES)
        ig = jax.nn.sigmoid(z[:, LANES:]).reshape(seg, nseg, LANES)
        log_a = (-RG_C * r) * sp[:, cs]
        a = jnp.exp(log_a)
        om = 1.0 - a * a
        root = jnp.where(om > 0.0, om * lax.rsqrt(om), 0.0)
        b = root * (ig * xc)

        hl, pr = [b[0]], [a[0]]
        for p in range(1, seg):
            hl.append(a[p] * hl[-1] + b[p])
            pr.append(a[p] * pr[-1])

        h_in = [hcar[:, cs]]
        for s in range(nseg):
            h_in.append(hl[-1][s:s + 1, :] + pr[-1][s:s + 1, :] * h_in[-1])
        hcar[:, cs] = h_in[nseg]
        h_in = jnp.concatenate(h_in[:nseg], axis=0)

        for p in range(seg):
            hs[c, pl.ds(p, nseg, stride=pitch), :] = hl[p] + pr[p] * h_in

        for s in range(nseg):
            rows = slice(s * seg, (s + 1) * seg)
            gt = gate_ref[rows, cs]
            gelu = 0.5 * gt * (1.0 + jnp.tanh(math.sqrt(2.0 / math.pi) * (gt + 0.044715 * (gt * gt * gt))))
            y_ref[rows, cs] = (hs[c, s * pitch:s * pitch + seg, :] * gelu).astype(y_ref.dtype)


def _rglru(u, gate, conv_w, conv_b, wg, bg, lam, name, T=512, LB=512):
    S = u.shape[0]
    W = LRU_WIDTH
    ncb = W // LB
    seg = T // SUBLANES
    pitch = seg + SUBLANES if (seg // SUBLANES) % 2 == 0 else seg
    slab = pltpu.VMEM((LB // LANES, SUBLANES * pitch, LANES), F32)
    kern = functools.partial(_rglru_kernel, T=T, LB=LB, pitch=pitch)
    return pl.pallas_call(
        kern,
        grid=(ncb, S // T),
        in_specs=[
            pl.BlockSpec((T, LB), lambda n, i: (i, n)),
            pl.BlockSpec((T, LB), lambda n, i: (i, n)),
            pl.BlockSpec((CONV_WIDTH, LB), lambda n, i: (0, n)),
            pl.BlockSpec((1, LB), lambda n, i: (0, n)),
            pl.BlockSpec((LB // LANES, LRU_BLOCK_WIDTH, 2 * LRU_BLOCK_WIDTH), lambda n, i: (n, 0, 0)),
            pl.BlockSpec((LB // LANES, 1, 2 * LRU_BLOCK_WIDTH), lambda n, i: (n, 0, 0)),
            pl.BlockSpec((1, LB), lambda n, i: (0, n)),
        ],
        out_specs=pl.BlockSpec((T, LB), lambda n, i: (i, n)),
        out_shape=jax.ShapeDtypeStruct((S, W), BF16),
        scratch_shapes=[slab, slab, pltpu.VMEM((CONV_WIDTH - 1, SUBLANES, LB), F32), pltpu.VMEM((1, LB), F32)],
        compiler_params=_params(("arbitrary", "arbitrary"), 32),
        name=name,
    )(u, gate, conv_w, conv_b.reshape(1, W), wg, bg, lam.reshape(1, W))


def _cast_weight_once(w_ref, wb_ref):
    @pl.when(pl.program_id(0) == 0)
    def _():
        wb_ref[...] = w_ref[...].astype(BF16)


def _out_a_kernel(x_ref, y_ref, m_ref, w_ref, o_ref, wb_ref):
    _cast_weight_once(w_ref, wb_ref)
    acc = jnp.dot(y_ref[...], wb_ref[0:LRU_WIDTH, :], preferred_element_type=F32)
    acc += jnp.dot(m_ref[...], wb_ref[LRU_WIDTH:LRU_WIDTH + MEM_WIDTH, :], preferred_element_type=F32)
    o_ref[...] = x_ref[...] + acc


def _out_a(x, y, mem_out, w, widx, name, tm=512):
    S, D = x.shape
    K = LRU_WIDTH + MEM_WIDTH
    row = pl.BlockSpec((tm, D), lambda i: (i, 0))
    return pl.pallas_call(
        _out_a_kernel,
        grid=(S // tm,),
        in_specs=[
            row,
            pl.BlockSpec((tm, LRU_WIDTH), lambda i: (i, 0)),
            pl.BlockSpec((tm, MEM_WIDTH), lambda i: (i, 0)),
            pl.BlockSpec((None, K, D), lambda i: (widx, 0, 0), pipeline_mode=pl.Buffered(1)),
        ],
        out_specs=row,
        out_shape=jax.ShapeDtypeStruct((S, D), F32),
        scratch_shapes=[pltpu.VMEM((K, D), BF16)],
        compiler_params=_params(("arbitrary",), 56),
        name=name,
    )(x, y, mem_out, w)


def _bucket_tables():
    qi = np.arange(Q_BLOCK)[:, None]
    ki = np.arange(2 * Q_BLOCK)[None, :]
    u = qi + Q_BLOCK - ki
    tables, valids = [], []
    for window, dilation in DIL_PATTERNS:
        n = np.maximum(u * dilation, 0)
        nf = np.maximum(n, 1).astype(np.float32)
        large = REL_MAX_EXACT + (np.log(nf / np.float32(REL_MAX_EXACT))
                                 / np.float32(math.log(REL_MAX_DISTANCE / REL_MAX_EXACT))
                                 * np.float32(REL_BUCKETS - REL_MAX_EXACT)).astype(np.int32)
        large = np.minimum(large, REL_BUCKETS - 1)
        tables.append(np.where(n < REL_MAX_EXACT, n, large).astype(np.int32))
        valids.append(((u >= 0) & (u <= window // dilation)).astype(np.int32))
    return np.stack(tables), np.stack(valids)


def _bias_kernel(tab_ref, idx_ref, valid_ref, o_ref):
    g = pl.program_id(0)
    idx = idx_ref[...]
    valid = valid_ref[...] > 0
    for h in range(DIL_HEADS):
        acc = jnp.zeros(idx.shape, F32)
        for b in range(REL_BUCKETS):
            acc = jnp.where(idx == b, tab_ref[b, g * DIL_HEADS + h], acc)
        o_ref[h] = jnp.where(valid, acc, NEG_INF)


def _bias_expand(rel_bias):
    idx, valid = _bucket_tables()
    blk = pl.BlockSpec((None, Q_BLOCK, 2 * Q_BLOCK), lambda g: (g, 0, 0))
    return pl.pallas_call(
        _bias_kernel,
        grid=(DIL_GROUPS,),
        in_specs=[pl.BlockSpec(memory_space=pltpu.SMEM), blk, blk],
        out_specs=pl.BlockSpec((DIL_HEADS, Q_BLOCK, 2 * Q_BLOCK), lambda g: (g, 0, 0)),
        out_shape=jax.ShapeDtypeStruct((DIL_GROUPS * DIL_HEADS, Q_BLOCK, 2 * Q_BLOCK), F32),
        compiler_params=_params(("arbitrary",), 32),
        name="bias_expand",
    )(rel_bias, jnp.asarray(idx), jnp.asarray(valid))


def _dil_kernel(q_ref, kc_ref, kp_ref, vc_ref, vp_ref, b_ref, o_ref, *, LQ):
    is_first = pl.program_id(1) == 0
    for h in range(DIL_HEADS):
        hs = slice(h * HEAD_DIM, (h + 1) * HEAD_DIM)
        for sb in range(LQ // Q_BLOCK):
            rows = slice(sb * Q_BLOCK, (sb + 1) * Q_BLOCK)
            q = q_ref[rows, hs]
            if sb == 0:
                kk = jnp.concatenate([kp_ref[:, hs], kc_ref[0:Q_BLOCK, hs]], axis=0)
                vv = jnp.concatenate([vp_ref[:, hs], vc_ref[0:Q_BLOCK, hs]], axis=0)
            else:
                kk = kc_ref[(sb - 1) * Q_BLOCK:(sb + 1) * Q_BLOCK, hs]
                vv = vc_ref[(sb - 1) * Q_BLOCK:(sb + 1) * Q_BLOCK, hs]
            s = lax.dot_general(q, kk, (((1,), (1,)), ((), ())), preferred_element_type=F32) + b_ref[h]
            if sb == 0:
                col = lax.broadcasted_iota(jnp.int32, s.shape, 1)
                s = jnp.where(jnp.logical_and(is_first, col < Q_BLOCK), NEG_INF, s)
            m = jnp.max(s, axis=-1, keepdims=True)
            p = jnp.exp(s - m)
            den = jnp.sum(p, axis=-1, keepdims=True)
            o = jnp.dot(p.astype(BF16), vv, preferred_element_type=F32) / den
            lse = m + jnp.log(den)
            o_ref[rows, hs] = o
            o_ref[rows, DIL_OUT_WIDTH + h * HEAD_DIM:DIL_OUT_WIDTH + (h + 1) * HEAD_DIM] = (
                jnp.broadcast_to(lse, (Q_BLOCK, HEAD_DIM)))


def _dil_attn(q, k, v, bias, g, name, LQ=512):
    d, L, _ = q.shape
    assert d == DIL_PATTERNS[g][1]
    ow = 2 * DIL_OUT_WIDTH
    ratio = LQ // Q_BLOCK
    kern = functools.partial(_dil_kernel, LQ=LQ)
    cur = pl.BlockSpec((None, LQ, HEAD_BLOCK), lambda r, n: (r, n, 0))
    prev = pl.BlockSpec((None, Q_BLOCK, HEAD_BLOCK), lambda r, n: (r, jnp.maximum(n * ratio - 1, 0), 0))
    return pl.pallas_call(
        kern,
        grid=(d, L // LQ),
        in_specs=[cur, cur, prev, cur, prev,
                  pl.BlockSpec((DIL_HEADS, Q_BLOCK, 2 * Q_BLOCK), lambda r, n: (g, 0, 0))],
        out_specs=pl.BlockSpec((None, LQ, ow), lambda r, n: (r, n, 0)),
        out_shape=jax.ShapeDtypeStruct((d, L, ow), F32),
        compiler_params=_params(("arbitrary", "arbitrary"), 32),
        name=name,
    )(q, k, k, v, v, bias)


def _out_b_kernel(x_ref, o0_ref, o1_ref, o2_ref, m_ref, w_ref, o_ref, wb_ref, s1_ref, s2_ref, *, tm):
    _cast_weight_once(w_ref, wb_ref)
    nslab = 2 * DIL_HEADS
    for og_ref, s_ref in ((o1_ref, s1_ref), (o2_ref, s2_ref)):
        d = og_ref.shape[0]
        for r in range(d):
            for c in range(nslab):
                s_ref[c, pl.ds(r, tm // d, stride=d), :] = og_ref[r, :, c * LANES:(c + 1) * LANES]
    parts = []
    for h in range(DIL_HEADS):
        hs = slice(h * HEAD_DIM, (h + 1) * HEAD_DIM)
        ls = slice(DIL_OUT_WIDTH + h * HEAD_DIM, DIL_OUT_WIDTH + (h + 1) * HEAD_DIM)
        lses = [o0_ref[0, :, ls], s1_ref[DIL_HEADS + h], s2_ref[DIL_HEADS + h]]
        outs = [o0_ref[0, :, hs], s1_ref[h], s2_ref[h]]
        mx = jnp.maximum(jnp.maximum(lses[0], lses[1]), lses[2])
        es = [jnp.exp(l - mx) for l in lses]
        den = es[0] + es[1] + es[2]
        num = es[0] * outs[0] + es[1] * outs[1] + es[2] * outs[2]
        parts.append((num / den).astype(BF16))
    dil = jnp.concatenate(parts, axis=-1)
    acc = jnp.dot(dil, wb_ref[0:DIL_OUT_WIDTH, :], preferred_element_type=F32)
    acc += jnp.dot(m_ref[...], wb_ref[DIL_OUT_WIDTH:DIL_OUT_WIDTH + MEM_WIDTH, :], preferred_element_type=F32)
    o_ref[...] = x_ref[...] + acc


def _out_b(x, ogs, mem_out, w, widx, name, tm=512):
    S, D = x.shape
    ow = 2 * DIL_OUT_WIDTH
    K = DIL_OUT_WIDTH + MEM_WIDTH
    og_specs = [pl.BlockSpec((og.shape[0], tm // og.shape[0], ow), lambda i: (0, i, 0)) for og in ogs]
    row = pl.BlockSpec((tm, D), lambda i: (i, 0))
    return pl.pallas_call(
        functools.partial(_out_b_kernel, tm=tm),
        grid=(S // tm,),
        in_specs=[
            row,
            *og_specs,
            pl.BlockSpec((tm, MEM_WIDTH), lambda i: (i, 0)),
            pl.BlockSpec((None, K, D), lambda i: (widx, 0, 0), pipeline_mode=pl.Buffered(1)),
        ],
        out_specs=row,
        out_shape=jax.ShapeDtypeStruct((S, D), F32),
        scratch_shapes=[pltpu.VMEM((K, D), BF16), pltpu.VMEM((ow // LANES, tm, LANES), F32),
                        pltpu.VMEM((ow // LANES, tm, LANES), F32)],
        compiler_params=_params(("arbitrary",), 56),
        name=name,
    )(x, ogs[0], ogs[1], ogs[2], mem_out, w)


def kernel(x, mem, norm_mix_g, norm_mlp_g, mlp_w1, mlp_w2, mem_norm_g, mem_w_kv, mem_q_norm_g, mem_k_norm_g, a_w_in, a_conv_w, a_conv_b, a_gate_r_w, a_gate_r_b, a_gate_i_w, a_gate_i_b, a_lambda, a_w_out, kv_norm_g, kv_w, k_norm_g, rel_bias, b_w_q, b_q_norm_g, b_w_out):
    B, S, D = x.shape
    assert (B, S, D) == (1, SEQ, D_MODEL) and mem.shape == (1, MEM_TOKENS, D_MODEL)
    xs = x[0]
    ones = jnp.ones((1, HEAD_DIM), F32)

    mem_k, mem_v = _mem_kv(mem[0], mem_norm_g, mem_w_kv, mem_k_norm_g)
    bias = _bias_expand(rel_bias)
    w1b, w2b = mlp_w1[0].astype(BF16), mlp_w2[0].astype(BF16)

    for layer in range(DEPTH):
        mq_gain = mem_q_norm_g[layer][None, :]
        a_in, a_gain = xs, norm_mix_g[layer]
        if layer < N_A_LAYERS:
            i = layer
            nblk = A_IN_WIDTH // HEAD_BLOCK
            gains = jnp.concatenate([jnp.tile(ones, (nblk - 1, 1)), mq_gain], axis=0)
            u, gate, mq = _proj(a_in, a_gain, a_w_in[i].astype(BF16), gains,
                                ((3, 1, F32), (3, 1, F32), (1, 1, BF16)), (False,) * (nblk - 1) + (True,),
                                ATTN_SCALE, f"a{i}_in_proj", tm=512, vmem_mib=56)
            mem_out = _mem_attn(mq.reshape(S, MEM_WIDTH), 0, mem_k, mem_v, layer, f"a{i}_mem_attn")
            wg = jnp.concatenate([a_gate_r_w[i], a_gate_i_w[i]], axis=-1).astype(BF16)
            bg = jnp.concatenate([a_gate_r_b[i], a_gate_i_b[i]], axis=-1)[:, None, :]
            y = _rglru(u.reshape(S, LRU_WIDTH), gate.reshape(S, LRU_WIDTH), a_conv_w[i], a_conv_b[i], wg, bg,
                       a_lambda[i], f"a{i}_rglru")
            xs = _out_a(xs, y, mem_out, a_w_out, i, f"a{i}_out_proj")
        else:
            j = layer - N_A_LAYERS
            dils = tuple(d for _, d in DIL_PATTERNS)
            dil_groups = tuple((1, d, BF16) for d in dils)
            if j == 0:
                kv_gains = jnp.concatenate([k_norm_g, jnp.tile(ones, (DIL_GROUPS, 1))], axis=0)
                kvs = _proj(xs, kv_norm_g, kv_w.astype(BF16), kv_gains, dil_groups + dil_groups,
                            (True,) * DIL_GROUPS + (False,) * DIL_GROUPS, 1.0, "shared_kv", tm=1024, vmem_mib=56)
            q_gains = jnp.concatenate([b_q_norm_g[j], mq_gain], axis=0)
            qs = _proj(a_in, a_gain, b_w_q[j].astype(BF16), q_gains, dil_groups + ((1, 1, BF16),),
                       (True,) * (DIL_GROUPS + 1), ATTN_SCALE, f"b{j}_q_proj", tm=1024, vmem_mib=48)
            mem_out = _mem_attn(qs[DIL_GROUPS].reshape(S, MEM_WIDTH), 0, mem_k, mem_v, layer, f"b{j}_mem_attn")
            ogs = [_dil_attn(qs[g], kvs[g], kvs[DIL_GROUPS + g], bias, g, f"b{j}_dil{g}")
                   for g in range(DIL_GROUPS)]
            xs = _out_b(xs, ogs, mem_out, b_w_out, j, f"b{j}_out_proj")
        nxt = (mlp_w1, mlp_w2, layer + 1) if layer + 1 < DEPTH else None
        xs, w1b, w2b = _mlp(xs, norm_mlp_g[layer], w1b, w2b, nxt, f"mlp{layer}")
    return xs[None]
```

```python
import functools
import math

import numpy as np
import jax
import jax.numpy as jnp
from jax import lax
from jax.experimental import pallas as pl
from jax.experimental.pallas import tpu as pltpu

F32 = jnp.float32
BF16 = jnp.bfloat16

D_MODEL = 2048
SEQ = 8192
DEPTH = 4
N_A_LAYERS = DEPTH // 2
HEAD_DIM = 128
D_FF = 4 * D_MODEL
LRU_WIDTH = 3 * D_MODEL // 4
LRU_BLOCK_WIDTH = 128
LRU_BLOCKS = LRU_WIDTH // LRU_BLOCK_WIDTH
CONV_WIDTH = 4
RG_C = 8.0
MEM_TOKENS = 256
MEM_HEADS = 4
MEM_WIDTH = MEM_HEADS * HEAD_DIM
DIL_PATTERNS = ((128, 1), (512, 4), (2048, 16))
DIL_GROUPS = len(DIL_PATTERNS)
DIL_HEADS = 4
DIL_WIDTH = DIL_GROUPS * DIL_HEADS * HEAD_DIM
DIL_OUT_WIDTH = DIL_HEADS * HEAD_DIM
Q_BLOCK = 128
REL_BUCKETS = 32
REL_MAX_EXACT = REL_BUCKETS // 2
REL_MAX_DISTANCE = 2048
A_IN_WIDTH = 2 * LRU_WIDTH + MEM_WIDTH
NORM_EPS = 1e-6
NEG_INF = -1e30
ATTN_SCALE = HEAD_DIM ** -0.5

V7X_VMEM_BYTES = 64 * 1024 * 1024
SCOPED_VMEM_MIB = 60
assert SCOPED_VMEM_MIB * 1024 * 1024 < V7X_VMEM_BYTES
LANES = 128
SUBLANES = 8
HEAD_BLOCK = DIL_HEADS * HEAD_DIM

def _params(semantics):
    return pltpu.CompilerParams(dimension_semantics=semantics,
                                vmem_limit_bytes=SCOPED_VMEM_MIB * 1024 * 1024)


def _rms_rows(x, g):
    ms = jnp.mean(x * x, axis=-1, keepdims=True)
    return x * lax.rsqrt(ms + NORM_EPS) * g


def _head_norm_cols(a, g, scale):
    parts = []
    for h in range(a.shape[1] // HEAD_DIM):
        ah = a[:, h * HEAD_DIM:(h + 1) * HEAD_DIM]
        y = ah * lax.rsqrt(jnp.mean(ah * ah, axis=-1, keepdims=True) + NORM_EPS) * g
        parts.append(y * scale if scale != 1.0 else y)
    return jnp.concatenate(parts, axis=-1)


def _proj_kernel(a_ref, *rest, groups, normed, scale, tm, has_norm, has_scr):
    rest = list(rest)
    gn_ref = rest.pop(0) if has_norm else None
    w_ref, hg_ref = rest.pop(0), rest.pop(0)
    o_refs = [rest.pop(0) for _ in groups]
    hn_s = rest.pop(0) if has_norm else None
    scr = rest.pop(0) if has_scr else None
    if has_norm:
        hn_s[...] = _rms_rows(a_ref[...], gn_ref[...]).astype(BF16)
        lhs_ref = hn_s
    else:
        lhs_ref = a_ref
    jb = 0
    for (nb, d, _), o_ref in zip(groups, o_refs):
        for k in range(nb):
            acc = jnp.dot(lhs_ref[...], w_ref[:, jb * HEAD_BLOCK:(jb + 1) * HEAD_BLOCK],
                          preferred_element_type=F32)
            if normed[jb]:
                acc = _head_norm_cols(acc, hg_ref[jb:jb + 1, :], scale)
            if d == 1:
                o_ref[0, :, k * HEAD_BLOCK:(k + 1) * HEAD_BLOCK] = acc.astype(o_ref.dtype)
            else:
                for h in range(HEAD_BLOCK // LANES):
                    scr[h] = acc[:, h * LANES:(h + 1) * LANES]
                for r in range(d):
                    for h in range(HEAD_BLOCK // LANES):
                        o_ref[r, :, h * LANES:(h + 1) * LANES] = (
                            scr[h, pl.ds(r, tm // d, stride=d), :].astype(o_ref.dtype))
            jb += 1


def _proj(a, norm_gain, w, head_gains, groups, normed, scale, name, tm):
    S, D = a.shape
    N = w.shape[1]
    nblk = N // HEAD_BLOCK
    assert sum(nb for nb, _, _ in groups) == len(normed) == nblk
    assert all(nb == 1 for nb, d, _ in groups if d > 1)
    has_norm = norm_gain is not None
    has_scr = any(d > 1 for _, d, _ in groups)
    kern = functools.partial(_proj_kernel, groups=groups, normed=normed, scale=scale, tm=tm,
                             has_norm=has_norm, has_scr=has_scr)
    in_specs = [pl.BlockSpec((tm, D), lambda i: (i, 0))]
    args = [a]
    if has_norm:
        in_specs.append(pl.BlockSpec((1, D), lambda i: (0, 0)))
        args.append(norm_gain.reshape(1, D))
    in_specs += [pl.BlockSpec((D, N), lambda i: (0, 0), pipeline_mode=pl.Buffered(1)),
                 pl.BlockSpec((nblk, HEAD_DIM), lambda i: (0, 0))]
    args += [w, head_gains]
    scratch = []
    if has_norm:
        scratch.append(pltpu.VMEM((tm, D), BF16))
    if has_scr:
        scratch.append(pltpu.VMEM((HEAD_BLOCK // LANES, tm, LANES), F32))
    return pl.pallas_call(
        kern,
        grid=(S // tm,),
        in_specs=in_specs,
        out_specs=[pl.BlockSpec((d, tm // d, nb * HEAD_BLOCK), lambda i: (0, i, 0)) for nb, d, _ in groups],
        out_shape=[jax.ShapeDtypeStruct((d, S // d, nb * HEAD_BLOCK), dt) for nb, d, dt in groups],
        scratch_shapes=scratch,
        compiler_params=_params(("parallel",)),
        name=name,
    )(*args)


def _mlp_kernel(x_ref, g_ref, w1_ref, w2_ref, *rest, n_casts):
    cast_in, o_ref, cast_out, hn_ref = rest[:n_casts], rest[n_casts], rest[n_casts + 1:-1], rest[-1]

    @pl.when(pl.program_id(1) == 0)
    def _():
        x = x_ref[...]
        hn_ref[...] = _rms_rows(x, g_ref[...]).astype(BF16)
        o_ref[...] = x

    h = jnp.dot(hn_ref[...], w1_ref[...], preferred_element_type=F32)
    h = jnp.maximum(h, 0.0)
    h = (h * h).astype(BF16)
    o_ref[...] += jnp.dot(h, w2_ref[...], preferred_element_type=F32)

    for src, dst in zip(cast_in, cast_out):
        dst[...] = src[...].astype(BF16)


def _mlp(x, g, w1b, w2b, casts, name, tm=1024, tf=512):
    S, D = x.shape
    F = w1b.shape[1]
    ni, nj = S // tm, F // tf
    row_spec = pl.BlockSpec((tm, D), lambda i, j: (i, 0))
    in_specs = [row_spec, pl.BlockSpec((1, D), lambda i, j: (0, 0)),
                pl.BlockSpec((D, tf), lambda i, j: (0, j)),
                pl.BlockSpec((tf, D), lambda i, j: (j, 0))]
    args = [x, g.reshape(1, D), w1b, w2b]
    out_specs = [row_spec]
    out_shape = [jax.ShapeDtypeStruct((S, D), F32)]
    for w, idx in casts:
        R, C = w.shape[-2:]
        r = R // (ni * nj)
        assert r * ni * nj == R and r % 16 == 0, (R, ni, nj)
        if idx is None:
            in_specs.append(pl.BlockSpec((r, C), lambda i, j: (i * nj + j, 0)))
        else:
            in_specs.append(pl.BlockSpec((None, r, C), lambda i, j, idx=idx: (idx, i * nj + j, 0)))
        args.append(w)
        out_specs.append(pl.BlockSpec((r, C), lambda i, j: (i * nj + j, 0)))
        out_shape.append(jax.ShapeDtypeStruct((R, C), BF16))
    kern = functools.partial(_mlp_kernel, n_casts=len(casts))
    outs = pl.pallas_call(
        kern,
        grid=(ni, nj),
        in_specs=in_specs,
        out_specs=out_specs,
        out_shape=out_shape,
        scratch_shapes=[pltpu.VMEM((tm, D), BF16)],
        compiler_params=_params(("parallel", "arbitrary")),
        name=name,
    )(*args)
    return outs[0], list(outs[1:])


def _mem_kv_kernel(mem_ref, g_ref, w_ref, kg_ref, k_ref, v_ref):
    hn = _rms_rows(mem_ref[...], g_ref[...]).astype(BF16)
    kv = jnp.dot(hn, w_ref[...].astype(BF16), preferred_element_type=F32)
    k_ref[...] = _head_norm_cols(kv[:, :MEM_WIDTH], kg_ref[...], 1.0).astype(BF16)
    v_ref[...] = kv[:, MEM_WIDTH:].astype(BF16)


def _mem_kv(mem, mem_norm_g, mem_w_kv, mem_k_norm_g):
    M, D = mem.shape
    out = jax.ShapeDtypeStruct((DEPTH, M, MEM_WIDTH), BF16)
    return pl.pallas_call(
        _mem_kv_kernel,
        grid=(DEPTH,),
        in_specs=[
            pl.BlockSpec((M, D), lambda l: (0, 0)),
            pl.BlockSpec((None, 1, D), lambda l: (l, 0, 0)),
            pl.BlockSpec((None, D, 2 * MEM_WIDTH), lambda l: (l, 0, 0)),
            pl.BlockSpec((None, 1, HEAD_DIM), lambda l: (l, 0, 0)),
        ],
        out_specs=[pl.BlockSpec((None, M, MEM_WIDTH), lambda l: (l, 0, 0)),
                   pl.BlockSpec((None, M, MEM_WIDTH), lambda l: (l, 0, 0))],
        out_shape=[out, out],
        compiler_params=_params(("arbitrary",)),
        name="mem_kv",
    )(mem, mem_norm_g.reshape(DEPTH, 1, D), mem_w_kv, mem_k_norm_g.reshape(DEPTH, 1, HEAD_DIM))


def _mem_attn_kernel(q_ref, k_ref, v_ref, o_ref):
    for h in range(MEM_HEADS):
        hs = slice(h * HEAD_DIM, (h + 1) * HEAD_DIM)
        q = q_ref[:, hs].astype(BF16)
        s = lax.dot_general(q, k_ref[:, hs], (((1,), (1,)), ((), ())), preferred_element_type=F32)
        m = jnp.max(s, axis=-1, keepdims=True)
        p = jnp.exp(s - m)
        den = jnp.sum(p, axis=-1, keepdims=True)
        o = jnp.dot(p.astype(BF16), v_ref[:, hs], preferred_element_type=F32) / den
        o_ref[:, hs] = o.astype(o_ref.dtype)


def _mem_attn(proj, q_col_block, mem_k, mem_v, layer, name, tm=512):
    S = proj.shape[0]
    return pl.pallas_call(
        _mem_attn_kernel,
        grid=(S // tm,),
        in_specs=[
            pl.BlockSpec((tm, MEM_WIDTH), lambda i: (i, q_col_block)),
            pl.BlockSpec((None, MEM_TOKENS, MEM_WIDTH), lambda i: (layer, 0, 0)),
            pl.BlockSpec((None, MEM_TOKENS, MEM_WIDTH), lambda i: (layer, 0, 0)),
        ],
        out_specs=pl.BlockSpec((tm, MEM_WIDTH), lambda i: (i, 0)),
        out_shape=jax.ShapeDtypeStruct((S, MEM_WIDTH), BF16),
        compiler_params=_params(("parallel",)),
        name=name,
    )(proj, mem_k, mem_v)


def _rglru_tile(u_ref, gate_ref, cw_ref, cb_ref, wg_ref, bg_ref, lam_ref, y_ref, us, hs, utail, hcar,
                *, T, LB, pitch, before_block=lambda c: None):
    nseg = SUBLANES
    seg = T // nseg
    lam = lam_ref[...]
    sp = jnp.maximum(-lam, 0.0) + jnp.log1p(jnp.exp(-jnp.abs(lam)))
    first_seg = lax.broadcasted_iota(jnp.int32, (nseg, LANES), 0) == 0

    for c in range(LB // LANES):
        before_block(c)
        cs = slice(c * LANES, (c + 1) * LANES)
        for s in range(nseg):
            us[c, s * pitch:s * pitch + seg, :] = u_ref[s * seg:(s + 1) * seg, cs]
        U = [us[c, pl.ds(p, nseg, stride=pitch), :] for p in range(seg)]

        before = {}
        for k in range(1, CONV_WIDTH):
            cur = pltpu.roll(U[seg - k], 1, axis=0)
            prev = pltpu.roll(utail[k - 1, :, cs], 1, axis=0)
            before[-k] = jnp.where(first_seg, prev, cur)
        for k in range(1, CONV_WIDTH):
            utail[k - 1, :, cs] = U[seg - k]
        tap = lambda p: U[p] if p >= 0 else before[p]

        cwb = [jnp.broadcast_to(cw_ref[k:k + 1, cs], (nseg, LANES)) for k in range(CONV_WIDTH)]
        cbb = jnp.broadcast_to(cb_ref[:, cs], (nseg, LANES))
        xc = jnp.stack([cbb + (tap(p - 3) * cwb[0] + tap(p - 2) * cwb[1] + tap(p - 1) * cwb[2] + U[p] * cwb[3])
                        for p in range(seg)])

        z = jnp.dot(xc.reshape(T, LANES).astype(BF16), wg_ref[c], preferred_element_type=F32) + bg_ref[c]
        r = jax.nn.sigmoid(z[:, :LANES]).reshape(seg, nseg, LANES)
        ig = jax.nn.sigmoid(z[:, LANES:]).reshape(seg, nseg, LANES)
        log_a = (-RG_C * r) * sp[:, cs]
        a = jnp.exp(log_a)
        om = 1.0 - a * a
        root = jnp.where(om > 0.0, om * lax.rsqrt(om), 0.0)
        b = root * (ig * xc)

        hl, pr = [b[0]], [a[0]]
        for p in range(1, seg):
            hl.append(a[p] * hl[-1] + b[p])
            pr.append(a[p] * pr[-1])

        h_in = [hcar[:, cs]]
        for s in range(nseg):
            h_in.append(hl[-1][s:s + 1, :] + pr[-1][s:s + 1, :] * h_in[-1])
        hcar[:, cs] = h_in[nseg]
        h_in = jnp.concatenate(h_in[:nseg], axis=0)

        for p in range(seg):
            hs[c, pl.ds(p, nseg, stride=pitch), :] = hl[p] + pr[p] * h_in

        for s in range(nseg):
            rows = slice(s * seg, (s + 1) * seg)
            gt = gate_ref[rows, cs]
            gelu = 0.5 * gt * (1.0 + jnp.tanh(math.sqrt(2.0 / math.pi) * (gt + 0.044715 * (gt * gt * gt))))
            y_ref[rows, cs] = (hs[c, s * pitch:s * pitch + seg, :] * gelu).astype(y_ref.dtype)


def _a_mix_kernel(x_ref, gn_ref, w_ref, hg_ref, cw_ref, cb_ref, wg_ref, bg_ref, lam_ref, y_ref, mq_ref,
                  hn_s, u_cur, g_cur, u_new, g_new, us, hs, utail, hcar, *, T, pitch):
    i = pl.program_id(0)

    @pl.when(i <= 1)
    def _():
        utail[...] = jnp.zeros_like(utail)
        hcar[...] = jnp.zeros_like(hcar)

    @pl.when(i == 0)
    def _():
        u_cur[...] = jnp.zeros_like(u_cur)
        g_cur[...] = jnp.zeros_like(g_cur)

    hn_s[...] = _rms_rows(x_ref[...], gn_ref[...]).astype(BF16)
    nb = LRU_WIDTH // HEAD_BLOCK

    def project(jb):
        acc = jnp.dot(hn_s[...], w_ref[:, jb * HEAD_BLOCK:(jb + 1) * HEAD_BLOCK], preferred_element_type=F32)
        if jb < nb:
            u_new[:, jb * HEAD_BLOCK:(jb + 1) * HEAD_BLOCK] = acc
        elif jb < 2 * nb:
            g_new[:, (jb - nb) * HEAD_BLOCK:(jb - nb + 1) * HEAD_BLOCK] = acc
        else:
            mq_ref[...] = _head_norm_cols(acc, hg_ref[...], ATTN_SCALE).astype(mq_ref.dtype)

    n_proj = 2 * nb + 1
    lane_blocks = LRU_WIDTH // LANES
    at_block = {(jb * lane_blocks) // n_proj: jb for jb in range(n_proj)}
    assert len(at_block) == n_proj

    def before_block(c):
        if c in at_block:
            project(at_block[c])

    _rglru_tile(u_cur, g_cur, cw_ref, cb_ref, wg_ref, bg_ref, lam_ref, y_ref, us, hs, utail, hcar,
                T=T, LB=LRU_WIDTH, pitch=pitch, before_block=before_block)

    u_cur[...] = u_new[...]
    g_cur[...] = g_new[...]


def _a_mix(x, norm_gain, w_in, mq_gain, conv_w, conv_b, wg, bg, lam, name, T=512):
    S, D = x.shape
    W = LRU_WIDTH
    nt = S // T
    seg = T // SUBLANES
    pitch = seg + SUBLANES if (seg // SUBLANES) % 2 == 0 else seg
    slab = pltpu.VMEM((W // LANES, SUBLANES * pitch, LANES), F32)
    const = lambda shape: pl.BlockSpec(shape, lambda i: (0,) * len(shape))
    cur = lambda i: jnp.minimum(i, nt - 1)
    prv = lambda i: jnp.maximum(i - 1, 0)
    kern = functools.partial(_a_mix_kernel, T=T, pitch=pitch)
    return pl.pallas_call(
        kern,
        grid=(nt + 1,),
        in_specs=[
            pl.BlockSpec((T, D), lambda i: (cur(i), 0)),
            const((1, D)),
            pl.BlockSpec((D, A_IN_WIDTH), lambda i: (0, 0), pipeline_mode=pl.Buffered(1)),
            const((1, HEAD_DIM)),
            const((CONV_WIDTH, W)),
            const((1, W)),
            const((LRU_BLOCKS, LRU_BLOCK_WIDTH, 2 * LRU_BLOCK_WIDTH)),
            const((LRU_BLOCKS, 1, 2 * LRU_BLOCK_WIDTH)),
            const((1, W)),
        ],
        out_specs=[pl.BlockSpec((T, W), lambda i: (prv(i), 0)),
                   pl.BlockSpec((T, MEM_WIDTH), lambda i: (cur(i), 0))],
        out_shape=[jax.ShapeDtypeStruct((S, W), BF16), jax.ShapeDtypeStruct((S, MEM_WIDTH), BF16)],
        scratch_shapes=[pltpu.VMEM((T, D), BF16)] + [pltpu.VMEM((T, W), F32)] * 4 + [
                        slab, slab, pltpu.VMEM((CONV_WIDTH - 1, SUBLANES, W), F32), pltpu.VMEM((1, W), F32)],
        compiler_params=_params(("arbitrary",)),
        name=name,
    )(x, norm_gain.reshape(1, D), w_in, mq_gain, conv_w, conv_b.reshape(1, W), wg, bg, lam.reshape(1, W))


def _cast_weight_once(w_ref, wb_ref):
    @pl.when(pl.program_id(0) == 0)
    def _():
        wb_ref[...] = w_ref[...].astype(BF16)


def _out_a_kernel(x_ref, y_ref, m_ref, w_ref, o_ref, wb_ref):
    _cast_weight_once(w_ref, wb_ref)
    acc = jnp.dot(y_ref[...], wb_ref[0:LRU_WIDTH, :], preferred_element_type=F32)
    acc += jnp.dot(m_ref[...], wb_ref[LRU_WIDTH:LRU_WIDTH + MEM_WIDTH, :], preferred_element_type=F32)
    o_ref[...] = x_ref[...] + acc


def _out_a(x, y, mem_out, w, widx, name, tm=512):
    S, D = x.shape
    K = LRU_WIDTH + MEM_WIDTH
    row = pl.BlockSpec((tm, D), lambda i: (i, 0))
    return pl.pallas_call(
        _out_a_kernel,
        grid=(S // tm,),
        in_specs=[
            row,
            pl.BlockSpec((tm, LRU_WIDTH), lambda i: (i, 0)),
            pl.BlockSpec((tm, MEM_WIDTH), lambda i: (i, 0)),
            pl.BlockSpec((None, K, D), lambda i: (widx, 0, 0), pipeline_mode=pl.Buffered(1)),
        ],
        out_specs=row,
        out_shape=jax.ShapeDtypeStruct((S, D), F32),
        scratch_shapes=[pltpu.VMEM((K, D), BF16)],
        compiler_params=_params(("arbitrary",)),
        name=name,
    )(x, y, mem_out, w)


def _bucket_tables():
    qi = np.arange(Q_BLOCK)[:, None]
    ki = np.arange(2 * Q_BLOCK)[None, :]
    u = qi + Q_BLOCK - ki
    tables, valids = [], []
    for window, dilation in DIL_PATTERNS:
        n = np.maximum(u * dilation, 0)
        nf = np.maximum(n, 1).astype(np.float32)
        large = REL_MAX_EXACT + (np.log(nf / np.float32(REL_MAX_EXACT))
                                 / np.float32(math.log(REL_MAX_DISTANCE / REL_MAX_EXACT))
                                 * np.float32(REL_BUCKETS - REL_MAX_EXACT)).astype(np.int32)
        large = np.minimum(large, REL_BUCKETS - 1)
        tables.append(np.where(n < REL_MAX_EXACT, n, large).astype(np.int32))
        valids.append(((u >= 0) & (u <= window // dilation)).astype(np.int32))
    return np.stack(tables), np.stack(valids)


def _bias_kernel(tab_ref, idx_ref, valid_ref, o_ref):
    g = pl.program_id(0)
    idx = idx_ref[...]
    valid = valid_ref[...] > 0
    for h in range(DIL_HEADS):
        acc = jnp.zeros(idx.shape, F32)
        for b in range(REL_BUCKETS):
            acc = jnp.where(idx == b, tab_ref[b, g * DIL_HEADS + h], acc)
        o_ref[h] = jnp.where(valid, acc, NEG_INF)


def _bias_expand(rel_bias):
    idx, valid = _bucket_tables()
    blk = pl.BlockSpec((None, Q_BLOCK, 2 * Q_BLOCK), lambda g: (g, 0, 0))
    return pl.pallas_call(
        _bias_kernel,
        grid=(DIL_GROUPS,),
        in_specs=[pl.BlockSpec(memory_space=pltpu.SMEM), blk, blk],
        out_specs=pl.BlockSpec((DIL_HEADS, Q_BLOCK, 2 * Q_BLOCK), lambda g: (g, 0, 0)),
        out_shape=jax.ShapeDtypeStruct((DIL_GROUPS * DIL_HEADS, Q_BLOCK, 2 * Q_BLOCK), F32),
        compiler_params=_params(("arbitrary",)),
        name="bias_expand",
    )(rel_bias, jnp.asarray(idx), jnp.asarray(valid))


def _dil_kernel(q_ref, kc_ref, kp_ref, vc_ref, vp_ref, b_ref, o_ref, *, LQ):
    is_first = pl.program_id(1) == 0
    for h in range(DIL_HEADS):
        hs = slice(h * HEAD_DIM, (h + 1) * HEAD_DIM)
        for sb in range(LQ // Q_BLOCK):
            rows = slice(sb * Q_BLOCK, (sb + 1) * Q_BLOCK)
            q = q_ref[rows, hs]
            if sb == 0:
                kk = jnp.concatenate([kp_ref[:, hs], kc_ref[0:Q_BLOCK, hs]], axis=0)
                vv = jnp.concatenate([vp_ref[:, hs], vc_ref[0:Q_BLOCK, hs]], axis=0)
            else:
                kk = kc_ref[(sb - 1) * Q_BLOCK:(sb + 1) * Q_BLOCK, hs]
                vv = vc_ref[(sb - 1) * Q_BLOCK:(sb + 1) * Q_BLOCK, hs]
            s = lax.dot_general(q, kk, (((1,), (1,)), ((), ())), preferred_element_type=F32) + b_ref[h]
            if sb == 0:
                col = lax.broadcasted_iota(jnp.int32, s.shape, 1)
                s = jnp.where(jnp.logical_and(is_first, col < Q_BLOCK), NEG_INF, s)
            m = jnp.max(s, axis=-1, keepdims=True)
            p = jnp.exp(s - m)
            den = jnp.sum(p, axis=-1, keepdims=True)
            o = jnp.dot(p.astype(BF16), vv, preferred_element_type=F32) / den
            lse = m + jnp.log(den)
            o_ref[rows, hs] = o
            o_ref[rows, DIL_OUT_WIDTH + h * HEAD_DIM:DIL_OUT_WIDTH + (h + 1) * HEAD_DIM] = (
                jnp.broadcast_to(lse, (Q_BLOCK, HEAD_DIM)))


def _dil_attn(q, k, v, bias, g, name, LQ=512):
    d, L, _ = q.shape
    assert d == DIL_PATTERNS[g][1]
    ow = 2 * DIL_OUT_WIDTH
    ratio = LQ // Q_BLOCK
    kern = functools.partial(_dil_kernel, LQ=LQ)
    cur = pl.BlockSpec((None, LQ, HEAD_BLOCK), lambda r, n: (r, n, 0))
    prev = pl.BlockSpec((None, Q_BLOCK, HEAD_BLOCK), lambda r, n: (r, jnp.maximum(n * ratio - 1, 0), 0))
    return pl.pallas_call(
        kern,
        grid=(d, L // LQ),
        in_specs=[cur, cur, prev, cur, prev,
                  pl.BlockSpec((DIL_HEADS, Q_BLOCK, 2 * Q_BLOCK), lambda r, n: (g, 0, 0))],
        out_specs=pl.BlockSpec((None, LQ, ow), lambda r, n: (r, n, 0)),
        out_shape=jax.ShapeDtypeStruct((d, L, ow), F32),
        compiler_params=_params(("arbitrary", "arbitrary")),
        name=name,
    )(q, k, k, v, v, bias)


def _out_b_kernel(x_ref, o0_ref, o1_ref, o2_ref, m_ref, w_ref, o_ref, wb_ref, s1_ref, s2_ref, *, tm):
    _cast_weight_once(w_ref, wb_ref)
    nslab = 2 * DIL_HEADS
    for og_ref, s_ref in ((o1_ref, s1_ref), (o2_ref, s2_ref)):
        d = og_ref.shape[0]
        for r in range(d):
            for c in range(nslab):
                s_ref[c, pl.ds(r, tm // d, stride=d), :] = og_ref[r, :, c * LANES:(c + 1) * LANES]
    parts = []
    for h in range(DIL_HEADS):
        hs = slice(h * HEAD_DIM, (h + 1) * HEAD_DIM)
        ls = slice(DIL_OUT_WIDTH + h * HEAD_DIM, DIL_OUT_WIDTH + (h + 1) * HEAD_DIM)
        lses = [o0_ref[0, :, ls], s1_ref[DIL_HEADS + h], s2_ref[DIL_HEADS + h]]
        outs = [o0_ref[0, :, hs], s1_ref[h], s2_ref[h]]
        mx = jnp.maximum(jnp.maximum(lses[0], lses[1]), lses[2])
        es = [jnp.exp(l - mx) for l in lses]
        den = es[0] + es[1] + es[2]
        num = es[0] * outs[0] + es[1] * outs[1] + es[2] * outs[2]
        parts.append((num / den).astype(BF16))
    dil = jnp.concatenate(parts, axis=-1)
    acc = jnp.dot(dil, wb_ref[0:DIL_OUT_WIDTH, :], preferred_element_type=F32)
    acc += jnp.dot(m_ref[...], wb_ref[DIL_OUT_WIDTH:DIL_OUT_WIDTH + MEM_WIDTH, :], preferred_element_type=F32)
    o_ref[...] = x_ref[...] + acc


def _out_b(x, ogs, mem_out, w, widx, name, tm=512):
    S, D = x.shape
    ow = 2 * DIL_OUT_WIDTH
    K = DIL_OUT_WIDTH + MEM_WIDTH
    og_specs = [pl.BlockSpec((og.shape[0], tm // og.shape[0], ow), lambda i: (0, i, 0)) for og in ogs]
    row = pl.BlockSpec((tm, D), lambda i: (i, 0))
    return pl.pallas_call(
        functools.partial(_out_b_kernel, tm=tm),
        grid=(S // tm,),
        in_specs=[
            row,
            *og_specs,
            pl.BlockSpec((tm, MEM_WIDTH), lambda i: (i, 0)),
            pl.BlockSpec((None, K, D), lambda i: (widx, 0, 0), pipeline_mode=pl.Buffered(1)),
        ],
        out_specs=row,
        out_shape=jax.ShapeDtypeStruct((S, D), F32),
        scratch_shapes=[pltpu.VMEM((K, D), BF16), pltpu.VMEM((ow // LANES, tm, LANES), F32),
                        pltpu.VMEM((ow // LANES, tm, LANES), F32)],
        compiler_params=_params(("arbitrary",)),
        name=name,
    )(x, ogs[0], ogs[1], ogs[2], mem_out, w)


def kernel(x, mem, norm_mix_g, norm_mlp_g, mlp_w1, mlp_w2, mem_norm_g, mem_w_kv, mem_q_norm_g, mem_k_norm_g, a_w_in, a_conv_w, a_conv_b, a_gate_r_w, a_gate_r_b, a_gate_i_w, a_gate_i_b, a_lambda, a_w_out, kv_norm_g, kv_w, k_norm_g, rel_bias, b_w_q, b_q_norm_g, b_w_out):
    B, S, D = x.shape
    assert (B, S, D) == (1, SEQ, D_MODEL) and mem.shape == (1, MEM_TOKENS, D_MODEL)
    xs = x[0]
    ones = jnp.ones((1, HEAD_DIM), F32)

    mem_k, mem_v = _mem_kv(mem[0], mem_norm_g, mem_w_kv, mem_k_norm_g)
    bias = _bias_expand(rel_bias)
    w1b, w2b = mlp_w1[0].astype(BF16), mlp_w2[0].astype(BF16)
    w_in_b = a_w_in[0].astype(BF16)
    later = {}

    for layer in range(DEPTH):
        mq_gain = mem_q_norm_g[layer][None, :]
        a_in, a_gain = xs, norm_mix_g[layer]
        if layer < N_A_LAYERS:
            i = layer
            wg = jnp.concatenate([a_gate_r_w[i], a_gate_i_w[i]], axis=-1).astype(BF16)
            bg = jnp.concatenate([a_gate_r_b[i], a_gate_i_b[i]], axis=-1)[:, None, :]
            y, mq = _a_mix(a_in, a_gain, w_in_b, mq_gain, a_conv_w[i], a_conv_b[i], wg, bg,
                           a_lambda[i], f"a{i}_mix")
            mem_out = _mem_attn(mq, 0, mem_k, mem_v, layer, f"a{i}_mem_attn")
            xs = _out_a(xs, y, mem_out, a_w_out, i, f"a{i}_out_proj")
        else:
            j = layer - N_A_LAYERS
            dils = tuple(d for _, d in DIL_PATTERNS)
            dil_groups = tuple((1, d, BF16) for d in dils)
            if j == 0:
                kv_gains = jnp.concatenate([k_norm_g, jnp.tile(ones, (DIL_GROUPS, 1))], axis=0)
                kvs = _proj(xs, kv_norm_g, later["kv_w"], kv_gains, dil_groups + dil_groups,
                            (True,) * DIL_GROUPS + (False,) * DIL_GROUPS, 1.0, "shared_kv", tm=1024)
            q_gains = jnp.concatenate([b_q_norm_g[j], mq_gain], axis=0)
            qs = _proj(a_in, a_gain, later["w_q"], q_gains, dil_groups + ((1, 1, BF16),),
                       (True,) * (DIL_GROUPS + 1), ATTN_SCALE, f"b{j}_q_proj", tm=1024)
            mem_out = _mem_attn(qs[DIL_GROUPS].reshape(S, MEM_WIDTH), 0, mem_k, mem_v, layer, f"b{j}_mem_attn")
            ogs = [_dil_attn(qs[g], kvs[g], kvs[DIL_GROUPS + g], bias, g, f"b{j}_dil{g}")
                   for g in range(DIL_GROUPS)]
            xs = _out_b(xs, ogs, mem_out, b_w_out, j, f"b{j}_out_proj")

        nl = layer + 1
        casts, names = [], []
        if nl < DEPTH:
            casts += [(mlp_w1, nl), (mlp_w2, nl)]
            names += ["w1", "w2"]
            if nl < N_A_LAYERS:
                casts.append((a_w_in, nl))
                names.append("w_in")
            else:
                casts.append((b_w_q, nl - N_A_LAYERS))
                names.append("w_q")
                if nl == N_A_LAYERS:
                    casts.append((kv_w, None))
                    names.append("kv_w")
        xs, cast_out = _mlp(xs, norm_mlp_g[layer], w1b, w2b, casts, f"mlp{layer}")
        later = dict(zip(names, cast_out))
        w1b, w2b, w_in_b = later.get("w1"), later.get("w2"), later.get("w_in")
    return xs[None]
```

```python
import functools
import math

import numpy as np
import jax
import jax.numpy as jnp
from jax import lax
from jax.experimental import pallas as pl
from jax.experimental.pallas import tpu as pltpu

F32 = jnp.float32
BF16 = jnp.bfloat16

D_MODEL = 2048
SEQ = 8192
DEPTH = 4
N_A_LAYERS = DEPTH // 2
HEAD_DIM = 128
D_FF = 4 * D_MODEL
LRU_WIDTH = 3 * D_MODEL // 4
LRU_BLOCK_WIDTH = 128
LRU_BLOCKS = LRU_WIDTH // LRU_BLOCK_WIDTH
CONV_WIDTH = 4
RG_C = 8.0
MEM_TOKENS = 256
MEM_HEADS = 4
MEM_WIDTH = MEM_HEADS * HEAD_DIM
DIL_PATTERNS = ((128, 1), (512, 4), (2048, 16))
DIL_GROUPS = len(DIL_PATTERNS)
DIL_HEADS = 4
DIL_WIDTH = DIL_GROUPS * DIL_HEADS * HEAD_DIM
DIL_OUT_WIDTH = DIL_HEADS * HEAD_DIM
Q_BLOCK = 128
REL_BUCKETS = 32
REL_MAX_EXACT = REL_BUCKETS // 2
REL_MAX_DISTANCE = 2048
A_IN_WIDTH = 2 * LRU_WIDTH + MEM_WIDTH
NORM_EPS = 1e-6
NEG_INF = -1e30
ATTN_SCALE = HEAD_DIM ** -0.5

V7X_VMEM_BYTES = 64 * 1024 * 1024
SCOPED_VMEM_MIB = 60
assert SCOPED_VMEM_MIB * 1024 * 1024 < V7X_VMEM_BYTES
LANES = 128
SUBLANES = 8
HEAD_BLOCK = DIL_HEADS * HEAD_DIM

def _params(semantics):
    return pltpu.CompilerParams(dimension_semantics=semantics,
                                vmem_limit_bytes=SCOPED_VMEM_MIB * 1024 * 1024)


def _rms_rows(x, g):
    ms = jnp.mean(x * x, axis=-1, keepdims=True)
    return x * lax.rsqrt(ms + NORM_EPS) * g


def _head_norm_cols(a, g, scale):
    parts = []
    for h in range(a.shape[1] // HEAD_DIM):
        ah = a[:, h * HEAD_DIM:(h + 1) * HEAD_DIM]
        y = ah * lax.rsqrt(jnp.mean(ah * ah, axis=-1, keepdims=True) + NORM_EPS) * g
        parts.append(y * scale if scale != 1.0 else y)
    return jnp.concatenate(parts, axis=-1)


def _proj_kernel(a_ref, *rest, groups, normed, scale, tm, has_norm, has_scr):
    rest = list(rest)
    gn_ref = rest.pop(0) if has_norm else None
    w_ref, hg_ref = rest.pop(0), rest.pop(0)
    o_refs = [rest.pop(0) for _ in groups]
    hn_s = rest.pop(0) if has_norm else None
    scr = rest.pop(0) if has_scr else None
    if has_norm:
        hn_s[...] = _rms_rows(a_ref[...], gn_ref[...]).astype(BF16)
        lhs_ref = hn_s
    else:
        lhs_ref = a_ref
    jb = 0
    for (nb, d, _), o_ref in zip(groups, o_refs):
        for k in range(nb):
            acc = jnp.dot(lhs_ref[...], w_ref[:, jb * HEAD_BLOCK:(jb + 1) * HEAD_BLOCK],
                          preferred_element_type=F32)
            if normed[jb]:
                acc = _head_norm_cols(acc, hg_ref[jb:jb + 1, :], scale)
            if d == 1:
                o_ref[0, :, k * HEAD_BLOCK:(k + 1) * HEAD_BLOCK] = acc.astype(o_ref.dtype)
            else:
                for h in range(HEAD_BLOCK // LANES):
                    scr[h] = acc[:, h * LANES:(h + 1) * LANES]
                for r in range(d):
                    for h in range(HEAD_BLOCK // LANES):
                        o_ref[r, :, h * LANES:(h + 1) * LANES] = (
                            scr[h, pl.ds(r, tm // d, stride=d), :].astype(o_ref.dtype))
            jb += 1


def _proj(a, norm_gain, w, head_gains, groups, normed, scale, name, tm):
    S, D = a.shape
    N = w.shape[1]
    nblk = N // HEAD_BLOCK
    assert sum(nb for nb, _, _ in groups) == len(normed) == nblk
    assert all(nb == 1 for nb, d, _ in groups if d > 1)
    has_norm = norm_gain is not None
    has_scr = any(d > 1 for _, d, _ in groups)
    kern = functools.partial(_proj_kernel, groups=groups, normed=normed, scale=scale, tm=tm,
                             has_norm=has_norm, has_scr=has_scr)
    in_specs = [pl.BlockSpec((tm, D), lambda i: (i, 0))]
    args = [a]
    if has_norm:
        in_specs.append(pl.BlockSpec((1, D), lambda i: (0, 0)))
        args.append(norm_gain.reshape(1, D))
    in_specs += [pl.BlockSpec((D, N), lambda i: (0, 0), pipeline_mode=pl.Buffered(1)),
                 pl.BlockSpec((nblk, HEAD_DIM), lambda i: (0, 0))]
    args += [w, head_gains]
    scratch = []
    if has_norm:
        scratch.append(pltpu.VMEM((tm, D), BF16))
    if has_scr:
        scratch.append(pltpu.VMEM((HEAD_BLOCK // LANES, tm, LANES), F32))
    return pl.pallas_call(
        kern,
        grid=(S // tm,),
        in_specs=in_specs,
        out_specs=[pl.BlockSpec((d, tm // d, nb * HEAD_BLOCK), lambda i: (0, i, 0)) for nb, d, _ in groups],
        out_shape=[jax.ShapeDtypeStruct((d, S // d, nb * HEAD_BLOCK), dt) for nb, d, dt in groups],
        scratch_shapes=scratch,
        compiler_params=_params(("parallel",)),
        name=name,
    )(*args)


def _mlp_kernel(x_ref, g_ref, w1_ref, w2_ref, *rest, n_casts):
    cast_in, o_ref, cast_out, hn_ref = rest[:n_casts], rest[n_casts], rest[n_casts + 1:-1], rest[-1]

    @pl.when(pl.program_id(1) == 0)
    def _():
        x = x_ref[...]
        hn_ref[...] = _rms_rows(x, g_ref[...]).astype(BF16)
        o_ref[...] = x

    h = jnp.dot(hn_ref[...], w1_ref[...], preferred_element_type=F32)
    h = jnp.maximum(h, 0.0)
    h = (h * h).astype(BF16)
    o_ref[...] += jnp.dot(h, w2_ref[...], preferred_element_type=F32)

    for src, dst in zip(cast_in, cast_out):
        dst[...] = src[...].astype(BF16)


def _mlp(x, g, w1b, w2b, casts, name, tm=1024, tf=512):
    S, D = x.shape
    F = w1b.shape[1]
    ni, nj = S // tm, F // tf
    row_spec = pl.BlockSpec((tm, D), lambda i, j: (i, 0))
    in_specs = [row_spec, pl.BlockSpec((1, D), lambda i, j: (0, 0)),
                pl.BlockSpec((D, tf), lambda i, j: (0, j)),
                pl.BlockSpec((tf, D), lambda i, j: (j, 0))]
    args = [x, g.reshape(1, D), w1b, w2b]
    out_specs = [row_spec]
    out_shape = [jax.ShapeDtypeStruct((S, D), F32)]
    for w, idx in casts:
        R, C = w.shape[-2:]
        r = R // (ni * nj)
        assert r * ni * nj == R and r % 16 == 0, (R, ni, nj)
        if idx is None:
            in_specs.append(pl.BlockSpec((r, C), lambda i, j: (i * nj + j, 0)))
        else:
            in_specs.append(pl.BlockSpec((None, r, C), lambda i, j, idx=idx: (idx, i * nj + j, 0)))
        args.append(w)
        out_specs.append(pl.BlockSpec((r, C), lambda i, j: (i * nj + j, 0)))
        out_shape.append(jax.ShapeDtypeStruct((R, C), BF16))
    kern = functools.partial(_mlp_kernel, n_casts=len(casts))
    outs = pl.pallas_call(
        kern,
        grid=(ni, nj),
        in_specs=in_specs,
        out_specs=out_specs,
        out_shape=out_shape,
        scratch_shapes=[pltpu.VMEM((tm, D), BF16)],
        compiler_params=_params(("parallel", "arbitrary")),
        name=name,
    )(*args)
    return outs[0], list(outs[1:])


def _mem_kv_kernel(mem_ref, g_ref, w_ref, kg_ref, k_ref, v_ref):
    hn = _rms_rows(mem_ref[...], g_ref[...]).astype(BF16)
    kv = jnp.dot(hn, w_ref[...].astype(BF16), preferred_element_type=F32)
    k_ref[...] = _head_norm_cols(kv[:, :MEM_WIDTH], kg_ref[...], 1.0).astype(BF16)
    v_ref[...] = kv[:, MEM_WIDTH:].astype(BF16)


def _mem_kv(mem, mem_norm_g, mem_w_kv, mem_k_norm_g):
    M, D = mem.shape
    out = jax.ShapeDtypeStruct((DEPTH, M, MEM_WIDTH), BF16)
    return pl.pallas_call(
        _mem_kv_kernel,
        grid=(DEPTH,),
        in_specs=[
            pl.BlockSpec((M, D), lambda l: (0, 0)),
            pl.BlockSpec((None, 1, D), lambda l: (l, 0, 0)),
            pl.BlockSpec((None, D, 2 * MEM_WIDTH), lambda l: (l, 0, 0)),
            pl.BlockSpec((None, 1, HEAD_DIM), lambda l: (l, 0, 0)),
        ],
        out_specs=[pl.BlockSpec((None, M, MEM_WIDTH), lambda l: (l, 0, 0)),
                   pl.BlockSpec((None, M, MEM_WIDTH), lambda l: (l, 0, 0))],
        out_shape=[out, out],
        compiler_params=_params(("arbitrary",)),
        name="mem_kv",
    )(mem, mem_norm_g.reshape(DEPTH, 1, D), mem_w_kv, mem_k_norm_g.reshape(DEPTH, 1, HEAD_DIM))


def _mem_attn_rows(q_ref, k_ref, v_ref):
    outs = []
    for h in range(MEM_HEADS):
        hs = slice(h * HEAD_DIM, (h + 1) * HEAD_DIM)
        s = lax.dot_general(q_ref[:, hs], k_ref[:, hs], (((1,), (1,)), ((), ())), preferred_element_type=F32)
        m = jnp.max(s, axis=-1, keepdims=True)
        p = jnp.exp(s - m)
        den = jnp.sum(p, axis=-1, keepdims=True)
        o = jnp.dot(p.astype(BF16), v_ref[:, hs], preferred_element_type=F32) / den
        outs.append(o.astype(BF16))
    return jnp.concatenate(outs, axis=-1)


def _mem_kv_specs(layer):
    spec = pl.BlockSpec((None, MEM_TOKENS, MEM_WIDTH), lambda i: (layer, 0, 0))
    return [spec, spec]


def _rglru_tile(u_ref, gate_ref, cw_ref, cb_ref, wg_ref, bg_ref, lam_ref, y_ref, us, hs, utail, hcar,
                *, T, LB, pitch, before_block=lambda c: None):
    nseg = SUBLANES
    seg = T // nseg
    lam = lam_ref[...]
    sp = jnp.maximum(-lam, 0.0) + jnp.log1p(jnp.exp(-jnp.abs(lam)))
    first_seg = lax.broadcasted_iota(jnp.int32, (nseg, LANES), 0) == 0

    for c in range(LB // LANES):
        before_block(c)
        cs = slice(c * LANES, (c + 1) * LANES)
        for s in range(nseg):
            us[c, s * pitch:s * pitch + seg, :] = u_ref[s * seg:(s + 1) * seg, cs]
        U = [us[c, pl.ds(p, nseg, stride=pitch), :] for p in range(seg)]

        before = {}
        for k in range(1, CONV_WIDTH):
            cur = pltpu.roll(U[seg - k], 1, axis=0)
            prev = pltpu.roll(utail[k - 1, :, cs], 1, axis=0)
            before[-k] = jnp.where(first_seg, prev, cur)
        for k in range(1, CONV_WIDTH):
            utail[k - 1, :, cs] = U[seg - k]
        tap = lambda p: U[p] if p >= 0 else before[p]

        cwb = [jnp.broadcast_to(cw_ref[k:k + 1, cs], (nseg, LANES)) for k in range(CONV_WIDTH)]
        cbb = jnp.broadcast_to(cb_ref[:, cs], (nseg, LANES))
        xc = jnp.stack([cbb + (tap(p - 3) * cwb[0] + tap(p - 2) * cwb[1] + tap(p - 1) * cwb[2] + U[p] * cwb[3])
                        for p in range(seg)])

        z = jnp.dot(xc.reshape(T, LANES).astype(BF16), wg_ref[c], preferred_element_type=F32) + bg_ref[c]
        r = jax.nn.sigmoid(z[:, :LANES]).reshape(seg, nseg, LANES)
        ig = jax.nn.sigmoid(z[:, LANES:]).reshape(seg, nseg, LANES)
        log_a = (-RG_C * r) * sp[:, cs]
        a = jnp.exp(log_a)
        om = 1.0 - a * a
        root = jnp.where(om > 0.0, om * lax.rsqrt(om), 0.0)
        b = root * (ig * xc)

        hl, pr = [b[0]], [a[0]]
        for p in range(1, seg):
            hl.append(a[p] * hl[-1] + b[p])
            pr.append(a[p] * pr[-1])

        h_in = [hcar[:, cs]]
        for s in range(nseg):
            h_in.append(hl[-1][s:s + 1, :] + pr[-1][s:s + 1, :] * h_in[-1])
        hcar[:, cs] = h_in[nseg]
        h_in = jnp.concatenate(h_in[:nseg], axis=0)

        for p in range(seg):
            hs[c, pl.ds(p, nseg, stride=pitch), :] = hl[p] + pr[p] * h_in

        for s in range(nseg):
            rows = slice(s * seg, (s + 1) * seg)
            gt = gate_ref[rows, cs]
            gelu = 0.5 * gt * (1.0 + jnp.tanh(math.sqrt(2.0 / math.pi) * (gt + 0.044715 * (gt * gt * gt))))
            y_ref[rows, cs] = (hs[c, s * pitch:s * pitch + seg, :] * gelu).astype(y_ref.dtype)


def _a_mix_kernel(x_ref, gn_ref, w_ref, hg_ref, cw_ref, cb_ref, wg_ref, bg_ref, lam_ref, y_ref, mq_ref,
                  hn_s, u_cur, g_cur, u_new, g_new, us, hs, utail, hcar, *, T, pitch):
    i = pl.program_id(0)

    @pl.when(i <= 1)
    def _():
        utail[...] = jnp.zeros_like(utail)
        hcar[...] = jnp.zeros_like(hcar)

    @pl.when(i == 0)
    def _():
        u_cur[...] = jnp.zeros_like(u_cur)
        g_cur[...] = jnp.zeros_like(g_cur)

    hn_s[...] = _rms_rows(x_ref[...], gn_ref[...]).astype(BF16)
    nb = LRU_WIDTH // HEAD_BLOCK

    def project(jb):
        acc = jnp.dot(hn_s[...], w_ref[:, jb * HEAD_BLOCK:(jb + 1) * HEAD_BLOCK], preferred_element_type=F32)
        if jb < nb:
            u_new[:, jb * HEAD_BLOCK:(jb + 1) * HEAD_BLOCK] = acc
        elif jb < 2 * nb:
            g_new[:, (jb - nb) * HEAD_BLOCK:(jb - nb + 1) * HEAD_BLOCK] = acc
        else:
            mq_ref[...] = _head_norm_cols(acc, hg_ref[...], ATTN_SCALE).astype(mq_ref.dtype)

    n_proj = 2 * nb + 1
    lane_blocks = LRU_WIDTH // LANES
    at_block = {(jb * lane_blocks) // n_proj: jb for jb in range(n_proj)}
    assert len(at_block) == n_proj

    def before_block(c):
        if c in at_block:
            project(at_block[c])

    _rglru_tile(u_cur, g_cur, cw_ref, cb_ref, wg_ref, bg_ref, lam_ref, y_ref, us, hs, utail, hcar,
                T=T, LB=LRU_WIDTH, pitch=pitch, before_block=before_block)

    u_cur[...] = u_new[...]
    g_cur[...] = g_new[...]


def _a_mix(x, norm_gain, w_in, mq_gain, conv_w, conv_b, wg, bg, lam, name, T=512):
    S, D = x.shape
    W = LRU_WIDTH
    nt = S // T
    seg = T // SUBLANES
    pitch = seg + SUBLANES if (seg // SUBLANES) % 2 == 0 else seg
    slab = pltpu.VMEM((W // LANES, SUBLANES * pitch, LANES), F32)
    const = lambda shape: pl.BlockSpec(shape, lambda i: (0,) * len(shape))
    cur = lambda i: jnp.minimum(i, nt - 1)
    prv = lambda i: jnp.maximum(i - 1, 0)
    kern = functools.partial(_a_mix_kernel, T=T, pitch=pitch)
    return pl.pallas_call(
        kern,
        grid=(nt + 1,),
        in_specs=[
            pl.BlockSpec((T, D), lambda i: (cur(i), 0)),
            const((1, D)),
            pl.BlockSpec((D, A_IN_WIDTH), lambda i: (0, 0), pipeline_mode=pl.Buffered(1)),
            const((1, HEAD_DIM)),
            const((CONV_WIDTH, W)),
            const((1, W)),
            const((LRU_BLOCKS, LRU_BLOCK_WIDTH, 2 * LRU_BLOCK_WIDTH)),
            const((LRU_BLOCKS, 1, 2 * LRU_BLOCK_WIDTH)),
            const((1, W)),
        ],
        out_specs=[pl.BlockSpec((T, W), lambda i: (prv(i), 0)),
                   pl.BlockSpec((T, MEM_WIDTH), lambda i: (cur(i), 0))],
        out_shape=[jax.ShapeDtypeStruct((S, W), BF16), jax.ShapeDtypeStruct((S, MEM_WIDTH), BF16)],
        scratch_shapes=[pltpu.VMEM((T, D), BF16)] + [pltpu.VMEM((T, W), F32)] * 4 + [
                        slab, slab, pltpu.VMEM((CONV_WIDTH - 1, SUBLANES, W), F32), pltpu.VMEM((1, W), F32)],
        compiler_params=_params(("arbitrary",)),
        name=name,
    )(x, norm_gain.reshape(1, D), w_in, mq_gain, conv_w, conv_b.reshape(1, W), wg, bg, lam.reshape(1, W))


def _cast_weight_once(w_ref, wb_ref):
    @pl.when(pl.program_id(0) == 0)
    def _():
        wb_ref[...] = w_ref[...].astype(BF16)


def _out_a_kernel(x_ref, y_ref, mq_ref, mk_ref, mv_ref, w_ref, o_ref, wb_ref):
    _cast_weight_once(w_ref, wb_ref)
    acc = jnp.dot(y_ref[...], wb_ref[0:LRU_WIDTH, :], preferred_element_type=F32)
    mem_out = _mem_attn_rows(mq_ref, mk_ref, mv_ref)
    acc += jnp.dot(mem_out, wb_ref[LRU_WIDTH:LRU_WIDTH + MEM_WIDTH, :], preferred_element_type=F32)
    o_ref[...] = x_ref[...] + acc


def _out_a(x, y, mq, mem_k, mem_v, layer, w, widx, name, tm=512):
    S, D = x.shape
    K = LRU_WIDTH + MEM_WIDTH
    row = pl.BlockSpec((tm, D), lambda i: (i, 0))
    return pl.pallas_call(
        _out_a_kernel,
        grid=(S // tm,),
        in_specs=[
            row,
            pl.BlockSpec((tm, LRU_WIDTH), lambda i: (i, 0)),
            pl.BlockSpec((tm, MEM_WIDTH), lambda i: (i, 0)),
            *_mem_kv_specs(layer),
            pl.BlockSpec((None, K, D), lambda i: (widx, 0, 0), pipeline_mode=pl.Buffered(1)),
        ],
        out_specs=row,
        out_shape=jax.ShapeDtypeStruct((S, D), F32),
        scratch_shapes=[pltpu.VMEM((K, D), BF16)],
        compiler_params=_params(("arbitrary",)),
        name=name,
    )(x, y, mq, mem_k, mem_v, w)


def _bucket_tables():
    qi = np.arange(Q_BLOCK)[:, None]
    ki = np.arange(2 * Q_BLOCK)[None, :]
    u = qi + Q_BLOCK - ki
    tables, valids = [], []
    for window, dilation in DIL_PATTERNS:
        n = np.maximum(u * dilation, 0)
        nf = np.maximum(n, 1).astype(np.float32)
        large = REL_MAX_EXACT + (np.log(nf / np.float32(REL_MAX_EXACT))
                                 / np.float32(math.log(REL_MAX_DISTANCE / REL_MAX_EXACT))
                                 * np.float32(REL_BUCKETS - REL_MAX_EXACT)).astype(np.int32)
        large = np.minimum(large, REL_BUCKETS - 1)
        tables.append(np.where(n < REL_MAX_EXACT, n, large).astype(np.int32))
        valids.append(((u >= 0) & (u <= window // dilation)).astype(np.int32))
    return np.stack(tables), np.stack(valids)


def _bias_kernel(tab_ref, idx_ref, valid_ref, o_ref):
    g = pl.program_id(0)
    idx = idx_ref[...]
    valid = valid_ref[...] > 0
    for h in range(DIL_HEADS):
        acc = jnp.zeros(idx.shape, F32)
        for b in range(REL_BUCKETS):
            acc = jnp.where(idx == b, tab_ref[b, g * DIL_HEADS + h], acc)
        o_ref[h] = jnp.where(valid, acc, NEG_INF)


def _bias_expand(rel_bias):
    idx, valid = _bucket_tables()
    blk = pl.BlockSpec((None, Q_BLOCK, 2 * Q_BLOCK), lambda g: (g, 0, 0))
    return pl.pallas_call(
        _bias_kernel,
        grid=(DIL_GROUPS,),
        in_specs=[pl.BlockSpec(memory_space=pltpu.SMEM), blk, blk],
        out_specs=pl.BlockSpec((DIL_HEADS, Q_BLOCK, 2 * Q_BLOCK), lambda g: (g, 0, 0)),
        out_shape=jax.ShapeDtypeStruct((DIL_GROUPS * DIL_HEADS, Q_BLOCK, 2 * Q_BLOCK), F32),
        compiler_params=_params(("arbitrary",)),
        name="bias_expand",
    )(rel_bias, jnp.asarray(idx), jnp.asarray(valid))


def _dil_kernel(q_ref, kc_ref, kp_ref, vc_ref, vp_ref, b_ref, o_ref, *, LQ):
    is_first = pl.program_id(1) == 0
    for h in range(DIL_HEADS):
        hs = slice(h * HEAD_DIM, (h + 1) * HEAD_DIM)
        for sb in range(LQ // Q_BLOCK):
            rows = slice(sb * Q_BLOCK, (sb + 1) * Q_BLOCK)
            q = q_ref[rows, hs]
            if sb == 0:
                kk = jnp.concatenate([kp_ref[:, hs], kc_ref[0:Q_BLOCK, hs]], axis=0)
                vv = jnp.concatenate([vp_ref[:, hs], vc_ref[0:Q_BLOCK, hs]], axis=0)
            else:
                kk = kc_ref[(sb - 1) * Q_BLOCK:(sb + 1) * Q_BLOCK, hs]
                vv = vc_ref[(sb - 1) * Q_BLOCK:(sb + 1) * Q_BLOCK, hs]
            s = lax.dot_general(q, kk, (((1,), (1,)), ((), ())), preferred_element_type=F32) + b_ref[h]
            if sb == 0:
                col = lax.broadcasted_iota(jnp.int32, s.shape, 1)
                s = jnp.where(jnp.logical_and(is_first, col < Q_BLOCK), NEG_INF, s)
            m = jnp.max(s, axis=-1, keepdims=True)
            p = jnp.exp(s - m)
            den = jnp.sum(p, axis=-1, keepdims=True)
            o = jnp.dot(p.astype(BF16), vv, preferred_element_type=F32) / den
            lse = m + jnp.log(den)
            o_ref[rows, hs] = o
            o_ref[rows, DIL_OUT_WIDTH + h * HEAD_DIM:DIL_OUT_WIDTH + (h + 1) * HEAD_DIM] = (
                jnp.broadcast_to(lse, (Q_BLOCK, HEAD_DIM)))


def _dil_attn(q, k, v, bias, g, name, LQ=512):
    d, L, _ = q.shape
    assert d == DIL_PATTERNS[g][1]
    ow = 2 * DIL_OUT_WIDTH
    ratio = LQ // Q_BLOCK
    kern = functools.partial(_dil_kernel, LQ=LQ)
    cur = pl.BlockSpec((None, LQ, HEAD_BLOCK), lambda r, n: (r, n, 0))
    prev = pl.BlockSpec((None, Q_BLOCK, HEAD_BLOCK), lambda r, n: (r, jnp.maximum(n * ratio - 1, 0), 0))
    return pl.pallas_call(
        kern,
        grid=(d, L // LQ),
        in_specs=[cur, cur, prev, cur, prev,
                  pl.BlockSpec((DIL_HEADS, Q_BLOCK, 2 * Q_BLOCK), lambda r, n: (g, 0, 0))],
        out_specs=pl.BlockSpec((None, LQ, ow), lambda r, n: (r, n, 0)),
        out_shape=jax.ShapeDtypeStruct((d, L, ow), F32),
        compiler_params=_params(("arbitrary", "arbitrary")),
        name=name,
    )(q, k, k, v, v, bias)


def _out_b_kernel(x_ref, o0_ref, o1_ref, o2_ref, mq_ref, mk_ref, mv_ref, w_ref, o_ref, wb_ref, s1_ref, s2_ref,
                  *, tm):
    _cast_weight_once(w_ref, wb_ref)
    mem_out = _mem_attn_rows(mq_ref, mk_ref, mv_ref)
    acc = jnp.dot(mem_out, wb_ref[DIL_OUT_WIDTH:DIL_OUT_WIDTH + MEM_WIDTH, :], preferred_element_type=F32)
    nslab = 2 * DIL_HEADS
    for og_ref, s_ref in ((o1_ref, s1_ref), (o2_ref, s2_ref)):
        d = og_ref.shape[0]
        for r in range(d):
            for c in range(nslab):
                s_ref[c, pl.ds(r, tm // d, stride=d), :] = og_ref[r, :, c * LANES:(c + 1) * LANES]
    parts = []
    for h in range(DIL_HEADS):
        hs = slice(h * HEAD_DIM, (h + 1) * HEAD_DIM)
        ls = slice(DIL_OUT_WIDTH + h * HEAD_DIM, DIL_OUT_WIDTH + (h + 1) * HEAD_DIM)
        lses = [o0_ref[0, :, ls], s1_ref[DIL_HEADS + h], s2_ref[DIL_HEADS + h]]
        outs = [o0_ref[0, :, hs], s1_ref[h], s2_ref[h]]
        mx = jnp.maximum(jnp.maximum(lses[0], lses[1]), lses[2])
        es = [jnp.exp(l - mx) for l in lses]
        den = es[0] + es[1] + es[2]
        num = es[0] * outs[0] + es[1] * outs[1] + es[2] * outs[2]
        parts.append((num / den).astype(BF16))
    dil = jnp.concatenate(parts, axis=-1)
    acc += jnp.dot(dil, wb_ref[0:DIL_OUT_WIDTH, :], preferred_element_type=F32)
    o_ref[...] = x_ref[...] + acc


def _out_b(x, ogs, mq, mem_k, mem_v, layer, w, widx, name, tm=512):
    S, D = x.shape
    ow = 2 * DIL_OUT_WIDTH
    K = DIL_OUT_WIDTH + MEM_WIDTH
    og_specs = [pl.BlockSpec((og.shape[0], tm // og.shape[0], ow), lambda i: (0, i, 0)) for og in ogs]
    row = pl.BlockSpec((tm, D), lambda i: (i, 0))
    return pl.pallas_call(
        functools.partial(_out_b_kernel, tm=tm),
        grid=(S // tm,),
        in_specs=[
            row,
            *og_specs,
            pl.BlockSpec((tm, MEM_WIDTH), lambda i: (i, 0)),
            *_mem_kv_specs(layer),
            pl.BlockSpec((None, K, D), lambda i: (widx, 0, 0), pipeline_mode=pl.Buffered(1)),
        ],
        out_specs=row,
        out_shape=jax.ShapeDtypeStruct((S, D), F32),
        scratch_shapes=[pltpu.VMEM((K, D), BF16), pltpu.VMEM((ow // LANES, tm, LANES), F32),
                        pltpu.VMEM((ow // LANES, tm, LANES), F32)],
        compiler_params=_params(("arbitrary",)),
        name=name,
    )(x, ogs[0], ogs[1], ogs[2], mq, mem_k, mem_v, w)


def kernel(x, mem, norm_mix_g, norm_mlp_g, mlp_w1, mlp_w2, mem_norm_g, mem_w_kv, mem_q_norm_g, mem_k_norm_g, a_w_in, a_conv_w, a_conv_b, a_gate_r_w, a_gate_r_b, a_gate_i_w, a_gate_i_b, a_lambda, a_w_out, kv_norm_g, kv_w, k_norm_g, rel_bias, b_w_q, b_q_norm_g, b_w_out):
    B, S, D = x.shape
    assert (B, S, D) == (1, SEQ, D_MODEL) and mem.shape == (1, MEM_TOKENS, D_MODEL)
    xs = x[0]
    ones = jnp.ones((1, HEAD_DIM), F32)

    mem_k, mem_v = _mem_kv(mem[0], mem_norm_g, mem_w_kv, mem_k_norm_g)
    bias = _bias_expand(rel_bias)
    w1b, w2b = mlp_w1[0].astype(BF16), mlp_w2[0].astype(BF16)
    w_in_b = a_w_in[0].astype(BF16)
    later = {}

    for layer in range(DEPTH):
        mq_gain = mem_q_norm_g[layer][None, :]
        a_in, a_gain = xs, norm_mix_g[layer]
        if layer < N_A_LAYERS:
            i = layer
            wg = jnp.concatenate([a_gate_r_w[i], a_gate_i_w[i]], axis=-1).astype(BF16)
            bg = jnp.concatenate([a_gate_r_b[i], a_gate_i_b[i]], axis=-1)[:, None, :]
            y, mq = _a_mix(a_in, a_gain, w_in_b, mq_gain, a_conv_w[i], a_conv_b[i], wg, bg,
                           a_lambda[i], f"a{i}_mix")
            xs = _out_a(xs, y, mq, mem_k, mem_v, layer, a_w_out, i, f"a{i}_out_proj")
        else:
            j = layer - N_A_LAYERS
            dils = tuple(d for _, d in DIL_PATTERNS)
            dil_groups = tuple((1, d, BF16) for d in dils)
            if j == 0:
                kv_gains = jnp.concatenate([k_norm_g, jnp.tile(ones, (DIL_GROUPS, 1))], axis=0)
                kvs = _proj(xs, kv_norm_g, later["kv_w"], kv_gains, dil_groups + dil_groups,
                            (True,) * DIL_GROUPS + (False,) * DIL_GROUPS, 1.0, "shared_kv", tm=1024)
            q_gains = jnp.concatenate([b_q_norm_g[j], mq_gain], axis=0)
            qs = _proj(a_in, a_gain, later["w_q"], q_gains, dil_groups + ((1, 1, BF16),),
                       (True,) * (DIL_GROUPS + 1), ATTN_SCALE, f"b{j}_q_proj", tm=1024)
            ogs = [_dil_attn(qs[g], kvs[g], kvs[DIL_GROUPS + g], bias, g, f"b{j}_dil{g}")
                   for g in range(DIL_GROUPS)]
            xs = _out_b(xs, ogs, qs[DIL_GROUPS].reshape(S, MEM_WIDTH), mem_k, mem_v, layer, b_w_out, j,
                        f"b{j}_out_proj")

        nl = layer + 1
        casts, names = [], []
        if nl < DEPTH:
            casts += [(mlp_w1, nl), (mlp_w2, nl)]
            names += ["w1", "w2"]
            if nl < N_A_LAYERS:
                casts.append((a_w_in, nl))
                names.append("w_in")
            else:
                casts.append((b_w_q, nl - N_A_LAYERS))
                names.append("w_q")
                if nl == N_A_LAYERS:
                    casts.append((kv_w, None))
                    names.append("kv_w")
        xs, cast_out = _mlp(xs, norm_mlp_g[layer], w1b, w2b, casts, f"mlp{layer}")
        later = dict(zip(names, cast_out))
        w1b, w2b, w_in_b = later.get("w1"), later.get("w2"), later.get("w_in")
    return xs[None]
```

```python
import functools
import math

import numpy as np
import jax
import jax.numpy as jnp
from jax import lax
from jax.experimental import pallas as pl
from jax.experimental.pallas import tpu as pltpu

F32 = jnp.float32
BF16 = jnp.bfloat16

D_MODEL = 2048
SEQ = 8192
DEPTH = 4
N_A_LAYERS = DEPTH // 2
HEAD_DIM = 128
D_FF = 4 * D_MODEL
LRU_WIDTH = 3 * D_MODEL // 4
LRU_BLOCK_WIDTH = 128
LRU_BLOCKS = LRU_WIDTH // LRU_BLOCK_WIDTH
CONV_WIDTH = 4
RG_C = 8.0
MEM_TOKENS = 256
MEM_HEADS = 4
MEM_WIDTH = MEM_HEADS * HEAD_DIM
DIL_PATTERNS = ((128, 1), (512, 4), (2048, 16))
DIL_GROUPS = len(DIL_PATTERNS)
DIL_HEADS = 4
DIL_WIDTH = DIL_GROUPS * DIL_HEADS * HEAD_DIM
DIL_OUT_WIDTH = DIL_HEADS * HEAD_DIM
Q_BLOCK = 128
REL_BUCKETS = 32
REL_MAX_EXACT = REL_BUCKETS // 2
REL_MAX_DISTANCE = 2048
A_IN_WIDTH = 2 * LRU_WIDTH + MEM_WIDTH
NORM_EPS = 1e-6
NEG_INF = -1e30
ATTN_SCALE = HEAD_DIM ** -0.5

V7X_VMEM_BYTES = 64 * 1024 * 1024
SCOPED_VMEM_MIB = 60
assert SCOPED_VMEM_MIB * 1024 * 1024 < V7X_VMEM_BYTES
LANES = 128
SUBLANES = 8
HEAD_BLOCK = DIL_HEADS * HEAD_DIM

def _params(semantics):
    return pltpu.CompilerParams(dimension_semantics=semantics,
                                vmem_limit_bytes=SCOPED_VMEM_MIB * 1024 * 1024)


def _rms_rows(x, g):
    ms = jnp.mean(x * x, axis=-1, keepdims=True)
    return x * lax.rsqrt(ms + NORM_EPS) * g


def _head_norm_cols(a, g, scale):
    parts = []
    for h in range(a.shape[1] // HEAD_DIM):
        ah = a[:, h * HEAD_DIM:(h + 1) * HEAD_DIM]
        y = ah * lax.rsqrt(jnp.mean(ah * ah, axis=-1, keepdims=True) + NORM_EPS) * g
        parts.append(y * scale if scale != 1.0 else y)
    return jnp.concatenate(parts, axis=-1)


def _proj_kernel(a_ref, *rest, groups, normed, scale, tm, has_norm, has_scr):
    rest = list(rest)
    gn_ref = rest.pop(0) if has_norm else None
    w_ref, hg_ref = rest.pop(0), rest.pop(0)
    o_refs = [rest.pop(0) for _ in groups]
    hn_s = rest.pop(0) if has_norm else None
    scr = rest.pop(0) if has_scr else None
    if has_norm:
        hn_s[...] = _rms_rows(a_ref[...], gn_ref[...]).astype(BF16)
        lhs_ref = hn_s
    else:
        lhs_ref = a_ref
    jb = 0
    for (nb, d, _), o_ref in zip(groups, o_refs):
        for k in range(nb):
            acc = jnp.dot(lhs_ref[...], w_ref[:, jb * HEAD_BLOCK:(jb + 1) * HEAD_BLOCK],
                          preferred_element_type=F32)
            if normed[jb]:
                acc = _head_norm_cols(acc, hg_ref[jb:jb + 1, :], scale)
            if d == 1:
                o_ref[0, :, k * HEAD_BLOCK:(k + 1) * HEAD_BLOCK] = acc.astype(o_ref.dtype)
            else:
                for h in range(HEAD_BLOCK // LANES):
                    scr[h] = acc[:, h * LANES:(h + 1) * LANES]
                for r in range(d):
                    for h in range(HEAD_BLOCK // LANES):
                        o_ref[r, :, h * LANES:(h + 1) * LANES] = (
                            scr[h, pl.ds(r, tm // d, stride=d), :].astype(o_ref.dtype))
            jb += 1


def _proj(a, norm_gain, w, head_gains, groups, normed, scale, name, tm):
    S, D = a.shape
    N = w.shape[1]
    nblk = N // HEAD_BLOCK
    assert sum(nb for nb, _, _ in groups) == len(normed) == nblk
    assert all(nb == 1 for nb, d, _ in groups if d > 1)
    has_norm = norm_gain is not None
    has_scr = any(d > 1 for _, d, _ in groups)
    kern = functools.partial(_proj_kernel, groups=groups, normed=normed, scale=scale, tm=tm,
                             has_norm=has_norm, has_scr=has_scr)
    in_specs = [pl.BlockSpec((tm, D), lambda i: (i, 0))]
    args = [a]
    if has_norm:
        in_specs.append(pl.BlockSpec((1, D), lambda i: (0, 0)))
        args.append(norm_gain.reshape(1, D))
    in_specs += [pl.BlockSpec((D, N), lambda i: (0, 0), pipeline_mode=pl.Buffered(1)),
                 pl.BlockSpec((nblk, HEAD_DIM), lambda i: (0, 0))]
    args += [w, head_gains]
    scratch = []
    if has_norm:
        scratch.append(pltpu.VMEM((tm, D), BF16))
    if has_scr:
        scratch.append(pltpu.VMEM((HEAD_BLOCK // LANES, tm, LANES), F32))
    return pl.pallas_call(
        kern,
        grid=(S // tm,),
        in_specs=in_specs,
        out_specs=[pl.BlockSpec((d, tm // d, nb * HEAD_BLOCK), lambda i: (0, i, 0)) for nb, d, _ in groups],
        out_shape=[jax.ShapeDtypeStruct((d, S // d, nb * HEAD_BLOCK), dt) for nb, d, dt in groups],
        scratch_shapes=scratch,
        compiler_params=_params(("parallel",)),
        name=name,
    )(*args)


def _mlp_kernel(x_ref, g_ref, w1_ref, w2_ref, *rest, n_casts):
    cast_in, o_ref, cast_out, hn_ref = rest[:n_casts], rest[n_casts], rest[n_casts + 1:-1], rest[-1]

    @pl.when(pl.program_id(1) == 0)
    def _():
        x = x_ref[...]
        hn_ref[...] = _rms_rows(x, g_ref[...]).astype(BF16)
        o_ref[...] = x

    h = jnp.dot(hn_ref[...], w1_ref[...], preferred_element_type=F32)
    h = jnp.maximum(h, 0.0)
    h = (h * h).astype(BF16)
    o_ref[...] += jnp.dot(h, w2_ref[...], preferred_element_type=F32)

    for src, dst in zip(cast_in, cast_out):
        dst[...] = src[...].astype(BF16)


def _mlp(x, g, w1b, w2b, casts, name, tm=1024, tf=512):
    S, D = x.shape
    F = w1b.shape[1]
    ni, nj = S // tm, F // tf
    row_spec = pl.BlockSpec((tm, D), lambda i, j: (i, 0))
    in_specs = [row_spec, pl.BlockSpec((1, D), lambda i, j: (0, 0)),
                pl.BlockSpec((D, tf), lambda i, j: (0, j)),
                pl.BlockSpec((tf, D), lambda i, j: (j, 0))]
    args = [x, g.reshape(1, D), w1b, w2b]
    out_specs = [row_spec]
    out_shape = [jax.ShapeDtypeStruct((S, D), F32)]
    for w, idx in casts:
        R, C = w.shape[-2:]
        r = R // (ni * nj)
        assert r * ni * nj == R and r % 16 == 0, (R, ni, nj)
        if idx is None:
            in_specs.append(pl.BlockSpec((r, C), lambda i, j: (i * nj + j, 0)))
        else:
            in_specs.append(pl.BlockSpec((None, r, C), lambda i, j, idx=idx: (idx, i * nj + j, 0)))
        args.append(w)
        out_specs.append(pl.BlockSpec((r, C), lambda i, j: (i * nj + j, 0)))
        out_shape.append(jax.ShapeDtypeStruct((R, C), BF16))
    kern = functools.partial(_mlp_kernel, n_casts=len(casts))
    outs = pl.pallas_call(
        kern,
        grid=(ni, nj),
        in_specs=in_specs,
        out_specs=out_specs,
        out_shape=out_shape,
        scratch_shapes=[pltpu.VMEM((tm, D), BF16)],
        compiler_params=_params(("parallel", "arbitrary")),
        name=name,
    )(*args)
    return outs[0], list(outs[1:])


def _mem_kv_kernel(mem_ref, g_ref, w_ref, kg_ref, k_ref, v_ref):
    hn = _rms_rows(mem_ref[...], g_ref[...]).astype(BF16)
    kv = jnp.dot(hn, w_ref[...].astype(BF16), preferred_element_type=F32)
    k_ref[...] = _head_norm_cols(kv[:, :MEM_WIDTH], kg_ref[...], 1.0).astype(BF16)
    v_ref[...] = kv[:, MEM_WIDTH:].astype(BF16)


def _mem_kv(mem, mem_norm_g, mem_w_kv, mem_k_norm_g):
    M, D = mem.shape
    out = jax.ShapeDtypeStruct((DEPTH, M, MEM_WIDTH), BF16)
    return pl.pallas_call(
        _mem_kv_kernel,
        grid=(DEPTH,),
        in_specs=[
            pl.BlockSpec((M, D), lambda l: (0, 0)),
            pl.BlockSpec((None, 1, D), lambda l: (l, 0, 0)),
            pl.BlockSpec((None, D, 2 * MEM_WIDTH), lambda l: (l, 0, 0)),
            pl.BlockSpec((None, 1, HEAD_DIM), lambda l: (l, 0, 0)),
        ],
        out_specs=[pl.BlockSpec((None, M, MEM_WIDTH), lambda l: (l, 0, 0)),
                   pl.BlockSpec((None, M, MEM_WIDTH), lambda l: (l, 0, 0))],
        out_shape=[out, out],
        compiler_params=_params(("arbitrary",)),
        name="mem_kv",
    )(mem, mem_norm_g.reshape(DEPTH, 1, D), mem_w_kv, mem_k_norm_g.reshape(DEPTH, 1, HEAD_DIM))


def _mem_attn_rows(q_ref, k_ref, v_ref):
    outs = []
    for h in range(MEM_HEADS):
        hs = slice(h * HEAD_DIM, (h + 1) * HEAD_DIM)
        s = lax.dot_general(q_ref[:, hs], k_ref[:, hs], (((1,), (1,)), ((), ())), preferred_element_type=F32)
        m = jnp.max(s, axis=-1, keepdims=True)
        p = jnp.exp(s - m)
        den = jnp.sum(p, axis=-1, keepdims=True)
        o = jnp.dot(p.astype(BF16), v_ref[:, hs], preferred_element_type=F32) / den
        outs.append(o.astype(BF16))
    return jnp.concatenate(outs, axis=-1)


def _mem_kv_specs(layer):
    spec = pl.BlockSpec((None, MEM_TOKENS, MEM_WIDTH), lambda i: (layer, 0, 0))
    return [spec, spec]


def _rglru_tile(u_ref, gate_ref, cw_ref, cb_ref, wg_ref, bg_ref, lam_ref, y_ref, us, hs, utail, hcar,
                *, T, LB, pitch, before_block=lambda c: None):
    nseg = SUBLANES
    seg = T // nseg
    lam = lam_ref[...]
    sp = jnp.maximum(-lam, 0.0) + jnp.log1p(jnp.exp(-jnp.abs(lam)))
    first_seg = lax.broadcasted_iota(jnp.int32, (nseg, LANES), 0) == 0

    for c in range(LB // LANES):
        before_block(c)
        cs = slice(c * LANES, (c + 1) * LANES)
        for s in range(nseg):
            us[c, s * pitch:s * pitch + seg, :] = u_ref[s * seg:(s + 1) * seg, cs]
        U = [us[c, pl.ds(p, nseg, stride=pitch), :] for p in range(seg)]

        before = {}
        for k in range(1, CONV_WIDTH):
            cur = pltpu.roll(U[seg - k], 1, axis=0)
            prev = pltpu.roll(utail[k - 1, :, cs], 1, axis=0)
            before[-k] = jnp.where(first_seg, prev, cur)
        for k in range(1, CONV_WIDTH):
            utail[k - 1, :, cs] = U[seg - k]
        tap = lambda p: U[p] if p >= 0 else before[p]

        cwb = [jnp.broadcast_to(cw_ref[k:k + 1, cs], (nseg, LANES)) for k in range(CONV_WIDTH)]
        cbb = jnp.broadcast_to(cb_ref[:, cs], (nseg, LANES))
        xc = jnp.stack([cbb + (tap(p - 3) * cwb[0] + tap(p - 2) * cwb[1] + tap(p - 1) * cwb[2] + U[p] * cwb[3])
                        for p in range(seg)])

        z = jnp.dot(xc.reshape(T, LANES).astype(BF16), wg_ref[c], preferred_element_type=F32) + bg_ref[c]
        r = jax.nn.sigmoid(z[:, :LANES]).reshape(seg, nseg, LANES)
        ig = jax.nn.sigmoid(z[:, LANES:]).reshape(seg, nseg, LANES)
        log_a = (-RG_C * r) * sp[:, cs]
        a = jnp.exp(log_a)
        om = 1.0 - a * a
        root = jnp.where(om > 0.0, om * lax.rsqrt(om), 0.0)
        b = root * (ig * xc)

        hl, pr = [b[0]], [a[0]]
        for p in range(1, seg):
            hl.append(a[p] * hl[-1] + b[p])
            pr.append(a[p] * pr[-1])

        h_in = [hcar[:, cs]]
        for s in range(nseg):
            h_in.append(hl[-1][s:s + 1, :] + pr[-1][s:s + 1, :] * h_in[-1])
        hcar[:, cs] = h_in[nseg]
        h_in = jnp.concatenate(h_in[:nseg], axis=0)

        for p in range(seg):
            hs[c, pl.ds(p, nseg, stride=pitch), :] = hl[p] + pr[p] * h_in

        for s in range(nseg):
            rows = slice(s * seg, (s + 1) * seg)
            gt = gate_ref[rows, cs]
            gelu = 0.5 * gt * (1.0 + jnp.tanh(math.sqrt(2.0 / math.pi) * (gt + 0.044715 * (gt * gt * gt))))
            y_ref[rows, cs] = (hs[c, s * pitch:s * pitch + seg, :] * gelu).astype(y_ref.dtype)


def _a_mix_kernel(x_ref, gn_ref, w_ref, hg_ref, cw_ref, cb_ref, wg_ref, bg_ref, lam_ref, y_ref, mq_ref,
                  hn_s, u_cur, g_cur, u_new, g_new, us, hs, utail, hcar, *, T, pitch):
    i = pl.program_id(0)

    @pl.when(i <= 1)
    def _():
        utail[...] = jnp.zeros_like(utail)
        hcar[...] = jnp.zeros_like(hcar)

    @pl.when(i == 0)
    def _():
        u_cur[...] = jnp.zeros_like(u_cur)
        g_cur[...] = jnp.zeros_like(g_cur)

    hn_s[...] = _rms_rows(x_ref[...], gn_ref[...]).astype(BF16)
    nb = LRU_WIDTH // HEAD_BLOCK

    def project(jb):
        acc = jnp.dot(hn_s[...], w_ref[:, jb * HEAD_BLOCK:(jb + 1) * HEAD_BLOCK], preferred_element_type=F32)
        if jb < nb:
            u_new[:, jb * HEAD_BLOCK:(jb + 1) * HEAD_BLOCK] = acc
        elif jb < 2 * nb:
            g_new[:, (jb - nb) * HEAD_BLOCK:(jb - nb + 1) * HEAD_BLOCK] = acc
        else:
            mq_ref[...] = _head_norm_cols(acc, hg_ref[...], ATTN_SCALE).astype(mq_ref.dtype)

    n_proj = 2 * nb + 1
    lane_blocks = LRU_WIDTH // LANES
    at_block = {(jb * lane_blocks) // n_proj: jb for jb in range(n_proj)}
    assert len(at_block) == n_proj

    def before_block(c):
        if c in at_block:
            project(at_block[c])

    _rglru_tile(u_cur, g_cur, cw_ref, cb_ref, wg_ref, bg_ref, lam_ref, y_ref, us, hs, utail, hcar,
                T=T, LB=LRU_WIDTH, pitch=pitch, before_block=before_block)

    u_cur[...] = u_new[...]
    g_cur[...] = g_new[...]


def _a_mix(x, norm_gain, w_in, mq_gain, conv_w, conv_b, wg, bg, lam, name, T=512):
    S, D = x.shape
    W = LRU_WIDTH
    nt = S // T
    seg = T // SUBLANES
    pitch = seg + SUBLANES if (seg // SUBLANES) % 2 == 0 else seg
    slab = pltpu.VMEM((W // LANES, SUBLANES * pitch, LANES), F32)
    const = lambda shape: pl.BlockSpec(shape, lambda i: (0,) * len(shape))
    cur = lambda i: jnp.minimum(i, nt - 1)
    prv = lambda i: jnp.maximum(i - 1, 0)
    kern = functools.partial(_a_mix_kernel, T=T, pitch=pitch)
    return pl.pallas_call(
        kern,
        grid=(nt + 1,),
        in_specs=[
            pl.BlockSpec((T, D), lambda i: (cur(i), 0)),
            const((1, D)),
            pl.BlockSpec((D, A_IN_WIDTH), lambda i: (0, 0), pipeline_mode=pl.Buffered(1)),
            const((1, HEAD_DIM)),
            const((CONV_WIDTH, W)),
            const((1, W)),
            const((LRU_BLOCKS, LRU_BLOCK_WIDTH, 2 * LRU_BLOCK_WIDTH)),
            const((LRU_BLOCKS, 1, 2 * LRU_BLOCK_WIDTH)),
            const((1, W)),
        ],
        out_specs=[pl.BlockSpec((T, W), lambda i: (prv(i), 0)),
                   pl.BlockSpec((T, MEM_WIDTH), lambda i: (cur(i), 0))],
        out_shape=[jax.ShapeDtypeStruct((S, W), BF16), jax.ShapeDtypeStruct((S, MEM_WIDTH), BF16)],
        scratch_shapes=[pltpu.VMEM((T, D), BF16)] + [pltpu.VMEM((T, W), F32)] * 4 + [
                        slab, slab, pltpu.VMEM((CONV_WIDTH - 1, SUBLANES, W), F32), pltpu.VMEM((1, W), F32)],
        compiler_params=_params(("arbitrary",)),
        name=name,
    )(x, norm_gain.reshape(1, D), w_in, mq_gain, conv_w, conv_b.reshape(1, W), wg, bg, lam.reshape(1, W))


def _cast_weight_once(w_ref, wb_ref):
    @pl.when(pl.program_id(0) == 0)
    def _():
        wb_ref[...] = w_ref[...].astype(BF16)


def _out_a_kernel(x_ref, y_ref, mq_ref, mk_ref, mv_ref, w_ref, o_ref, wb_ref):
    _cast_weight_once(w_ref, wb_ref)
    acc = jnp.dot(y_ref[...], wb_ref[0:LRU_WIDTH, :], preferred_element_type=F32)
    mem_out = _mem_attn_rows(mq_ref, mk_ref, mv_ref)
    acc += jnp.dot(mem_out, wb_ref[LRU_WIDTH:LRU_WIDTH + MEM_WIDTH, :], preferred_element_type=F32)
    o_ref[...] = x_ref[...] + acc


def _out_a(x, y, mq, mem_k, mem_v, layer, w, widx, name, tm=512):
    S, D = x.shape
    K = LRU_WIDTH + MEM_WIDTH
    row = pl.BlockSpec((tm, D), lambda i: (i, 0))
    return pl.pallas_call(
        _out_a_kernel,
        grid=(S // tm,),
        in_specs=[
            row,
            pl.BlockSpec((tm, LRU_WIDTH), lambda i: (i, 0)),
            pl.BlockSpec((tm, MEM_WIDTH), lambda i: (i, 0)),
            *_mem_kv_specs(layer),
            pl.BlockSpec((None, K, D), lambda i: (widx, 0, 0), pipeline_mode=pl.Buffered(1)),
        ],
        out_specs=row,
        out_shape=jax.ShapeDtypeStruct((S, D), F32),
        scratch_shapes=[pltpu.VMEM((K, D), BF16)],
        compiler_params=_params(("arbitrary",)),
        name=name,
    )(x, y, mq, mem_k, mem_v, w)


def _bucket_tables():
    qi = np.arange(Q_BLOCK)[:, None]
    ki = np.arange(2 * Q_BLOCK)[None, :]
    u = qi + Q_BLOCK - ki
    tables, valids = [], []
    for window, dilation in DIL_PATTERNS:
        n = np.maximum(u * dilation, 0)
        nf = np.maximum(n, 1).astype(np.float32)
        large = REL_MAX_EXACT + (np.log(nf / np.float32(REL_MAX_EXACT))
                                 / np.float32(math.log(REL_MAX_DISTANCE / REL_MAX_EXACT))
                                 * np.float32(REL_BUCKETS - REL_MAX_EXACT)).astype(np.int32)
        large = np.minimum(large, REL_BUCKETS - 1)
        tables.append(np.where(n < REL_MAX_EXACT, n, large).astype(np.int32))
        valids.append(((u >= 0) & (u <= window // dilation)).astype(np.int32))
    return np.stack(tables), np.stack(valids)


def _bias_kernel(tab_ref, idx_ref, valid_ref, o_ref):
    g = pl.program_id(0)
    idx = idx_ref[...]
    valid = valid_ref[...] > 0
    for h in range(DIL_HEADS):
        acc = jnp.zeros(idx.shape, F32)
        for b in range(REL_BUCKETS):
            acc = jnp.where(idx == b, tab_ref[b, g * DIL_HEADS + h], acc)
        o_ref[h] = jnp.where(valid, acc, NEG_INF)


def _bias_expand(rel_bias):
    idx, valid = _bucket_tables()
    blk = pl.BlockSpec((None, Q_BLOCK, 2 * Q_BLOCK), lambda g: (g, 0, 0))
    return pl.pallas_call(
        _bias_kernel,
        grid=(DIL_GROUPS,),
        in_specs=[pl.BlockSpec(memory_space=pltpu.SMEM), blk, blk],
        out_specs=pl.BlockSpec((DIL_HEADS, Q_BLOCK, 2 * Q_BLOCK), lambda g: (g, 0, 0)),
        out_shape=jax.ShapeDtypeStruct((DIL_GROUPS * DIL_HEADS, Q_BLOCK, 2 * Q_BLOCK), F32),
        compiler_params=_params(("arbitrary",)),
        name="bias_expand",
    )(rel_bias, jnp.asarray(idx), jnp.asarray(valid))


def _dil_kernel(q_ref, kc_ref, kp_ref, vc_ref, vp_ref, b_ref, o_ref, l_ref, *, LQ):
    is_first = pl.program_id(1) == 0
    for sb in range(LQ // Q_BLOCK):
        rows = slice(sb * Q_BLOCK, (sb + 1) * Q_BLOCK)
        for h in range(DIL_HEADS):
            hs = slice(h * HEAD_DIM, (h + 1) * HEAD_DIM)
            q = q_ref[rows, hs]
            if sb == 0:
                kk = jnp.concatenate([kp_ref[:, hs], kc_ref[0:Q_BLOCK, hs]], axis=0)
                vv = jnp.concatenate([vp_ref[:, hs], vc_ref[0:Q_BLOCK, hs]], axis=0)
            else:
                kk = kc_ref[(sb - 1) * Q_BLOCK:(sb + 1) * Q_BLOCK, hs]
                vv = vc_ref[(sb - 1) * Q_BLOCK:(sb + 1) * Q_BLOCK, hs]
            s = lax.dot_general(q, kk, (((1,), (1,)), ((), ())), preferred_element_type=F32) + b_ref[h]
            if sb == 0:
                col = lax.broadcasted_iota(jnp.int32, s.shape, 1)
                s = jnp.where(jnp.logical_and(is_first, col < Q_BLOCK), NEG_INF, s)
            m = jnp.max(s, axis=-1, keepdims=True)
            p = jnp.exp(s - m)
            den = jnp.sum(p, axis=-1, keepdims=True)
            o = jnp.dot(p.astype(BF16), vv, preferred_element_type=F32) / den
            o_ref[rows, hs] = o.astype(o_ref.dtype)
            l_ref[rows, hs] = jnp.broadcast_to(m + jnp.log(den), (Q_BLOCK, HEAD_DIM))


def _dil_attn(q, k, v, bias, g, name, LQ=512):
    d, L, _ = q.shape
    assert d == DIL_PATTERNS[g][1]
    ratio = LQ // Q_BLOCK
    kern = functools.partial(_dil_kernel, LQ=LQ)
    cur = pl.BlockSpec((None, LQ, HEAD_BLOCK), lambda r, n: (r, n, 0))
    prev = pl.BlockSpec((None, Q_BLOCK, HEAD_BLOCK), lambda r, n: (r, jnp.maximum(n * ratio - 1, 0), 0))
    return pl.pallas_call(
        kern,
        grid=(d, L // LQ),
        in_specs=[cur, cur, prev, cur, prev,
                  pl.BlockSpec((DIL_HEADS, Q_BLOCK, 2 * Q_BLOCK), lambda r, n: (g, 0, 0))],
        out_specs=[pl.BlockSpec((None, LQ, DIL_OUT_WIDTH), lambda r, n: (r, n, 0)),
                   pl.BlockSpec((None, LQ, DIL_OUT_WIDTH), lambda r, n: (r, n, 0))],
        out_shape=[jax.ShapeDtypeStruct((d, L, DIL_OUT_WIDTH), BF16),
                   jax.ShapeDtypeStruct((d, L, DIL_OUT_WIDTH), F32)],
        compiler_params=_params(("arbitrary", "arbitrary")),
        name=name,
    )(q, k, k, v, v, bias)


def _out_b_kernel(x_ref, o0_ref, l0_ref, o1_ref, l1_ref, o2_ref, l2_ref, mq_ref, mk_ref, mv_ref, w_ref, o_ref,
                  wb_ref, s1_ref, s2_ref, *, tm):
    _cast_weight_once(w_ref, wb_ref)
    mem_out = _mem_attn_rows(mq_ref, mk_ref, mv_ref)
    acc = jnp.dot(mem_out, wb_ref[DIL_OUT_WIDTH:DIL_OUT_WIDTH + MEM_WIDTH, :], preferred_element_type=F32)
    for og_ref, lg_ref, s_ref in ((o1_ref, l1_ref, s1_ref), (o2_ref, l2_ref, s2_ref)):
        d = og_ref.shape[0]
        for r in range(d):
            for c in range(DIL_HEADS):
                cl = slice(c * LANES, (c + 1) * LANES)
                s_ref[c, pl.ds(r, tm // d, stride=d), :] = og_ref[r, :, cl].astype(F32)
                s_ref[DIL_HEADS + c, pl.ds(r, tm // d, stride=d), :] = lg_ref[r, :, cl]
    parts = []
    for h in range(DIL_HEADS):
        hs = slice(h * HEAD_DIM, (h + 1) * HEAD_DIM)
        lses = [l0_ref[0, :, hs], s1_ref[DIL_HEADS + h], s2_ref[DIL_HEADS + h]]
        outs = [o0_ref[0, :, hs].astype(F32), s1_ref[h], s2_ref[h]]
        mx = jnp.maximum(jnp.maximum(lses[0], lses[1]), lses[2])
        es = [jnp.exp(l - mx) for l in lses]
        den = es[0] + es[1] + es[2]
        num = es[0] * outs[0] + es[1] * outs[1] + es[2] * outs[2]
        parts.append((num / den).astype(BF16))
    dil = jnp.concatenate(parts, axis=-1)
    acc += jnp.dot(dil, wb_ref[0:DIL_OUT_WIDTH, :], preferred_element_type=F32)
    o_ref[...] = x_ref[...] + acc


def _out_b(x, ogs, mq, mem_k, mem_v, layer, w, widx, name, tm=512):
    S, D = x.shape
    K = DIL_OUT_WIDTH + MEM_WIDTH
    og_specs, og_args = [], []
    for o, l in ogs:
        d = o.shape[0]
        og_specs += [pl.BlockSpec((d, tm // d, DIL_OUT_WIDTH), lambda i: (0, i, 0)),
                     pl.BlockSpec((d, tm // d, DIL_OUT_WIDTH), lambda i: (0, i, 0))]
        og_args += [o, l]
    slab = pltpu.VMEM((2 * DIL_HEADS, tm, LANES), F32)
    row = pl.BlockSpec((tm, D), lambda i: (i, 0))
    return pl.pallas_call(
        functools.partial(_out_b_kernel, tm=tm),
        grid=(S // tm,),
        in_specs=[
            row,
            *og_specs,
            pl.BlockSpec((tm, MEM_WIDTH), lambda i: (i, 0)),
            *_mem_kv_specs(layer),
            pl.BlockSpec((None, K, D), lambda i: (widx, 0, 0), pipeline_mode=pl.Buffered(1)),
        ],
        out_specs=row,
        out_shape=jax.ShapeDtypeStruct((S, D), F32),
        scratch_shapes=[pltpu.VMEM((K, D), BF16), slab, slab],
        compiler_params=_params(("arbitrary",)),
        name=name,
    )(x, *og_args, mq, mem_k, mem_v, w)


def kernel(x, mem, norm_mix_g, norm_mlp_g, mlp_w1, mlp_w2, mem_norm_g, mem_w_kv, mem_q_norm_g, mem_k_norm_g, a_w_in, a_conv_w, a_conv_b, a_gate_r_w, a_gate_r_b, a_gate_i_w, a_gate_i_b, a_lambda, a_w_out, kv_norm_g, kv_w, k_norm_g, rel_bias, b_w_q, b_q_norm_g, b_w_out):
    B, S, D = x.shape
    assert (B, S, D) == (1, SEQ, D_MODEL) and mem.shape == (1, MEM_TOKENS, D_MODEL)
    xs = x[0]
    ones = jnp.ones((1, HEAD_DIM), F32)

    mem_k, mem_v = _mem_kv(mem[0], mem_norm_g, mem_w_kv, mem_k_norm_g)
    bias = _bias_expand(rel_bias)
    w1b, w2b = mlp_w1[0].astype(BF16), mlp_w2[0].astype(BF16)
    w_in_b = a_w_in[0].astype(BF16)
    later = {}

    for layer in range(DEPTH):
        mq_gain = mem_q_norm_g[layer][None, :]
        a_in, a_gain = xs, norm_mix_g[layer]
        if layer < N_A_LAYERS:
            i = layer
            wg = jnp.concatenate([a_gate_r_w[i], a_gate_i_w[i]], axis=-1).astype(BF16)
            bg = jnp.concatenate([a_gate_r_b[i], a_gate_i_b[i]], axis=-1)[:, None, :]
            y, mq = _a_mix(a_in, a_gain, w_in_b, mq_gain, a_conv_w[i], a_conv_b[i], wg, bg,
                           a_lambda[i], f"a{i}_mix")
            xs = _out_a(xs, y, mq, mem_k, mem_v, layer, a_w_out, i, f"a{i}_out_proj")
        else:
            j = layer - N_A_LAYERS
            dils = tuple(d for _, d in DIL_PATTERNS)
            dil_groups = tuple((1, d, BF16) for d in dils)
            if j == 0:
                kv_gains = jnp.concatenate([k_norm_g, jnp.tile(ones, (DIL_GROUPS, 1))], axis=0)
                kvs = _proj(xs, kv_norm_g, later["kv_w"], kv_gains, dil_groups + dil_groups,
                            (True,) * DIL_GROUPS + (False,) * DIL_GROUPS, 1.0, "shared_kv", tm=1024)
            q_gains = jnp.concatenate([b_q_norm_g[j], mq_gain], axis=0)
            qs = _proj(a_in, a_gain, later["w_q"], q_gains, dil_groups + ((1, 1, BF16),),
                       (True,) * (DIL_GROUPS + 1), ATTN_SCALE, f"b{j}_q_proj", tm=1024)
            ogs = [_dil_attn(qs[g], kvs[g], kvs[DIL_GROUPS + g], bias, g, f"b{j}_dil{g}")
                   for g in range(DIL_GROUPS)]
            xs = _out_b(xs, ogs, qs[DIL_GROUPS].reshape(S, MEM_WIDTH), mem_k, mem_v, layer, b_w_out, j,
                        f"b{j}_out_proj")

        nl = layer + 1
        casts, names = [], []
        if nl < DEPTH:
            casts += [(mlp_w1, nl), (mlp_w2, nl)]
            names += ["w1", "w2"]
            if nl < N_A_LAYERS:
                casts.append((a_w_in, nl))
                names.append("w_in")
            else:
                casts.append((b_w_q, nl - N_A_LAYERS))
                names.append("w_q")
                if nl == N_A_LAYERS:
                    casts.append((kv_w, None))
                    names.append("kv_w")
        xs, cast_out = _mlp(xs, norm_mlp_g[layer], w1b, w2b, casts, f"mlp{layer}")
        later = dict(zip(names, cast_out))
        w1b, w2b, w_in_b = later.get("w1"), later.get("w2"), later.get("w_in")
    return xs[None]
```

```python
import functools
import math

import numpy as np
import jax
import jax.numpy as jnp
from jax import lax
from jax.experimental import pallas as pl
from jax.experimental.pallas import tpu as pltpu

F32 = jnp.float32
BF16 = jnp.bfloat16

D_MODEL = 2048
SEQ = 8192
DEPTH = 4
N_A_LAYERS = DEPTH // 2
HEAD_DIM = 128
D_FF = 4 * D_MODEL
LRU_WIDTH = 3 * D_MODEL // 4
LRU_BLOCK_WIDTH = 128
LRU_BLOCKS = LRU_WIDTH // LRU_BLOCK_WIDTH
CONV_WIDTH = 4
RG_C = 8.0
MEM_TOKENS = 256
MEM_HEADS = 4
MEM_WIDTH = MEM_HEADS * HEAD_DIM
DIL_PATTERNS = ((128, 1), (512, 4), (2048, 16))
DIL_GROUPS = len(DIL_PATTERNS)
DIL_HEADS = 4
DIL_WIDTH = DIL_GROUPS * DIL_HEADS * HEAD_DIM
DIL_OUT_WIDTH = DIL_HEADS * HEAD_DIM
Q_BLOCK = 128
REL_BUCKETS = 32
REL_MAX_EXACT = REL_BUCKETS // 2
REL_MAX_DISTANCE = 2048
A_IN_WIDTH = 2 * LRU_WIDTH + MEM_WIDTH
NORM_EPS = 1e-6
NEG_INF = -1e30
ATTN_SCALE = HEAD_DIM ** -0.5

V7X_VMEM_BYTES = 64 * 1024 * 1024
SCOPED_VMEM_MIB = 60
assert SCOPED_VMEM_MIB * 1024 * 1024 < V7X_VMEM_BYTES
LANES = 128
SUBLANES = 8
HEAD_BLOCK = DIL_HEADS * HEAD_DIM

def _params(semantics):
    return pltpu.CompilerParams(dimension_semantics=semantics,
                                vmem_limit_bytes=SCOPED_VMEM_MIB * 1024 * 1024)


def _rms_rows(x, g):
    ms = jnp.mean(x * x, axis=-1, keepdims=True)
    return x * lax.rsqrt(ms + NORM_EPS) * g


def _head_norm_cols(a, g, scale):
    parts = []
    for h in range(a.shape[1] // HEAD_DIM):
        ah = a[:, h * HEAD_DIM:(h + 1) * HEAD_DIM]
        y = ah * lax.rsqrt(jnp.mean(ah * ah, axis=-1, keepdims=True) + NORM_EPS) * g
        parts.append(y * scale if scale != 1.0 else y)
    return jnp.concatenate(parts, axis=-1)


def _proj_kernel(a_ref, *rest, groups, normed, scale, tm, has_norm, has_scr):
    rest = list(rest)
    gn_ref = rest.pop(0) if has_norm else None
    w_ref, hg_ref = rest.pop(0), rest.pop(0)
    o_refs = [rest.pop(0) for _ in groups]
    hn_s = rest.pop(0) if has_norm else None
    scr = rest.pop(0) if has_scr else None
    if has_norm:
        hn_s[...] = _rms_rows(a_ref[...], gn_ref[...]).astype(BF16)
        lhs_ref = hn_s
    else:
        lhs_ref = a_ref
    jb = 0
    for (nb, d, _), o_ref in zip(groups, o_refs):
        for k in range(nb):
            acc = jnp.dot(lhs_ref[...], w_ref[:, jb * HEAD_BLOCK:(jb + 1) * HEAD_BLOCK],
                          preferred_element_type=F32)
            if normed[jb]:
                acc = _head_norm_cols(acc, hg_ref[jb:jb + 1, :], scale)
            if d == 1:
                o_ref[0, :, k * HEAD_BLOCK:(k + 1) * HEAD_BLOCK] = acc.astype(o_ref.dtype)
            else:
                for h in range(HEAD_BLOCK // LANES):
                    scr[h] = acc[:, h * LANES:(h + 1) * LANES]
                for r in range(d):
                    for h in range(HEAD_BLOCK // LANES):
                        o_ref[r, :, h * LANES:(h + 1) * LANES] = (
                            scr[h, pl.ds(r, tm // d, stride=d), :].astype(o_ref.dtype))
            jb += 1


def _proj(a, norm_gain, w, head_gains, groups, normed, scale, name, tm):
    S, D = a.shape
    N = w.shape[1]
    nblk = N // HEAD_BLOCK
    assert sum(nb for nb, _, _ in groups) == len(normed) == nblk
    assert all(nb == 1 for nb, d, _ in groups if d > 1)
    has_norm = norm_gain is not None
    has_scr = any(d > 1 for _, d, _ in groups)
    kern = functools.partial(_proj_kernel, groups=groups, normed=normed, scale=scale, tm=tm,
                             has_norm=has_norm, has_scr=has_scr)
    in_specs = [pl.BlockSpec((tm, D), lambda i: (i, 0))]
    args = [a]
    if has_norm:
        in_specs.append(pl.BlockSpec((1, D), lambda i: (0, 0)))
        args.append(norm_gain.reshape(1, D))
    in_specs += [pl.BlockSpec((D, N), lambda i: (0, 0), pipeline_mode=pl.Buffered(1)),
                 pl.BlockSpec((nblk, HEAD_DIM), lambda i: (0, 0))]
    args += [w, head_gains]
    scratch = []
    if has_norm:
        scratch.append(pltpu.VMEM((tm, D), BF16))
    if has_scr:
        scratch.append(pltpu.VMEM((HEAD_BLOCK // LANES, tm, LANES), F32))
    return pl.pallas_call(
        kern,
        grid=(S // tm,),
        in_specs=in_specs,
        out_specs=[pl.BlockSpec((d, tm // d, nb * HEAD_BLOCK), lambda i: (0, i, 0)) for nb, d, _ in groups],
        out_shape=[jax.ShapeDtypeStruct((d, S // d, nb * HEAD_BLOCK), dt) for nb, d, dt in groups],
        scratch_shapes=scratch,
        compiler_params=_params(("parallel",)),
        name=name,
    )(*args)


def _mlp_kernel(x_ref, g_ref, w1_ref, w2_ref, *rest, n_casts):
    cast_in, o_ref, cast_out, hn_ref = rest[:n_casts], rest[n_casts], rest[n_casts + 1:-1], rest[-1]

    @pl.when(pl.program_id(1) == 0)
    def _():
        x = x_ref[...]
        hn_ref[...] = _rms_rows(x, g_ref[...]).astype(BF16)
        o_ref[...] = x

    h = jnp.dot(hn_ref[...], w1_ref[...], preferred_element_type=F32)
    h = jnp.maximum(h, 0.0)
    h = (h * h).astype(BF16)
    o_ref[...] += jnp.dot(h, w2_ref[...], preferred_element_type=F32)

    for src, dst in zip(cast_in, cast_out):
        dst[...] = src[...].astype(BF16)


def _mlp(x, g, w1b, w2b, casts, name, tm=1024, tf=512):
    S, D = x.shape
    F = w1b.shape[1]
    ni, nj = S // tm, F // tf
    row_spec = pl.BlockSpec((tm, D), lambda i, j: (i, 0))
    in_specs = [row_spec, pl.BlockSpec((1, D), lambda i, j: (0, 0)),
                pl.BlockSpec((D, tf), lambda i, j: (0, j)),
                pl.BlockSpec((tf, D), lambda i, j: (j, 0))]
    args = [x, g.reshape(1, D), w1b, w2b]
    out_specs = [row_spec]
    out_shape = [jax.ShapeDtypeStruct((S, D), F32)]
    for w, idx in casts:
        R, C = w.shape[-2:]
        r = R // (ni * nj)
        assert r * ni * nj == R and r % 16 == 0, (R, ni, nj)
        if idx is None:
            in_specs.append(pl.BlockSpec((r, C), lambda i, j: (i * nj + j, 0)))
        else:
            in_specs.append(pl.BlockSpec((None, r, C), lambda i, j, idx=idx: (idx, i * nj + j, 0)))
        args.append(w)
        out_specs.append(pl.BlockSpec((r, C), lambda i, j: (i * nj + j, 0)))
        out_shape.append(jax.ShapeDtypeStruct((R, C), BF16))
    kern = functools.partial(_mlp_kernel, n_casts=len(casts))
    outs = pl.pallas_call(
        kern,
        grid=(ni, nj),
        in_specs=in_specs,
        out_specs=out_specs,
        out_shape=out_shape,
        scratch_shapes=[pltpu.VMEM((tm, D), BF16)],
        compiler_params=_params(("parallel", "arbitrary")),
        name=name,
    )(*args)
    return outs[0], list(outs[1:])


def _mem_kv_kernel(mem_ref, g_ref, w_ref, kg_ref, k_ref, v_ref):
    hn = _rms_rows(mem_ref[...], g_ref[...]).astype(BF16)
    kv = jnp.dot(hn, w_ref[...].astype(BF16), preferred_element_type=F32)
    k_ref[...] = _head_norm_cols(kv[:, :MEM_WIDTH], kg_ref[...], 1.0).astype(BF16)
    v_ref[...] = kv[:, MEM_WIDTH:].astype(BF16)


def _mem_kv(mem, mem_norm_g, mem_w_kv, mem_k_norm_g):
    M, D = mem.shape
    out = jax.ShapeDtypeStruct((DEPTH, M, MEM_WIDTH), BF16)
    return pl.pallas_call(
        _mem_kv_kernel,
        grid=(DEPTH,),
        in_specs=[
            pl.BlockSpec((M, D), lambda l: (0, 0)),
            pl.BlockSpec((None, 1, D), lambda l: (l, 0, 0)),
            pl.BlockSpec((None, D, 2 * MEM_WIDTH), lambda l: (l, 0, 0)),
            pl.BlockSpec((None, 1, HEAD_DIM), lambda l: (l, 0, 0)),
        ],
        out_specs=[pl.BlockSpec((None, M, MEM_WIDTH), lambda l: (l, 0, 0)),
                   pl.BlockSpec((None, M, MEM_WIDTH), lambda l: (l, 0, 0))],
        out_shape=[out, out],
        compiler_params=_params(("arbitrary",)),
        name="mem_kv",
    )(mem, mem_norm_g.reshape(DEPTH, 1, D), mem_w_kv, mem_k_norm_g.reshape(DEPTH, 1, HEAD_DIM))


def _mem_attn_rows(q_ref, k_ref, v_ref):
    outs = []
    for h in range(MEM_HEADS):
        hs = slice(h * HEAD_DIM, (h + 1) * HEAD_DIM)
        s = lax.dot_general(q_ref[:, hs], k_ref[:, hs], (((1,), (1,)), ((), ())), preferred_element_type=F32)
        m = jnp.max(s, axis=-1, keepdims=True)
        p = jnp.exp(s - m)
        den = jnp.sum(p, axis=-1, keepdims=True)
        o = jnp.dot(p.astype(BF16), v_ref[:, hs], preferred_element_type=F32) / den
        outs.append(o.astype(BF16))
    return jnp.concatenate(outs, axis=-1)


def _mem_kv_specs(layer):
    spec = pl.BlockSpec((None, MEM_TOKENS, MEM_WIDTH), lambda i: (layer, 0, 0))
    return [spec, spec]


def _rglru_tile(u_ref, gate_ref, cw_ref, cb_ref, wg_ref, bg_ref, lam_ref, y_ref, us, hs, utail, hcar,
                *, T, LB, pitch, before_block=lambda c: None):
    nseg = SUBLANES
    seg = T // nseg
    lam = lam_ref[...]
    sp = jnp.maximum(-lam, 0.0) + jnp.log1p(jnp.exp(-jnp.abs(lam)))
    first_seg = lax.broadcasted_iota(jnp.int32, (nseg, LANES), 0) == 0

    for c in range(LB // LANES):
        before_block(c)
        cs = slice(c * LANES, (c + 1) * LANES)
        for s in range(nseg):
            us[c, s * pitch:s * pitch + seg, :] = u_ref[s * seg:(s + 1) * seg, cs]
        U = [us[c, pl.ds(p, nseg, stride=pitch), :] for p in range(seg)]

        before = {}
        for k in range(1, CONV_WIDTH):
            cur = pltpu.roll(U[seg - k], 1, axis=0)
            prev = pltpu.roll(utail[k - 1, :, cs], 1, axis=0)
            before[-k] = jnp.where(first_seg, prev, cur)
        for k in range(1, CONV_WIDTH):
            utail[k - 1, :, cs] = U[seg - k]
        tap = lambda p: U[p] if p >= 0 else before[p]

        cwb = [jnp.broadcast_to(cw_ref[k:k + 1, cs], (nseg, LANES)) for k in range(CONV_WIDTH)]
        cbb = jnp.broadcast_to(cb_ref[:, cs], (nseg, LANES))
        xc = jnp.stack([cbb + (tap(p - 3) * cwb[0] + tap(p - 2) * cwb[1] + tap(p - 1) * cwb[2] + U[p] * cwb[3])
                        for p in range(seg)])

        z = jnp.dot(xc.reshape(T, LANES).astype(BF16), wg_ref[c], preferred_element_type=F32) + bg_ref[c]
        r = jax.nn.sigmoid(z[:, :LANES]).reshape(seg, nseg, LANES)
        ig = jax.nn.sigmoid(z[:, LANES:]).reshape(seg, nseg, LANES)
        log_a = (-RG_C * r) * sp[:, cs]
        a = jnp.exp(log_a)
        om = 1.0 - a * a
        root = jnp.where(om > 0.0, om * lax.rsqrt(om), 0.0)
        b = root * (ig * xc)

        hl, pr = [b[0]], [a[0]]
        for p in range(1, seg):
            hl.append(a[p] * hl[-1] + b[p])
            pr.append(a[p] * pr[-1])

        h_in = [hcar[:, cs]]
        for s in range(nseg):
            h_in.append(hl[-1][s:s + 1, :] + pr[-1][s:s + 1, :] * h_in[-1])
        hcar[:, cs] = h_in[nseg]
        h_in = jnp.concatenate(h_in[:nseg], axis=0)

        for p in range(seg):
            hs[c, pl.ds(p, nseg, stride=pitch), :] = hl[p] + pr[p] * h_in

        for s in range(nseg):
            rows = slice(s * seg, (s + 1) * seg)
            gt = gate_ref[rows, cs]
            gelu = 0.5 * gt * (1.0 + jnp.tanh(math.sqrt(2.0 / math.pi) * (gt + 0.044715 * (gt * gt * gt))))
            y_ref[rows, cs] = (hs[c, s * pitch:s * pitch + seg, :] * gelu).astype(y_ref.dtype)


def _a_mix_kernel(x_ref, gn_ref, w_ref, hg_ref, cw_ref, cb_ref, wg_ref, bg_ref, lam_ref, y_ref, mq_ref,
                  hn_s, u_cur, g_cur, u_new, g_new, us, hs, utail, hcar, *, T, pitch):
    i = pl.program_id(0)

    @pl.when(i <= 1)
    def _():
        utail[...] = jnp.zeros_like(utail)
        hcar[...] = jnp.zeros_like(hcar)

    @pl.when(i == 0)
    def _():
        u_cur[...] = jnp.zeros_like(u_cur)
        g_cur[...] = jnp.zeros_like(g_cur)

    hn_s[...] = _rms_rows(x_ref[...], gn_ref[...]).astype(BF16)
    nb = LRU_WIDTH // HEAD_BLOCK

    def project(jb):
        acc = jnp.dot(hn_s[...], w_ref[:, jb * HEAD_BLOCK:(jb + 1) * HEAD_BLOCK], preferred_element_type=F32)
        if jb < nb:
            u_new[:, jb * HEAD_BLOCK:(jb + 1) * HEAD_BLOCK] = acc
        elif jb < 2 * nb:
            g_new[:, (jb - nb) * HEAD_BLOCK:(jb - nb + 1) * HEAD_BLOCK] = acc
        else:
            mq_ref[...] = _head_norm_cols(acc, hg_ref[...], ATTN_SCALE).astype(mq_ref.dtype)

    n_proj = 2 * nb + 1
    lane_blocks = LRU_WIDTH // LANES
    at_block = {(jb * lane_blocks) // n_proj: jb for jb in range(n_proj)}
    assert len(at_block) == n_proj

    def before_block(c):
        if c in at_block:
            project(at_block[c])

    _rglru_tile(u_cur, g_cur, cw_ref, cb_ref, wg_ref, bg_ref, lam_ref, y_ref, us, hs, utail, hcar,
                T=T, LB=LRU_WIDTH, pitch=pitch, before_block=before_block)

    u_cur[...] = u_new[...]
    g_cur[...] = g_new[...]


def _a_mix(x, norm_gain, w_in, mq_gain, conv_w, conv_b, wg, bg, lam, name, T=512):
    S, D = x.shape
    W = LRU_WIDTH
    nt = S // T
    seg = T // SUBLANES
    pitch = seg + SUBLANES if (seg // SUBLANES) % 2 == 0 else seg
    slab = pltpu.VMEM((W // LANES, SUBLANES * pitch, LANES), F32)
    const = lambda shape: pl.BlockSpec(shape, lambda i: (0,) * len(shape))
    cur = lambda i: jnp.minimum(i, nt - 1)
    prv = lambda i: jnp.maximum(i - 1, 0)
    kern = functools.partial(_a_mix_kernel, T=T, pitch=pitch)
    return pl.pallas_call(
        kern,
        grid=(nt + 1,),
        in_specs=[
            pl.BlockSpec((T, D), lambda i: (cur(i), 0)),
            const((1, D)),
            pl.BlockSpec((D, A_IN_WIDTH), lambda i: (0, 0), pipeline_mode=pl.Buffered(1)),
            const((1, HEAD_DIM)),
            const((CONV_WIDTH, W)),
            const((1, W)),
            const((LRU_BLOCKS, LRU_BLOCK_WIDTH, 2 * LRU_BLOCK_WIDTH)),
            const((LRU_BLOCKS, 1, 2 * LRU_BLOCK_WIDTH)),
            const((1, W)),
        ],
        out_specs=[pl.BlockSpec((T, W), lambda i: (prv(i), 0)),
                   pl.BlockSpec((T, MEM_WIDTH), lambda i: (cur(i), 0))],
        out_shape=[jax.ShapeDtypeStruct((S, W), BF16), jax.ShapeDtypeStruct((S, MEM_WIDTH), BF16)],
        scratch_shapes=[pltpu.VMEM((T, D), BF16)] + [pltpu.VMEM((T, W), F32)] * 4 + [
                        slab, slab, pltpu.VMEM((CONV_WIDTH - 1, SUBLANES, W), F32), pltpu.VMEM((1, W), F32)],
        compiler_params=_params(("arbitrary",)),
        name=name,
    )(x, norm_gain.reshape(1, D), w_in, mq_gain, conv_w, conv_b.reshape(1, W), wg, bg, lam.reshape(1, W))


def _cast_weight_once(w_ref, wb_ref):
    @pl.when(pl.program_id(0) == 0)
    def _():
        wb_ref[...] = w_ref[...].astype(BF16)


def _out_a_kernel(x_ref, y_ref, mq_ref, mk_ref, mv_ref, w_ref, o_ref):
    acc = jnp.dot(y_ref[...], w_ref[0:LRU_WIDTH, :], preferred_element_type=F32)
    mem_out = _mem_attn_rows(mq_ref, mk_ref, mv_ref)
    acc += jnp.dot(mem_out, w_ref[LRU_WIDTH:LRU_WIDTH + MEM_WIDTH, :], preferred_element_type=F32)
    o_ref[...] = x_ref[...] + acc


def _out_a(x, y, mq, mem_k, mem_v, layer, w, name, tm=1024):
    S, D = x.shape
    K = LRU_WIDTH + MEM_WIDTH
    row = pl.BlockSpec((tm, D), lambda i: (i, 0))
    return pl.pallas_call(
        _out_a_kernel,
        grid=(S // tm,),
        in_specs=[
            row,
            pl.BlockSpec((tm, LRU_WIDTH), lambda i: (i, 0)),
            pl.BlockSpec((tm, MEM_WIDTH), lambda i: (i, 0)),
            *_mem_kv_specs(layer),
            pl.BlockSpec((K, D), lambda i: (0, 0), pipeline_mode=pl.Buffered(1)),
        ],
        out_specs=row,
        out_shape=jax.ShapeDtypeStruct((S, D), F32),
        compiler_params=_params(("parallel",)),
        name=name,
    )(x, y, mq, mem_k, mem_v, w)


def _bucket_tables():
    qi = np.arange(Q_BLOCK)[:, None]
    ki = np.arange(2 * Q_BLOCK)[None, :]
    u = qi + Q_BLOCK - ki
    tables, valids = [], []
    for window, dilation in DIL_PATTERNS:
        n = np.maximum(u * dilation, 0)
        nf = np.maximum(n, 1).astype(np.float32)
        large = REL_MAX_EXACT + (np.log(nf / np.float32(REL_MAX_EXACT))
                                 / np.float32(math.log(REL_MAX_DISTANCE / REL_MAX_EXACT))
                                 * np.float32(REL_BUCKETS - REL_MAX_EXACT)).astype(np.int32)
        large = np.minimum(large, REL_BUCKETS - 1)
        tables.append(np.where(n < REL_MAX_EXACT, n, large).astype(np.int32))
        valids.append(((u >= 0) & (u <= window // dilation)).astype(np.int32))
    return np.stack(tables), np.stack(valids)


def _bias_kernel(tab_ref, idx_ref, valid_ref, o_ref):
    g = pl.program_id(0)
    idx = idx_ref[...]
    valid = valid_ref[...] > 0
    for h in range(DIL_HEADS):
        acc = jnp.zeros(idx.shape, F32)
        for b in range(REL_BUCKETS):
            acc = jnp.where(idx == b, tab_ref[b, g * DIL_HEADS + h], acc)
        o_ref[h] = jnp.where(valid, acc, NEG_INF)


def _bias_expand(rel_bias):
    idx, valid = _bucket_tables()
    blk = pl.BlockSpec((None, Q_BLOCK, 2 * Q_BLOCK), lambda g: (g, 0, 0))
    return pl.pallas_call(
        _bias_kernel,
        grid=(DIL_GROUPS,),
        in_specs=[pl.BlockSpec(memory_space=pltpu.SMEM), blk, blk],
        out_specs=pl.BlockSpec((DIL_HEADS, Q_BLOCK, 2 * Q_BLOCK), lambda g: (g, 0, 0)),
        out_shape=jax.ShapeDtypeStruct((DIL_GROUPS * DIL_HEADS, Q_BLOCK, 2 * Q_BLOCK), F32),
        compiler_params=_params(("arbitrary",)),
        name="bias_expand",
    )(rel_bias, jnp.asarray(idx), jnp.asarray(valid))


def _dil_kernel(q_ref, kc_ref, kp_ref, vc_ref, vp_ref, b_ref, o_ref, l_ref, *, LQ):
    is_first = pl.program_id(1) == 0
    for sb in range(LQ // Q_BLOCK):
        rows = slice(sb * Q_BLOCK, (sb + 1) * Q_BLOCK)
        for h in range(DIL_HEADS):
            hs = slice(h * HEAD_DIM, (h + 1) * HEAD_DIM)
            q = q_ref[rows, hs]
            if sb == 0:
                kk = jnp.concatenate([kp_ref[:, hs], kc_ref[0:Q_BLOCK, hs]], axis=0)
                vv = jnp.concatenate([vp_ref[:, hs], vc_ref[0:Q_BLOCK, hs]], axis=0)
            else:
                kk = kc_ref[(sb - 1) * Q_BLOCK:(sb + 1) * Q_BLOCK, hs]
                vv = vc_ref[(sb - 1) * Q_BLOCK:(sb + 1) * Q_BLOCK, hs]
            s = lax.dot_general(q, kk, (((1,), (1,)), ((), ())), preferred_element_type=F32) + b_ref[h]
            if sb == 0:
                col = lax.broadcasted_iota(jnp.int32, s.shape, 1)
                s = jnp.where(jnp.logical_and(is_first, col < Q_BLOCK), NEG_INF, s)
            m = jnp.max(s, axis=-1, keepdims=True)
            p = jnp.exp(s - m)
            den = jnp.sum(p, axis=-1, keepdims=True)
            o = jnp.dot(p.astype(BF16), vv, preferred_element_type=F32) / den
            o_ref[rows, hs] = o.astype(o_ref.dtype)
            l_ref[rows, hs] = jnp.broadcast_to(m + jnp.log(den), (Q_BLOCK, HEAD_DIM))


def _dil_attn(q, k, v, bias, g, name, max_lq=1024):
    d, L, _ = q.shape
    assert d == DIL_PATTERNS[g][1]
    LQ = min(max_lq, L)
    ratio = LQ // Q_BLOCK
    kern = functools.partial(_dil_kernel, LQ=LQ)
    cur = pl.BlockSpec((None, LQ, HEAD_BLOCK), lambda r, n: (r, n, 0))
    prev = pl.BlockSpec((None, Q_BLOCK, HEAD_BLOCK), lambda r, n: (r, jnp.maximum(n * ratio - 1, 0), 0))
    return pl.pallas_call(
        kern,
        grid=(d, L // LQ),
        in_specs=[cur, cur, prev, cur, prev,
                  pl.BlockSpec((DIL_HEADS, Q_BLOCK, 2 * Q_BLOCK), lambda r, n: (g, 0, 0))],
        out_specs=[pl.BlockSpec((None, LQ, DIL_OUT_WIDTH), lambda r, n: (r, n, 0)),
                   pl.BlockSpec((None, LQ, DIL_OUT_WIDTH), lambda r, n: (r, n, 0))],
        out_shape=[jax.ShapeDtypeStruct((d, L, DIL_OUT_WIDTH), BF16),
                   jax.ShapeDtypeStruct((d, L, DIL_OUT_WIDTH), F32)],
        compiler_params=_params(("arbitrary", "arbitrary")),
        name=name,
    )(q, k, k, v, v, bias)


def _out_b_kernel(x_ref, o0_ref, l0_ref, o1_ref, l1_ref, o2_ref, l2_ref, mq_ref, mk_ref, mv_ref, w_ref, o_ref,
                  wb_ref, s1_ref, s2_ref, *, tm):
    _cast_weight_once(w_ref, wb_ref)
    mem_out = _mem_attn_rows(mq_ref, mk_ref, mv_ref)
    acc = jnp.dot(mem_out, wb_ref[DIL_OUT_WIDTH:DIL_OUT_WIDTH + MEM_WIDTH, :], preferred_element_type=F32)
    for og_ref, lg_ref, s_ref in ((o1_ref, l1_ref, s1_ref), (o2_ref, l2_ref, s2_ref)):
        d = og_ref.shape[0]
        for r in range(d):
            for c in range(DIL_HEADS):
                cl = slice(c * LANES, (c + 1) * LANES)
                s_ref[c, pl.ds(r, tm // d, stride=d), :] = og_ref[r, :, cl].astype(F32)
                s_ref[DIL_HEADS + c, pl.ds(r, tm // d, stride=d), :] = lg_ref[r, :, cl]
    parts = []
    for h in range(DIL_HEADS):
        hs = slice(h * HEAD_DIM, (h + 1) * HEAD_DIM)
        lses = [l0_ref[0, :, hs], s1_ref[DIL_HEADS + h], s2_ref[DIL_HEADS + h]]
        outs = [o0_ref[0, :, hs].astype(F32), s1_ref[h], s2_ref[h]]
        mx = jnp.maximum(jnp.maximum(lses[0], lses[1]), lses[2])
        es = [jnp.exp(l - mx) for l in lses]
        den = es[0] + es[1] + es[2]
        num = es[0] * outs[0] + es[1] * outs[1] + es[2] * outs[2]
        parts.append((num / den).astype(BF16))
    dil = jnp.concatenate(parts, axis=-1)
    acc += jnp.dot(dil, wb_ref[0:DIL_OUT_WIDTH, :], preferred_element_type=F32)
    o_ref[...] = x_ref[...] + acc


def _out_b(x, ogs, mq, mem_k, mem_v, layer, w, widx, name, tm=512):
    S, D = x.shape
    K = DIL_OUT_WIDTH + MEM_WIDTH
    og_specs, og_args = [], []
    for o, l in ogs:
        d = o.shape[0]
        og_specs += [pl.BlockSpec((d, tm // d, DIL_OUT_WIDTH), lambda i: (0, i, 0)),
                     pl.BlockSpec((d, tm // d, DIL_OUT_WIDTH), lambda i: (0, i, 0))]
        og_args += [o, l]
    slab = pltpu.VMEM((2 * DIL_HEADS, tm, LANES), F32)
    row = pl.BlockSpec((tm, D), lambda i: (i, 0))
    return pl.pallas_call(
        functools.partial(_out_b_kernel, tm=tm),
        grid=(S // tm,),
        in_specs=[
            row,
            *og_specs,
            pl.BlockSpec((tm, MEM_WIDTH), lambda i: (i, 0)),
            *_mem_kv_specs(layer),
            pl.BlockSpec((None, K, D), lambda i: (widx, 0, 0), pipeline_mode=pl.Buffered(1)),
        ],
        out_specs=row,
        out_shape=jax.ShapeDtypeStruct((S, D), F32),
        scratch_shapes=[pltpu.VMEM((K, D), BF16), slab, slab],
        compiler_params=_params(("arbitrary",)),
        name=name,
    )(x, *og_args, mq, mem_k, mem_v, w)


def kernel(x, mem, norm_mix_g, norm_mlp_g, mlp_w1, mlp_w2, mem_norm_g, mem_w_kv, mem_q_norm_g, mem_k_norm_g, a_w_in, a_conv_w, a_conv_b, a_gate_r_w, a_gate_r_b, a_gate_i_w, a_gate_i_b, a_lambda, a_w_out, kv_norm_g, kv_w, k_norm_g, rel_bias, b_w_q, b_q_norm_g, b_w_out):
    B, S, D = x.shape
    assert (B, S, D) == (1, SEQ, D_MODEL) and mem.shape == (1, MEM_TOKENS, D_MODEL)
    xs = x[0]
    ones = jnp.ones((1, HEAD_DIM), F32)

    mem_k, mem_v = _mem_kv(mem[0], mem_norm_g, mem_w_kv, mem_k_norm_g)
    bias = _bias_expand(rel_bias)
    w1b, w2b = mlp_w1[0].astype(BF16), mlp_w2[0].astype(BF16)
    w_in_b, w_out_b = a_w_in[0].astype(BF16), a_w_out[0].astype(BF16)
    later = {}

    for layer in range(DEPTH):
        mq_gain = mem_q_norm_g[layer][None, :]
        a_in, a_gain = xs, norm_mix_g[layer]
        if layer < N_A_LAYERS:
            i = layer
            wg = jnp.concatenate([a_gate_r_w[i], a_gate_i_w[i]], axis=-1).astype(BF16)
            bg = jnp.concatenate([a_gate_r_b[i], a_gate_i_b[i]], axis=-1)[:, None, :]
            y, mq = _a_mix(a_in, a_gain, w_in_b, mq_gain, a_conv_w[i], a_conv_b[i], wg, bg,
                           a_lambda[i], f"a{i}_mix")
            xs = _out_a(xs, y, mq, mem_k, mem_v, layer, w_out_b, f"a{i}_out_proj")
        else:
            j = layer - N_A_LAYERS
            dils = tuple(d for _, d in DIL_PATTERNS)
            dil_groups = tuple((1, d, BF16) for d in dils)
            if j == 0:
                kv_gains = jnp.concatenate([k_norm_g, jnp.tile(ones, (DIL_GROUPS, 1))], axis=0)
                kvs = _proj(xs, kv_norm_g, later["kv_w"], kv_gains, dil_groups + dil_groups,
                            (True,) * DIL_GROUPS + (False,) * DIL_GROUPS, 1.0, "shared_kv", tm=1024)
            q_gains = jnp.concatenate([b_q_norm_g[j], mq_gain], axis=0)
            qs = _proj(a_in, a_gain, later["w_q"], q_gains, dil_groups + ((1, 1, BF16),),
                       (True,) * (DIL_GROUPS + 1), ATTN_SCALE, f"b{j}_q_proj", tm=1024)
            ogs = [_dil_attn(qs[g], kvs[g], kvs[DIL_GROUPS + g], bias, g, f"b{j}_dil{g}")
                   for g in range(DIL_GROUPS)]
            xs = _out_b(xs, ogs, qs[DIL_GROUPS].reshape(S, MEM_WIDTH), mem_k, mem_v, layer, b_w_out, j,
                        f"b{j}_out_proj")

        nl = layer + 1
        casts, names = [], []
        if nl < DEPTH:
            casts += [(mlp_w1, nl), (mlp_w2, nl)]
            names += ["w1", "w2"]
            if nl < N_A_LAYERS:
                casts += [(a_w_in, nl), (a_w_out, nl)]
                names += ["w_in", "w_out"]
            else:
                casts.append((b_w_q, nl - N_A_LAYERS))
                names.append("w_q")
                if nl == N_A_LAYERS:
                    casts.append((kv_w, None))
                    names.append("kv_w")
        xs, cast_out = _mlp(xs, norm_mlp_g[layer], w1b, w2b, casts, f"mlp{layer}")
        later = dict(zip(names, cast_out))
        w1b, w2b, w_in_b, w_out_b = (later.get(n) for n in ("w1", "w2", "w_in", "w_out"))
    return xs[None]
```

```python
import functools
import math

import numpy as np
import jax
import jax.numpy as jnp
from jax import lax
from jax.experimental import pallas as pl
from jax.experimental.pallas import tpu as pltpu

F32 = jnp.float32
BF16 = jnp.bfloat16

D_MODEL = 2048
SEQ = 8192
DEPTH = 4
N_A_LAYERS = DEPTH // 2
HEAD_DIM = 128
D_FF = 4 * D_MODEL
LRU_WIDTH = 3 * D_MODEL // 4
LRU_BLOCK_WIDTH = 128
LRU_BLOCKS = LRU_WIDTH // LRU_BLOCK_WIDTH
CONV_WIDTH = 4
RG_C = 8.0
MEM_TOKENS = 256
MEM_HEADS = 4
MEM_WIDTH = MEM_HEADS * HEAD_DIM
DIL_PATTERNS = ((128, 1), (512, 4), (2048, 16))
DIL_GROUPS = len(DIL_PATTERNS)
DIL_HEADS = 4
DIL_WIDTH = DIL_GROUPS * DIL_HEADS * HEAD_DIM
DIL_OUT_WIDTH = DIL_HEADS * HEAD_DIM
Q_BLOCK = 128
REL_BUCKETS = 32
REL_MAX_EXACT = REL_BUCKETS // 2
REL_MAX_DISTANCE = 2048
A_IN_WIDTH = 2 * LRU_WIDTH + MEM_WIDTH
NORM_EPS = 1e-6
NEG_INF = -1e30
ATTN_SCALE = HEAD_DIM ** -0.5

V7X_VMEM_BYTES = 64 * 1024 * 1024
SCOPED_VMEM_MIB = 60
assert SCOPED_VMEM_MIB * 1024 * 1024 < V7X_VMEM_BYTES
LANES = 128
SUBLANES = 8
HEAD_BLOCK = DIL_HEADS * HEAD_DIM


def _params(semantics):
    return pltpu.CompilerParams(dimension_semantics=semantics,
                                vmem_limit_bytes=SCOPED_VMEM_MIB * 1024 * 1024)


def _rms_rows(x, g):
    ms = jnp.mean(x * x, axis=-1, keepdims=True)
    return x * lax.rsqrt(ms + NORM_EPS) * g


def _head_norm_cols(a, g, scale):
    parts = []
    for h in range(a.shape[1] // HEAD_DIM):
        ah = a[:, h * HEAD_DIM:(h + 1) * HEAD_DIM]
        y = ah * lax.rsqrt(jnp.mean(ah * ah, axis=-1, keepdims=True) + NORM_EPS) * g
        parts.append(y * scale if scale != 1.0 else y)
    return jnp.concatenate(parts, axis=-1)


def _proj_kernel(a_ref, *rest, groups, normed, scale, tm, has_norm, has_scr):
    rest = list(rest)
    gn_ref = rest.pop(0) if has_norm else None
    w_ref, hg_ref = rest.pop(0), rest.pop(0)
    o_refs = [rest.pop(0) for _ in groups]
    hn_s = rest.pop(0) if has_norm else None
    scr = rest.pop(0) if has_scr else None
    if has_norm:
        hn_s[...] = _rms_rows(a_ref[...], gn_ref[...]).astype(BF16)
        lhs_ref = hn_s
    else:
        lhs_ref = a_ref
    jb = 0
    for (nb, d, _), o_ref in zip(groups, o_refs):
        for k in range(nb):
            acc = jnp.dot(lhs_ref[...], w_ref[:, jb * HEAD_BLOCK:(jb + 1) * HEAD_BLOCK],
                          preferred_element_type=F32)
            if normed[jb]:
                acc = _head_norm_cols(acc, hg_ref[jb:jb + 1, :], scale)
            if d == 1:
                o_ref[0, :, k * HEAD_BLOCK:(k + 1) * HEAD_BLOCK] = acc.astype(o_ref.dtype)
            else:
                for h in range(HEAD_BLOCK // LANES):
                    scr[h] = acc[:, h * LANES:(h + 1) * LANES]
                for r in range(d):
                    for h in range(HEAD_BLOCK // LANES):
                        o_ref[r, :, h * LANES:(h + 1) * LANES] = (
                            scr[h, pl.ds(r, tm // d, stride=d), :].astype(o_ref.dtype))
            jb += 1


def _proj(a, norm_gain, w, head_gains, groups, normed, scale, name, tm):
    S, D = a.shape
    N = w.shape[1]
    nblk = N // HEAD_BLOCK
    assert sum(nb for nb, _, _ in groups) == len(normed) == nblk
    assert all(nb == 1 for nb, d, _ in groups if d > 1)
    has_norm = norm_gain is not None
    has_scr = any(d > 1 for _, d, _ in groups)
    kern = functools.partial(_proj_kernel, groups=groups, normed=normed, scale=scale, tm=tm,
                             has_norm=has_norm, has_scr=has_scr)
    in_specs = [pl.BlockSpec((tm, D), lambda i: (i, 0))]
    args = [a]
    if has_norm:
        in_specs.append(pl.BlockSpec((1, D), lambda i: (0, 0)))
        args.append(norm_gain.reshape(1, D))
    in_specs += [pl.BlockSpec((D, N), lambda i: (0, 0), pipeline_mode=pl.Buffered(1)),
                 pl.BlockSpec((nblk, HEAD_DIM), lambda i: (0, 0))]
    args += [w, head_gains]
    scratch = []
    if has_norm:
        scratch.append(pltpu.VMEM((tm, D), BF16))
    if has_scr:
        scratch.append(pltpu.VMEM((HEAD_BLOCK // LANES, tm, LANES), F32))
    return pl.pallas_call(
        kern,
        grid=(S // tm,),
        in_specs=in_specs,
        out_specs=[pl.BlockSpec((d, tm // d, nb * HEAD_BLOCK), lambda i: (0, i, 0)) for nb, d, _ in groups],
        out_shape=[jax.ShapeDtypeStruct((d, S // d, nb * HEAD_BLOCK), dt) for nb, d, dt in groups],
        scratch_shapes=scratch,
        compiler_params=_params(("parallel",)),
        name=name,
    )(*args)


def _mlp_kernel(x_ref, g_ref, w1_ref, w2_ref, *rest, n_casts):
    cast_in, o_ref, cast_out, hn_ref = rest[:n_casts], rest[n_casts], rest[n_casts + 1:-1], rest[-1]

    @pl.when(pl.program_id(1) == 0)
    def _():
        x = x_ref[...]
        hn_ref[...] = _rms_rows(x, g_ref[...]).astype(BF16)
        o_ref[...] = x

    h = jnp.dot(hn_ref[...], w1_ref[...], preferred_element_type=F32)
    h = jnp.maximum(h, 0.0)
    h = (h * h).astype(BF16)
    o_ref[...] += jnp.dot(h, w2_ref[...], preferred_element_type=F32)

    for src, dst in zip(cast_in, cast_out):
        dst[...] = src[...].astype(BF16)


def _mlp(x, g, w1b, w2b, casts, name, tm=1024, tf=512):
    S, D = x.shape
    F = w1b.shape[1]
    ni, nj = S // tm, F // tf
    row_spec = pl.BlockSpec((tm, D), lambda i, j: (i, 0))
    in_specs = [row_spec, pl.BlockSpec((1, D), lambda i, j: (0, 0)),
                pl.BlockSpec((D, tf), lambda i, j: (0, j)),
                pl.BlockSpec((tf, D), lambda i, j: (j, 0))]
    args = [x, g.reshape(1, D), w1b, w2b]
    out_specs = [row_spec]
    out_shape = [jax.ShapeDtypeStruct((S, D), F32)]
    for w, idx in casts:
        R, C = w.shape[-2:]
        r = R // (ni * nj)
        assert r * ni * nj == R and r % 16 == 0, (R, ni, nj)
        if idx is None:
            in_specs.append(pl.BlockSpec((r, C), lambda i, j: (i * nj + j, 0)))
        else:
            in_specs.append(pl.BlockSpec((None, r, C), lambda i, j, idx=idx: (idx, i * nj + j, 0)))
        args.append(w)
        out_specs.append(pl.BlockSpec((r, C), lambda i, j: (i * nj + j, 0)))
        out_shape.append(jax.ShapeDtypeStruct((R, C), BF16))
    kern = functools.partial(_mlp_kernel, n_casts=len(casts))
    outs = pl.pallas_call(
        kern,
        grid=(ni, nj),
        in_specs=in_specs,
        out_specs=out_specs,
        out_shape=out_shape,
        scratch_shapes=[pltpu.VMEM((tm, D), BF16)],
        compiler_params=_params(("parallel", "arbitrary")),
        name=name,
    )(*args)
    return outs[0], list(outs[1:])


def _mem_kv_kernel(mem_ref, g_ref, w_ref, kg_ref, k_ref, v_ref):
    hn = _rms_rows(mem_ref[...], g_ref[...]).astype(BF16)
    kv = jnp.dot(hn, w_ref[...].astype(BF16), preferred_element_type=F32)
    k_ref[...] = _head_norm_cols(kv[:, :MEM_WIDTH], kg_ref[...], 1.0).astype(BF16)
    v_ref[...] = kv[:, MEM_WIDTH:].astype(BF16)


def _mem_kv(mem, mem_norm_g, mem_w_kv, mem_k_norm_g):
    M, D = mem.shape
    out = jax.ShapeDtypeStruct((DEPTH, M, MEM_WIDTH), BF16)
    return pl.pallas_call(
        _mem_kv_kernel,
        grid=(DEPTH,),
        in_specs=[
            pl.BlockSpec((M, D), lambda l: (0, 0)),
            pl.BlockSpec((None, 1, D), lambda l: (l, 0, 0)),
            pl.BlockSpec((None, D, 2 * MEM_WIDTH), lambda l: (l, 0, 0)),
            pl.BlockSpec((None, 1, HEAD_DIM), lambda l: (l, 0, 0)),
        ],
        out_specs=[pl.BlockSpec((None, M, MEM_WIDTH), lambda l: (l, 0, 0)),
                   pl.BlockSpec((None, M, MEM_WIDTH), lambda l: (l, 0, 0))],
        out_shape=[out, out],
        compiler_params=_params(("arbitrary",)),
        name="mem_kv",
    )(mem, mem_norm_g.reshape(DEPTH, 1, D), mem_w_kv, mem_k_norm_g.reshape(DEPTH, 1, HEAD_DIM))


def _mem_attn_rows(q_ref, k_ref, v_ref):
    outs = []
    for h in range(MEM_HEADS):
        hs = slice(h * HEAD_DIM, (h + 1) * HEAD_DIM)
        s = lax.dot_general(q_ref[:, hs], k_ref[:, hs], (((1,), (1,)), ((), ())), preferred_element_type=F32)
        m = jnp.max(s, axis=-1, keepdims=True)
        p = jnp.exp(s - m)
        den = jnp.sum(p, axis=-1, keepdims=True)
        o = jnp.dot(p.astype(BF16), v_ref[:, hs], preferred_element_type=F32) / den
        outs.append(o.astype(BF16))
    return jnp.concatenate(outs, axis=-1)


def _mem_kv_specs(layer):
    spec = pl.BlockSpec((None, MEM_TOKENS, MEM_WIDTH), lambda i: (layer, 0, 0))
    return [spec, spec]


def _rglru_tile(u_ref, gate_ref, cw_ref, cb_ref, wg_ref, bg_ref, lam_ref, y_ref, us, hs, utail, hcar,
                *, T, LB, pitch, before_block=lambda c: None):
    nseg = SUBLANES
    seg = T // nseg
    lam = lam_ref[...]
    sp = jnp.maximum(-lam, 0.0) + jnp.log1p(jnp.exp(-jnp.abs(lam)))
    first_seg = lax.broadcasted_iota(jnp.int32, (nseg, LANES), 0) == 0

    for c in range(LB // LANES):
        before_block(c)
        cs = slice(c * LANES, (c + 1) * LANES)
        for s in range(nseg):
            us[c, s * pitch:s * pitch + seg, :] = u_ref[s * seg:(s + 1) * seg, cs]
        U = [us[c, pl.ds(p, nseg, stride=pitch), :] for p in range(seg)]

        before = {}
        for k in range(1, CONV_WIDTH):
            cur = pltpu.roll(U[seg - k], 1, axis=0)
            prev = pltpu.roll(utail[k - 1, :, cs], 1, axis=0)
            before[-k] = jnp.where(first_seg, prev, cur)
        for k in range(1, CONV_WIDTH):
            utail[k - 1, :, cs] = U[seg - k]
        tap = lambda p: U[p] if p >= 0 else before[p]

        cwb = [jnp.broadcast_to(cw_ref[k:k + 1, cs], (nseg, LANES)) for k in range(CONV_WIDTH)]
        cbb = jnp.broadcast_to(cb_ref[:, cs], (nseg, LANES))
        xc = jnp.stack([cbb + (tap(p - 3) * cwb[0] + tap(p - 2) * cwb[1] + tap(p - 1) * cwb[2] + U[p] * cwb[3])
                        for p in range(seg)])

        z = jnp.dot(xc.reshape(T, LANES).astype(BF16), wg_ref[c], preferred_element_type=F32) + bg_ref[c]
        r = jax.nn.sigmoid(z[:, :LANES]).reshape(seg, nseg, LANES)
        ig = jax.nn.sigmoid(z[:, LANES:]).reshape(seg, nseg, LANES)
        log_a = (-RG_C * r) * sp[:, cs]
        a = jnp.exp(log_a)
        om = 1.0 - a * a
        root = jnp.where(om > 0.0, om * lax.rsqrt(om), 0.0)
        b = root * (ig * xc)

        hl, pr = [b[0]], [a[0]]
        for p in range(1, seg):
            hl.append(a[p] * hl[-1] + b[p])
            pr.append(a[p] * pr[-1])

        h_in = [hcar[:, cs]]
        for s in range(nseg):
            h_in.append(hl[-1][s:s + 1, :] + pr[-1][s:s + 1, :] * h_in[-1])
        hcar[:, cs] = h_in[nseg]
        h_in = jnp.concatenate(h_in[:nseg], axis=0)

        for p in range(seg):
            hs[c, pl.ds(p, nseg, stride=pitch), :] = hl[p] + pr[p] * h_in

        for s in range(nseg):
            rows = slice(s * seg, (s + 1) * seg)
            gt = gate_ref[rows, cs]
            gelu = 0.5 * gt * (1.0 + jnp.tanh(math.sqrt(2.0 / math.pi) * (gt + 0.044715 * (gt * gt * gt))))
            y_ref[rows, cs] = (hs[c, s * pitch:s * pitch + seg, :] * gelu).astype(y_ref.dtype)


def _a_mix_kernel(*refs, T, pitch, has_cast):
    refs = list(refs)
    x_ref, gn_ref, w_ref, hg_ref, cw_ref, cb_ref, wg_ref, bg_ref, lam_ref = refs[:9]
    del refs[:9]
    cast_src = refs.pop(0) if has_cast else None
    y_ref, mq_ref = refs.pop(0), refs.pop(0)
    cast_dst = refs.pop(0) if has_cast else None
    hn_s, u_cur, g_cur, u_new, g_new, us, hs, utail, hcar = refs
    if has_cast:
        cast_dst[...] = cast_src[...].astype(BF16)
    i = pl.program_id(0)

    @pl.when(i <= 1)
    def _():
        utail[...] = jnp.zeros_like(utail)
        hcar[...] = jnp.zeros_like(hcar)

    @pl.when(i == 0)
    def _():
        u_cur[...] = jnp.zeros_like(u_cur)
        g_cur[...] = jnp.zeros_like(g_cur)

    hn_s[...] = _rms_rows(x_ref[...], gn_ref[...]).astype(BF16)
    nb = LRU_WIDTH // HEAD_BLOCK

    def project(jb):
        acc = jnp.dot(hn_s[...], w_ref[:, jb * HEAD_BLOCK:(jb + 1) * HEAD_BLOCK], preferred_element_type=F32)
        if jb < nb:
            u_new[:, jb * HEAD_BLOCK:(jb + 1) * HEAD_BLOCK] = acc
        elif jb < 2 * nb:
            g_new[:, (jb - nb) * HEAD_BLOCK:(jb - nb + 1) * HEAD_BLOCK] = acc
        else:
            mq_ref[...] = _head_norm_cols(acc, hg_ref[...], ATTN_SCALE).astype(mq_ref.dtype)

    n_proj = 2 * nb + 1
    lane_blocks = LRU_WIDTH // LANES
    at_block = {(jb * lane_blocks) // n_proj: jb for jb in range(n_proj)}
    assert len(at_block) == n_proj

    def before_block(c):
        if c in at_block:
            project(at_block[c])

    _rglru_tile(u_cur, g_cur, cw_ref, cb_ref, wg_ref, bg_ref, lam_ref, y_ref, us, hs, utail, hcar,
                T=T, LB=LRU_WIDTH, pitch=pitch, before_block=before_block)

    u_cur[...] = u_new[...]
    g_cur[...] = g_new[...]


def _side_cast_specs(cast, n_steps):
    w, idx = cast
    R, C = w.shape[-2:]
    r = R // n_steps
    assert r * n_steps == R and r % 16 == 0
    step = lambda i: jnp.minimum(i, n_steps - 1)
    return (pl.BlockSpec((None, r, C), lambda i: (idx, step(i), 0)),
            pl.BlockSpec((r, C), lambda i: (step(i), 0)),
            jax.ShapeDtypeStruct((R, C), BF16))


def _a_mix(x, norm_gain, w_in, mq_gain, conv_w, conv_b, wg, bg, lam, name, T=512, cast=None):
    S, D = x.shape
    W = LRU_WIDTH
    nt = S // T
    seg = T // SUBLANES
    pitch = seg + SUBLANES if (seg // SUBLANES) % 2 == 0 else seg
    slab = pltpu.VMEM((W // LANES, SUBLANES * pitch, LANES), F32)
    const = lambda shape: pl.BlockSpec(shape, lambda i: (0,) * len(shape))
    cur = lambda i: jnp.minimum(i, nt - 1)
    prv = lambda i: jnp.maximum(i - 1, 0)
    kern = functools.partial(_a_mix_kernel, T=T, pitch=pitch, has_cast=cast is not None)
    in_specs = [
        pl.BlockSpec((T, D), lambda i: (cur(i), 0)),
        const((1, D)),
        pl.BlockSpec((D, A_IN_WIDTH), lambda i: (0, 0), pipeline_mode=pl.Buffered(1)),
        const((1, HEAD_DIM)),
        const((CONV_WIDTH, W)),
        const((1, W)),
        const((LRU_BLOCKS, LRU_BLOCK_WIDTH, 2 * LRU_BLOCK_WIDTH)),
        const((LRU_BLOCKS, 1, 2 * LRU_BLOCK_WIDTH)),
        const((1, W)),
    ]
    args = [x, norm_gain.reshape(1, D), w_in, mq_gain, conv_w, conv_b.reshape(1, W), wg, bg, lam.reshape(1, W)]
    out_specs = [pl.BlockSpec((T, W), lambda i: (prv(i), 0)),
                 pl.BlockSpec((T, MEM_WIDTH), lambda i: (cur(i), 0))]
    out_shape = [jax.ShapeDtypeStruct((S, W), BF16), jax.ShapeDtypeStruct((S, MEM_WIDTH), BF16)]
    if cast is not None:
        c_in, c_out, c_shape = _side_cast_specs(cast, nt)
        in_specs.append(c_in)
        args.append(cast[0])
        out_specs.append(c_out)
        out_shape.append(c_shape)
    return pl.pallas_call(
        kern,
        grid=(nt + 1,),
        in_specs=in_specs,
        out_specs=out_specs,
        out_shape=out_shape,
        scratch_shapes=[pltpu.VMEM((T, D), BF16)] + [pltpu.VMEM((T, W), F32)] * 4 + [
                        slab, slab, pltpu.VMEM((CONV_WIDTH - 1, SUBLANES, W), F32), pltpu.VMEM((1, W), F32)],
        compiler_params=_params(("arbitrary",)),
        name=name,
    )(*args)


def _cast_weight_once(w_ref, wb_ref):
    @pl.when(pl.program_id(0) == 0)
    def _():
        wb_ref[...] = w_ref[...].astype(BF16)


def _out_a_kernel(x_ref, y_ref, mq_ref, mk_ref, mv_ref, w_ref, *rest):
    acc = jnp.dot(y_ref[...], w_ref[0:LRU_WIDTH, :], preferred_element_type=F32)
    mem_out = _mem_attn_rows(mq_ref, mk_ref, mv_ref)
    acc += jnp.dot(mem_out, w_ref[LRU_WIDTH:LRU_WIDTH + MEM_WIDTH, :], preferred_element_type=F32)
    if len(rest) == 1:
        (o_ref,) = rest
    else:
        cast_src, o_ref, cast_dst = rest
        cast_dst[...] = cast_src[...].astype(BF16)
    o_ref[...] = x_ref[...] + acc


def _out_a(x, y, mq, mem_k, mem_v, layer, w, name, tm=1024, cast=None):
    S, D = x.shape
    K = LRU_WIDTH + MEM_WIDTH
    row = pl.BlockSpec((tm, D), lambda i: (i, 0))
    in_specs = [
        row,
        pl.BlockSpec((tm, LRU_WIDTH), lambda i: (i, 0)),
        pl.BlockSpec((tm, MEM_WIDTH), lambda i: (i, 0)),
        *_mem_kv_specs(layer),
        pl.BlockSpec((K, D), lambda i: (0, 0), pipeline_mode=pl.Buffered(1)),
    ]
    args = [x, y, mq, mem_k, mem_v, w]
    out_specs, out_shape = [row], [jax.ShapeDtypeStruct((S, D), F32)]
    if cast is not None:
        c_in, c_out, c_shape = _side_cast_specs(cast, S // tm)
        in_specs.append(c_in)
        args.append(cast[0])
        out_specs.append(c_out)
        out_shape.append(c_shape)
    outs = pl.pallas_call(
        _out_a_kernel,
        grid=(S // tm,),
        in_specs=in_specs,
        out_specs=out_specs,
        out_shape=out_shape,
        compiler_params=_params(("parallel",)),
        name=name,
    )(*args)
    return outs if cast is not None else outs[0]


def _bucket_tables():
    qi = np.arange(Q_BLOCK)[:, None]
    ki = np.arange(2 * Q_BLOCK)[None, :]
    u = qi + Q_BLOCK - ki
    tables, valids = [], []
    for window, dilation in DIL_PATTERNS:
        n = np.maximum(u * dilation, 0)
        nf = np.maximum(n, 1).astype(np.float32)
        large = REL_MAX_EXACT + (np.log(nf / np.float32(REL_MAX_EXACT))
                                 / np.float32(math.log(REL_MAX_DISTANCE / REL_MAX_EXACT))
                                 * np.float32(REL_BUCKETS - REL_MAX_EXACT)).astype(np.int32)
        large = np.minimum(large, REL_BUCKETS - 1)
        tables.append(np.where(n < REL_MAX_EXACT, n, large).astype(np.int32))
        valids.append(((u >= 0) & (u <= window // dilation)).astype(np.int32))
    return np.stack(tables), np.stack(valids)


def _bias_kernel(tab_ref, idx_ref, valid_ref, o_ref):
    g = pl.program_id(0)
    idx = idx_ref[...]
    valid = valid_ref[...] > 0
    for h in range(DIL_HEADS):
        acc = jnp.zeros(idx.shape, F32)
        for b in range(REL_BUCKETS):
            acc = jnp.where(idx == b, tab_ref[b, g * DIL_HEADS + h], acc)
        o_ref[h] = jnp.where(valid, acc, NEG_INF)


def _bias_expand(rel_bias):
    idx, valid = _bucket_tables()
    blk = pl.BlockSpec((None, Q_BLOCK, 2 * Q_BLOCK), lambda g: (g, 0, 0))
    return pl.pallas_call(
        _bias_kernel,
        grid=(DIL_GROUPS,),
        in_specs=[pl.BlockSpec(memory_space=pltpu.SMEM), blk, blk],
        out_specs=pl.BlockSpec((DIL_HEADS, Q_BLOCK, 2 * Q_BLOCK), lambda g: (g, 0, 0)),
        out_shape=jax.ShapeDtypeStruct((DIL_GROUPS * DIL_HEADS, Q_BLOCK, 2 * Q_BLOCK), F32),
        compiler_params=_params(("arbitrary",)),
        name="bias_expand",
    )(rel_bias, jnp.asarray(idx), jnp.asarray(valid))


def _dil_kernel(q_ref, kc_ref, kp_ref, vc_ref, vp_ref, b_ref, o_ref, l_ref, *, LQ):
    is_first = pl.program_id(1) == 0
    for sb in range(LQ // Q_BLOCK):
        rows = slice(sb * Q_BLOCK, (sb + 1) * Q_BLOCK)
        for h in range(DIL_HEADS):
            hs = slice(h * HEAD_DIM, (h + 1) * HEAD_DIM)
            q = q_ref[rows, hs]
            if sb == 0:
                kk = jnp.concatenate([kp_ref[:, hs], kc_ref[0:Q_BLOCK, hs]], axis=0)
                vv = jnp.concatenate([vp_ref[:, hs], vc_ref[0:Q_BLOCK, hs]], axis=0)
            else:
                kk = kc_ref[(sb - 1) * Q_BLOCK:(sb + 1) * Q_BLOCK, hs]
                vv = vc_ref[(sb - 1) * Q_BLOCK:(sb + 1) * Q_BLOCK, hs]
            s = lax.dot_general(q, kk, (((1,), (1,)), ((), ())), preferred_element_type=F32) + b_ref[h]
            if sb == 0:
                col = lax.broadcasted_iota(jnp.int32, s.shape, 1)
                s = jnp.where(jnp.logical_and(is_first, col < Q_BLOCK), NEG_INF, s)
            m = jnp.max(s, axis=-1, keepdims=True)
            p = jnp.exp(s - m)
            den = jnp.sum(p, axis=-1, keepdims=True)
            o = jnp.dot(p.astype(BF16), vv, preferred_element_type=F32) / den
            o_ref[rows, hs] = o.astype(o_ref.dtype)
            l_ref[rows, hs] = jnp.broadcast_to(m + jnp.log(den), (Q_BLOCK, HEAD_DIM))


def _dil_attn(q, k, v, bias, g, name, max_lq=1024):
    d, L, _ = q.shape
    assert d == DIL_PATTERNS[g][1]
    LQ = min(max_lq, L)
    ratio = LQ // Q_BLOCK
    kern = functools.partial(_dil_kernel, LQ=LQ)
    cur = pl.BlockSpec((None, LQ, HEAD_BLOCK), lambda r, n: (r, n, 0))
    prev = pl.BlockSpec((None, Q_BLOCK, HEAD_BLOCK), lambda r, n: (r, jnp.maximum(n * ratio - 1, 0), 0))
    return pl.pallas_call(
        kern,
        grid=(d, L // LQ),
        in_specs=[cur, cur, prev, cur, prev,
                  pl.BlockSpec((DIL_HEADS, Q_BLOCK, 2 * Q_BLOCK), lambda r, n: (g, 0, 0))],
        out_specs=[pl.BlockSpec((None, LQ, DIL_OUT_WIDTH), lambda r, n: (r, n, 0)),
                   pl.BlockSpec((None, LQ, DIL_OUT_WIDTH), lambda r, n: (r, n, 0))],
        out_shape=[jax.ShapeDtypeStruct((d, L, DIL_OUT_WIDTH), BF16),
                   jax.ShapeDtypeStruct((d, L, DIL_OUT_WIDTH), F32)],
        compiler_params=_params(("arbitrary", "arbitrary")),
        name=name,
    )(q, k, k, v, v, bias)


def _out_b_kernel(x_ref, o0_ref, l0_ref, o1_ref, l1_ref, o2_ref, l2_ref, mq_ref, mk_ref, mv_ref, w_ref, o_ref,
                  wb_ref, s1_ref, s2_ref, *, tm):
    _cast_weight_once(w_ref, wb_ref)
    mem_out = _mem_attn_rows(mq_ref, mk_ref, mv_ref)
    acc = jnp.dot(mem_out, wb_ref[DIL_OUT_WIDTH:DIL_OUT_WIDTH + MEM_WIDTH, :], preferred_element_type=F32)
    for og_ref, lg_ref, s_ref in ((o1_ref, l1_ref, s1_ref), (o2_ref, l2_ref, s2_ref)):
        d = og_ref.shape[0]
        for r in range(d):
            for c in range(DIL_HEADS):
                cl = slice(c * LANES, (c + 1) * LANES)
                s_ref[c, pl.ds(r, tm // d, stride=d), :] = og_ref[r, :, cl].astype(F32)
                s_ref[DIL_HEADS + c, pl.ds(r, tm // d, stride=d), :] = lg_ref[r, :, cl]
    parts = []
    for h in range(DIL_HEADS):
        hs = slice(h * HEAD_DIM, (h + 1) * HEAD_DIM)
        lses = [l0_ref[0, :, hs], s1_ref[DIL_HEADS + h], s2_ref[DIL_HEADS + h]]
        outs = [o0_ref[0, :, hs].astype(F32), s1_ref[h], s2_ref[h]]
        mx = jnp.maximum(jnp.maximum(lses[0], lses[1]), lses[2])
        es = [jnp.exp(l - mx) for l in lses]
        den = es[0] + es[1] + es[2]
        num = es[0] * outs[0] + es[1] * outs[1] + es[2] * outs[2]
        parts.append((num / den).astype(BF16))
    dil = jnp.concatenate(parts, axis=-1)
    acc += jnp.dot(dil, wb_ref[0:DIL_OUT_WIDTH, :], preferred_element_type=F32)
    o_ref[...] = x_ref[...] + acc


def _out_b(x, ogs, mq, mem_k, mem_v, layer, w, widx, name, tm=512):
    S, D = x.shape
    K = DIL_OUT_WIDTH + MEM_WIDTH
    og_specs, og_args = [], []
    for o, l in ogs:
        d = o.shape[0]
        og_specs += [pl.BlockSpec((d, tm // d, DIL_OUT_WIDTH), lambda i: (0, i, 0)),
                     pl.BlockSpec((d, tm // d, DIL_OUT_WIDTH), lambda i: (0, i, 0))]
        og_args += [o, l]
    slab = pltpu.VMEM((2 * DIL_HEADS, tm, LANES), F32)
    row = pl.BlockSpec((tm, D), lambda i: (i, 0))
    return pl.pallas_call(
        functools.partial(_out_b_kernel, tm=tm),
        grid=(S // tm,),
        in_specs=[
            row,
            *og_specs,
            pl.BlockSpec((tm, MEM_WIDTH), lambda i: (i, 0)),
            *_mem_kv_specs(layer),
            pl.BlockSpec((None, K, D), lambda i: (widx, 0, 0), pipeline_mode=pl.Buffered(1)),
        ],
        out_specs=row,
        out_shape=jax.ShapeDtypeStruct((S, D), F32),
        scratch_shapes=[pltpu.VMEM((K, D), BF16), slab, slab],
        compiler_params=_params(("arbitrary",)),
        name=name,
    )(x, *og_args, mq, mem_k, mem_v, w)


def kernel(x, mem, norm_mix_g, norm_mlp_g, mlp_w1, mlp_w2, mem_norm_g, mem_w_kv, mem_q_norm_g, mem_k_norm_g, a_w_in, a_conv_w, a_conv_b, a_gate_r_w, a_gate_r_b, a_gate_i_w, a_gate_i_b, a_lambda, a_w_out, kv_norm_g, kv_w, k_norm_g, rel_bias, b_w_q, b_q_norm_g, b_w_out):
    B, S, D = x.shape
    assert (B, S, D) == (1, SEQ, D_MODEL) and mem.shape == (1, MEM_TOKENS, D_MODEL)
    xs = x[0]
    ones = jnp.ones((1, HEAD_DIM), F32)

    mem_k, mem_v = _mem_kv(mem[0], mem_norm_g, mem_w_kv, mem_k_norm_g)
    bias = _bias_expand(rel_bias)
    w1b = w2b = None
    w_in_b, w_out_b = a_w_in[0].astype(BF16), a_w_out[0].astype(BF16)
    later = {}

    for layer in range(DEPTH):
        mq_gain = mem_q_norm_g[layer][None, :]
        a_in, a_gain = xs, norm_mix_g[layer]
        if layer < N_A_LAYERS:
            i = layer
            wg = jnp.concatenate([a_gate_r_w[i], a_gate_i_w[i]], axis=-1).astype(BF16)
            bg = jnp.concatenate([a_gate_r_b[i], a_gate_i_b[i]], axis=-1)[:, None, :]
            if layer == 0:
                y, mq, w1b = _a_mix(a_in, a_gain, w_in_b, mq_gain, a_conv_w[i], a_conv_b[i], wg, bg,
                                    a_lambda[i], f"a{i}_mix", cast=(mlp_w1, 0))
                xs, w2b = _out_a(xs, y, mq, mem_k, mem_v, layer, w_out_b, f"a{i}_out_proj", tm=512,
                                 cast=(mlp_w2, 0))
            else:
                y, mq = _a_mix(a_in, a_gain, w_in_b, mq_gain, a_conv_w[i], a_conv_b[i], wg, bg,
                               a_lambda[i], f"a{i}_mix")
                xs = _out_a(xs, y, mq, mem_k, mem_v, layer, w_out_b, f"a{i}_out_proj")
        else:
            j = layer - N_A_LAYERS
            dils = tuple(d for _, d in DIL_PATTERNS)
            dil_groups = tuple((1, d, BF16) for d in dils)
            if j == 0:
                kv_gains = jnp.concatenate([k_norm_g, jnp.tile(ones, (DIL_GROUPS, 1))], axis=0)
                kvs = _proj(xs, kv_norm_g, later["kv_w"], kv_gains, dil_groups + dil_groups,
                            (True,) * DIL_GROUPS + (False,) * DIL_GROUPS, 1.0, "shared_kv", tm=1024)
            q_gains = jnp.concatenate([b_q_norm_g[j], mq_gain], axis=0)
            qs = _proj(a_in, a_gain, later["w_q"], q_gains, dil_groups + ((1, 1, BF16),),
                       (True,) * (DIL_GROUPS + 1), ATTN_SCALE, f"b{j}_q_proj", tm=1024)
            ogs = [_dil_attn(qs[g], kvs[g], kvs[DIL_GROUPS + g], bias, g, f"b{j}_dil{g}")
                   for g in range(DIL_GROUPS)]
            xs = _out_b(xs, ogs, qs[DIL_GROUPS].reshape(S, MEM_WIDTH), mem_k, mem_v, layer, b_w_out, j,
                        f"b{j}_out_proj")

        nl = layer + 1
        casts, names = [], []
        if nl < DEPTH:
            casts += [(mlp_w1, nl), (mlp_w2, nl)]
            names += ["w1", "w2"]
            if nl < N_A_LAYERS:
                casts += [(a_w_in, nl), (a_w_out, nl)]
                names += ["w_in", "w_out"]
            else:
                casts.append((b_w_q, nl - N_A_LAYERS))
                names.append("w_q")
                if nl == N_A_LAYERS:
                    casts.append((kv_w, None))
                    names.append("kv_w")
        xs, cast_out = _mlp(xs, norm_mlp_g[layer], w1b, w2b, casts, f"mlp{layer}")
        later = dict(zip(names, cast_out))
        w1b, w2b, w_in_b, w_out_b = (later.get(n) for n in ("w1", "w2", "w_in", "w_out"))
    return xs[None]
```

```python
import functools
import math

import numpy as np
import jax
import jax.numpy as jnp
from jax import lax
from jax.experimental import pallas as pl
from jax.experimental.pallas import tpu as pltpu

F32 = jnp.float32
BF16 = jnp.bfloat16

D_MODEL = 2048
SEQ = 8192
DEPTH = 4
N_A_LAYERS = DEPTH // 2
HEAD_DIM = 128
LRU_WIDTH = 3 * D_MODEL // 4
LRU_BLOCK_WIDTH = 128
LRU_BLOCKS = LRU_WIDTH // LRU_BLOCK_WIDTH
CONV_WIDTH = 4
RG_C = 8.0
MEM_TOKENS = 256
MEM_HEADS = 4
MEM_WIDTH = MEM_HEADS * HEAD_DIM
DIL_PATTERNS = ((128, 1), (512, 4), (2048, 16))
DIL_GROUPS = len(DIL_PATTERNS)
DIL_HEADS = 4
DIL_OUT_WIDTH = DIL_HEADS * HEAD_DIM
Q_BLOCK = 128
REL_BUCKETS = 32
REL_MAX_EXACT = REL_BUCKETS // 2
REL_MAX_DISTANCE = 2048
A_IN_WIDTH = 2 * LRU_WIDTH + MEM_WIDTH
NORM_EPS = 1e-6
NEG_INF = -1e30
ATTN_SCALE = HEAD_DIM ** -0.5

V7X_VMEM_BYTES = 64 * 1024 * 1024
SCOPED_VMEM_MIB = 60
assert SCOPED_VMEM_MIB * 1024 * 1024 < V7X_VMEM_BYTES
LANES = 128
SUBLANES = 8
HEAD_BLOCK = DIL_HEADS * HEAD_DIM


def _params(semantics):
    return pltpu.CompilerParams(dimension_semantics=semantics,
                                vmem_limit_bytes=SCOPED_VMEM_MIB * 1024 * 1024)


def _rms_rows(x, g):
    ms = jnp.mean(x * x, axis=-1, keepdims=True)
    return x * lax.rsqrt(ms + NORM_EPS) * g


def _head_norm_cols(a, g, scale):
    parts = []
    for h in range(a.shape[1] // HEAD_DIM):
        ah = a[:, h * HEAD_DIM:(h + 1) * HEAD_DIM]
        y = ah * lax.rsqrt(jnp.mean(ah * ah, axis=-1, keepdims=True) + NORM_EPS) * g
        parts.append(y * scale if scale != 1.0 else y)
    return jnp.concatenate(parts, axis=-1)


def _proj_kernel(a_ref, *rest, groups, normed, scale, tm, has_norm, has_scr):
    rest = list(rest)
    gn_ref = rest.pop(0) if has_norm else None
    w_ref, hg_ref = rest.pop(0), rest.pop(0)
    o_refs = [rest.pop(0) for _ in groups]
    hn_s = rest.pop(0) if has_norm else None
    scr = rest.pop(0) if has_scr else None
    if has_norm:
        hn_s[...] = _rms_rows(a_ref[...], gn_ref[...]).astype(BF16)
        lhs_ref = hn_s
    else:
        lhs_ref = a_ref
    jb = 0
    for (nb, d, _), o_ref in zip(groups, o_refs):
        for k in range(nb):
            acc = jnp.dot(lhs_ref[...], w_ref[:, jb * HEAD_BLOCK:(jb + 1) * HEAD_BLOCK],
                          preferred_element_type=F32)
            if normed[jb]:
                acc = _head_norm_cols(acc, hg_ref[jb:jb + 1, :], scale)
            if d == 1:
                o_ref[0, :, k * HEAD_BLOCK:(k + 1) * HEAD_BLOCK] = acc.astype(o_ref.dtype)
            else:
                for h in range(HEAD_BLOCK // LANES):
                    scr[h] = acc[:, h * LANES:(h + 1) * LANES]
                for r in range(d):
                    for h in range(HEAD_BLOCK // LANES):
                        o_ref[r, :, h * LANES:(h + 1) * LANES] = (
                            scr[h, pl.ds(r, tm // d, stride=d), :].astype(o_ref.dtype))
            jb += 1


def _proj(a, norm_gain, w, head_gains, groups, normed, scale, name, tm):
    S, D = a.shape
    N = w.shape[1]
    nblk = N // HEAD_BLOCK
    assert sum(nb for nb, _, _ in groups) == len(normed) == nblk
    assert all(nb == 1 for nb, d, _ in groups if d > 1)
    has_norm = norm_gain is not None
    has_scr = any(d > 1 for _, d, _ in groups)
    kern = functools.partial(_proj_kernel, groups=groups, normed=normed, scale=scale, tm=tm,
                             has_norm=has_norm, has_scr=has_scr)
    in_specs = [pl.BlockSpec((tm, D), lambda i: (i, 0))]
    args = [a]
    if has_norm:
        in_specs.append(pl.BlockSpec((1, D), lambda i: (0, 0)))
        args.append(norm_gain.reshape(1, D))
    in_specs += [pl.BlockSpec((D, N), lambda i: (0, 0), pipeline_mode=pl.Buffered(1)),
                 pl.BlockSpec((nblk, HEAD_DIM), lambda i: (0, 0))]
    args += [w, head_gains]
    scratch = []
    if has_norm:
        scratch.append(pltpu.VMEM((tm, D), BF16))
    if has_scr:
        scratch.append(pltpu.VMEM((HEAD_BLOCK // LANES, tm, LANES), F32))
    return pl.pallas_call(
        kern,
        grid=(S // tm,),
        in_specs=in_specs,
        out_specs=[pl.BlockSpec((d, tm // d, nb * HEAD_BLOCK), lambda i: (0, i, 0)) for nb, d, _ in groups],
        out_shape=[jax.ShapeDtypeStruct((d, S // d, nb * HEAD_BLOCK), dt) for nb, d, dt in groups],
        scratch_shapes=scratch,
        compiler_params=_params(("parallel",)),
        name=name,
    )(*args)


def _mlp_kernel(x_ref, g_ref, w1_ref, w2_ref, *rest, n_casts):
    cast_in, o_ref, cast_out, hn_ref = rest[:n_casts], rest[n_casts], rest[n_casts + 1:-1], rest[-1]

    @pl.when(pl.program_id(1) == 0)
    def _():
        x = x_ref[...]
        hn_ref[...] = _rms_rows(x, g_ref[...]).astype(BF16)
        o_ref[...] = x

    h = jnp.dot(hn_ref[...], w1_ref[...], preferred_element_type=F32)
    h = jnp.maximum(h, 0.0)
    h = (h * h).astype(BF16)
    o_ref[...] += jnp.dot(h, w2_ref[...], preferred_element_type=F32)

    for src, dst in zip(cast_in, cast_out):
        dst[...] = src[...].astype(BF16)


def _mlp(x, g, w1b, w2b, casts, name, tm=1024, tf=512):
    S, D = x.shape
    F = w1b.shape[1]
    ni, nj = S // tm, F // tf
    row_spec = pl.BlockSpec((tm, D), lambda i, j: (i, 0))
    in_specs = [row_spec, pl.BlockSpec((1, D), lambda i, j: (0, 0)),
                pl.BlockSpec((D, tf), lambda i, j: (0, j)),
                pl.BlockSpec((tf, D), lambda i, j: (j, 0))]
    args = [x, g.reshape(1, D), w1b, w2b]
    out_specs = [row_spec]
    out_shape = [jax.ShapeDtypeStruct((S, D), F32)]
    for w, idx in casts:
        R, C = w.shape[-2:]
        r = R // (ni * nj)
        assert r * ni * nj == R and r % 16 == 0, (R, ni, nj)
        if idx is None:
            in_specs.append(pl.BlockSpec((r, C), lambda i, j: (i * nj + j, 0)))
        else:
            in_specs.append(pl.BlockSpec((None, r, C), lambda i, j, idx=idx: (idx, i * nj + j, 0)))
        args.append(w)
        out_specs.append(pl.BlockSpec((r, C), lambda i, j: (i * nj + j, 0)))
        out_shape.append(jax.ShapeDtypeStruct((R, C), BF16))
    kern = functools.partial(_mlp_kernel, n_casts=len(casts))
    outs = pl.pallas_call(
        kern,
        grid=(ni, nj),
        in_specs=in_specs,
        out_specs=out_specs,
        out_shape=out_shape,
        scratch_shapes=[pltpu.VMEM((tm, D), BF16)],
        compiler_params=_params(("parallel", "arbitrary")),
        name=name,
    )(*args)
    return outs[0], list(outs[1:])


def _mem_kv_kernel(mem_ref, g_ref, w_ref, kg_ref, *rest, n_casts):
    cast_in, (k_ref, v_ref), cast_out = rest[:n_casts], rest[n_casts:n_casts + 2], rest[n_casts + 2:]
    hn = _rms_rows(mem_ref[...], g_ref[...]).astype(BF16)
    kv = jnp.dot(hn, w_ref[...].astype(BF16), preferred_element_type=F32)
    k_ref[...] = _head_norm_cols(kv[:, :MEM_WIDTH], kg_ref[...], 1.0).astype(BF16)
    v_ref[...] = kv[:, MEM_WIDTH:].astype(BF16)
    for src, dst in zip(cast_in, cast_out):
        dst[...] = src[...].astype(BF16)


def _mem_kv(mem, mem_norm_g, mem_w_kv, mem_k_norm_g, casts):
    M, D = mem.shape
    out = jax.ShapeDtypeStruct((DEPTH, M, MEM_WIDTH), BF16)
    in_specs = [
        pl.BlockSpec((M, D), lambda l: (0, 0)),
        pl.BlockSpec((None, 1, D), lambda l: (l, 0, 0)),
        pl.BlockSpec((None, D, 2 * MEM_WIDTH), lambda l: (l, 0, 0)),
        pl.BlockSpec((None, 1, HEAD_DIM), lambda l: (l, 0, 0)),
    ]
    args = [mem, mem_norm_g.reshape(DEPTH, 1, D), mem_w_kv, mem_k_norm_g.reshape(DEPTH, 1, HEAD_DIM)]
    out_specs = [pl.BlockSpec((None, M, MEM_WIDTH), lambda l: (l, 0, 0)),
                 pl.BlockSpec((None, M, MEM_WIDTH), lambda l: (l, 0, 0))]
    out_shape = [out, out]
    for cast in casts:
        c_in, c_out, c_shape = _side_cast_specs(cast, DEPTH)
        in_specs.append(c_in)
        args.append(cast[0])
        out_specs.append(c_out)
        out_shape.append(c_shape)
    outs = pl.pallas_call(
        functools.partial(_mem_kv_kernel, n_casts=len(casts)),
        grid=(DEPTH,),
        in_specs=in_specs,
        out_specs=out_specs,
        out_shape=out_shape,
        compiler_params=_params(("arbitrary",)),
        name="mem_kv",
    )(*args)
    return outs[0], outs[1], list(outs[2:])


def _mem_attn_rows(q_ref, k_ref, v_ref):
    outs = []
    for h in range(MEM_HEADS):
        hs = slice(h * HEAD_DIM, (h + 1) * HEAD_DIM)
        s = lax.dot_general(q_ref[:, hs], k_ref[:, hs], (((1,), (1,)), ((), ())), preferred_element_type=F32)
        m = jnp.max(s, axis=-1, keepdims=True)
        p = jnp.exp(s - m)
        den = jnp.sum(p, axis=-1, keepdims=True)
        o = jnp.dot(p.astype(BF16), v_ref[:, hs], preferred_element_type=F32) / den
        outs.append(o.astype(BF16))
    return jnp.concatenate(outs, axis=-1)


def _mem_kv_specs(layer):
    spec = pl.BlockSpec((None, MEM_TOKENS, MEM_WIDTH), lambda i: (layer, 0, 0))
    return [spec, spec]


def _rglru_tile(u_ref, gate_ref, cw_ref, cb_ref, wg_ref, bg_ref, lam_ref, y_ref, us, hs, utail, hcar,
                *, T, LB, pitch, before_block=lambda c: None):
    nseg = SUBLANES
    seg = T // nseg
    lam = lam_ref[...]
    sp = jnp.maximum(-lam, 0.0) + jnp.log1p(jnp.exp(-jnp.abs(lam)))
    first_seg = lax.broadcasted_iota(jnp.int32, (nseg, LANES), 0) == 0

    for c in range(LB // LANES):
        before_block(c)
        cs = slice(c * LANES, (c + 1) * LANES)
        for s in range(nseg):
            us[c, s * pitch:s * pitch + seg, :] = u_ref[s * seg:(s + 1) * seg, cs]
        U = [us[c, pl.ds(p, nseg, stride=pitch), :] for p in range(seg)]

        before = {}
        for k in range(1, CONV_WIDTH):
            cur = pltpu.roll(U[seg - k], 1, axis=0)
            prev = pltpu.roll(utail[k - 1, :, cs], 1, axis=0)
            before[-k] = jnp.where(first_seg, prev, cur)
        for k in range(1, CONV_WIDTH):
            utail[k - 1, :, cs] = U[seg - k]
        tap = lambda p: U[p] if p >= 0 else before[p]

        cwb = [jnp.broadcast_to(cw_ref[k:k + 1, cs], (nseg, LANES)) for k in range(CONV_WIDTH)]
        cbb = jnp.broadcast_to(cb_ref[:, cs], (nseg, LANES))
        xc = jnp.stack([cbb + (tap(p - 3) * cwb[0] + tap(p - 2) * cwb[1] + tap(p - 1) * cwb[2] + U[p] * cwb[3])
                        for p in range(seg)])

        z = jnp.dot(xc.reshape(T, LANES).astype(BF16), wg_ref[c], preferred_element_type=F32) + bg_ref[c]
        r = jax.nn.sigmoid(z[:, :LANES]).reshape(seg, nseg, LANES)
        ig = jax.nn.sigmoid(z[:, LANES:]).reshape(seg, nseg, LANES)
        log_a = (-RG_C * r) * sp[:, cs]
        a = jnp.exp(log_a)
        om = 1.0 - a * a
        root = jnp.where(om > 0.0, om * lax.rsqrt(om), 0.0)
        b = root * (ig * xc)

        hl, pr = [b[0]], [a[0]]
        for p in range(1, seg):
            hl.append(a[p] * hl[-1] + b[p])
            pr.append(a[p] * pr[-1])

        h_in = [hcar[:, cs]]
        for s in range(nseg):
            h_in.append(hl[-1][s:s + 1, :] + pr[-1][s:s + 1, :] * h_in[-1])
        hcar[:, cs] = h_in[nseg]
        h_in = jnp.concatenate(h_in[:nseg], axis=0)

        for p in range(seg):
            hs[c, pl.ds(p, nseg, stride=pitch), :] = hl[p] + pr[p] * h_in

        for s in range(nseg):
            rows = slice(s * seg, (s + 1) * seg)
            gt = gate_ref[rows, cs]
            gelu = 0.5 * gt * (1.0 + jnp.tanh(math.sqrt(2.0 / math.pi) * (gt + 0.044715 * (gt * gt * gt))))
            y_ref[rows, cs] = (hs[c, s * pitch:s * pitch + seg, :] * gelu).astype(y_ref.dtype)


def _a_mix_kernel(*refs, T, pitch, has_cast):
    refs = list(refs)
    x_ref, gn_ref, w_ref, hg_ref, cw_ref, cb_ref, wg_ref, bg_ref, lam_ref = refs[:9]
    del refs[:9]
    cast_src = refs.pop(0) if has_cast else None
    y_ref, mq_ref = refs.pop(0), refs.pop(0)
    cast_dst = refs.pop(0) if has_cast else None
    hn_s, u_cur, g_cur, u_new, g_new, us, hs, utail, hcar = refs
    if has_cast:
        cast_dst[...] = cast_src[...].astype(BF16)
    i = pl.program_id(0)

    @pl.when(i <= 1)
    def _():
        utail[...] = jnp.zeros_like(utail)
        hcar[...] = jnp.zeros_like(hcar)

    @pl.when(i == 0)
    def _():
        u_cur[...] = jnp.zeros_like(u_cur)
        g_cur[...] = jnp.zeros_like(g_cur)

    hn_s[...] = _rms_rows(x_ref[...], gn_ref[...]).astype(BF16)
    nb = LRU_WIDTH // HEAD_BLOCK

    def project(jb):
        acc = jnp.dot(hn_s[...], w_ref[:, jb * HEAD_BLOCK:(jb + 1) * HEAD_BLOCK], preferred_element_type=F32)
        if jb < nb:
            u_new[:, jb * HEAD_BLOCK:(jb + 1) * HEAD_BLOCK] = acc
        elif jb < 2 * nb:
            g_new[:, (jb - nb) * HEAD_BLOCK:(jb - nb + 1) * HEAD_BLOCK] = acc
        else:
            mq_ref[...] = _head_norm_cols(acc, hg_ref[...], ATTN_SCALE).astype(mq_ref.dtype)

    n_proj = 2 * nb + 1
    lane_blocks = LRU_WIDTH // LANES
    at_block = {min(2 * jb, lane_blocks - 1): jb for jb in range(n_proj)}
    assert len(at_block) == n_proj

    def before_block(c):
        if c in at_block:
            project(at_block[c])

    _rglru_tile(u_cur, g_cur, cw_ref, cb_ref, wg_ref, bg_ref, lam_ref, y_ref, us, hs, utail, hcar,
                T=T, LB=LRU_WIDTH, pitch=pitch, before_block=before_block)

    u_cur[...] = u_new[...]
    g_cur[...] = g_new[...]


def _side_cast_specs(cast, n_steps):
    w, idx = cast
    R, C = w.shape[-2:]
    r = R // n_steps
    assert r * n_steps == R and r % 16 == 0
    step = lambda i: jnp.minimum(i, n_steps - 1)
    return (pl.BlockSpec((None, r, C), lambda i: (idx, step(i), 0)),
            pl.BlockSpec((r, C), lambda i: (step(i), 0)),
            jax.ShapeDtypeStruct((R, C), BF16))


def _a_mix(x, norm_gain, w_in, mq_gain, conv_w, conv_b, wg, bg, lam, name, T=512, cast=None):
    S, D = x.shape
    W = LRU_WIDTH
    nt = S // T
    seg = T // SUBLANES
    pitch = seg + SUBLANES if (seg // SUBLANES) % 2 == 0 else seg
    slab = pltpu.VMEM((W // LANES, SUBLANES * pitch, LANES), F32)
    const = lambda shape: pl.BlockSpec(shape, lambda i: (0,) * len(shape))
    cur = lambda i: jnp.minimum(i, nt - 1)
    prv = lambda i: jnp.maximum(i - 1, 0)
    kern = functools.partial(_a_mix_kernel, T=T, pitch=pitch, has_cast=cast is not None)
    in_specs = [
        pl.BlockSpec((T, D), lambda i: (cur(i), 0)),
        const((1, D)),
        pl.BlockSpec((D, A_IN_WIDTH), lambda i: (0, 0), pipeline_mode=pl.Buffered(1)),
        const((1, HEAD_DIM)),
        const((CONV_WIDTH, W)),
        const((1, W)),
        const((LRU_BLOCKS, LRU_BLOCK_WIDTH, 2 * LRU_BLOCK_WIDTH)),
        const((LRU_BLOCKS, 1, 2 * LRU_BLOCK_WIDTH)),
        const((1, W)),
    ]
    args = [x, norm_gain.reshape(1, D), w_in, mq_gain, conv_w, conv_b.reshape(1, W), wg, bg, lam.reshape(1, W)]
    out_specs = [pl.BlockSpec((T, W), lambda i: (prv(i), 0)),
                 pl.BlockSpec((T, MEM_WIDTH), lambda i: (cur(i), 0))]
    out_shape = [jax.ShapeDtypeStruct((S, W), BF16), jax.ShapeDtypeStruct((S, MEM_WIDTH), BF16)]
    if cast is not None:
        c_in, c_out, c_shape = _side_cast_specs(cast, nt)
        in_specs.append(c_in)
        args.append(cast[0])
        out_specs.append(c_out)
        out_shape.append(c_shape)
    return pl.pallas_call(
        kern,
        grid=(nt + 1,),
        in_specs=in_specs,
        out_specs=out_specs,
        out_shape=out_shape,
        scratch_shapes=[pltpu.VMEM((T, D), BF16)] + [pltpu.VMEM((T, W), F32)] * 4 + [
                        slab, slab, pltpu.VMEM((CONV_WIDTH - 1, SUBLANES, W), F32), pltpu.VMEM((1, W), F32)],
        compiler_params=_params(("arbitrary",)),
        name=name,
    )(*args)


def _cast_weight_once(w_ref, wb_ref):
    @pl.when(pl.program_id(0) == 0)
    def _():
        wb_ref[...] = w_ref[...].astype(BF16)


def _out_a_kernel(x_ref, y_ref, mq_ref, mk_ref, mv_ref, w_ref, *rest):
    acc = jnp.dot(y_ref[...], w_ref[0:LRU_WIDTH, :], preferred_element_type=F32)
    mem_out = _mem_attn_rows(mq_ref, mk_ref, mv_ref)
    acc += jnp.dot(mem_out, w_ref[LRU_WIDTH:LRU_WIDTH + MEM_WIDTH, :], preferred_element_type=F32)
    if len(rest) == 1:
        (o_ref,) = rest
    else:
        cast_src, o_ref, cast_dst = rest
        cast_dst[...] = cast_src[...].astype(BF16)
    o_ref[...] = x_ref[...] + acc


def _out_a(x, y, mq, mem_k, mem_v, layer, w, name, tm=1024, cast=None):
    S, D = x.shape
    K = LRU_WIDTH + MEM_WIDTH
    row = pl.BlockSpec((tm, D), lambda i: (i, 0))
    in_specs = [
        row,
        pl.BlockSpec((tm, LRU_WIDTH), lambda i: (i, 0)),
        pl.BlockSpec((tm, MEM_WIDTH), lambda i: (i, 0)),
        *_mem_kv_specs(layer),
        pl.BlockSpec((K, D), lambda i: (0, 0), pipeline_mode=pl.Buffered(1)),
    ]
    args = [x, y, mq, mem_k, mem_v, w]
    out_specs, out_shape = [row], [jax.ShapeDtypeStruct((S, D), F32)]
    if cast is not None:
        c_in, c_out, c_shape = _side_cast_specs(cast, S // tm)
        in_specs.append(c_in)
        args.append(cast[0])
        out_specs.append(c_out)
        out_shape.append(c_shape)
    outs = pl.pallas_call(
        _out_a_kernel,
        grid=(S // tm,),
        in_specs=in_specs,
        out_specs=out_specs,
        out_shape=out_shape,
        compiler_params=_params(("parallel",)),
        name=name,
    )(*args)
    return outs if cast is not None else outs[0]


def _bucket_tables():
    qi = np.arange(Q_BLOCK)[:, None]
    ki = np.arange(2 * Q_BLOCK)[None, :]
    u = qi + Q_BLOCK - ki
    tables, valids = [], []
    for window, dilation in DIL_PATTERNS:
        n = np.maximum(u * dilation, 0)
        nf = np.maximum(n, 1).astype(np.float32)
        large = REL_MAX_EXACT + (np.log(nf / np.float32(REL_MAX_EXACT))
                                 / np.float32(math.log(REL_MAX_DISTANCE / REL_MAX_EXACT))
                                 * np.float32(REL_BUCKETS - REL_MAX_EXACT)).astype(np.int32)
        large = np.minimum(large, REL_BUCKETS - 1)
        tables.append(np.where(n < REL_MAX_EXACT, n, large).astype(np.int32))
        valids.append(((u >= 0) & (u <= window // dilation)).astype(np.int32))
    return np.stack(tables), np.stack(valids)


def _bias_kernel(tab_ref, idx_ref, valid_ref, o_ref):
    g = pl.program_id(0)
    idx = idx_ref[...]
    valid = valid_ref[...] > 0
    for h in range(DIL_HEADS):
        acc = jnp.zeros(idx.shape, F32)
        for b in range(REL_BUCKETS):
            acc = jnp.where(idx == b, tab_ref[b, g * DIL_HEADS + h], acc)
        o_ref[h] = jnp.where(valid, acc, NEG_INF)


def _bias_expand(rel_bias):
    idx, valid = _bucket_tables()
    blk = pl.BlockSpec((None, Q_BLOCK, 2 * Q_BLOCK), lambda g: (g, 0, 0))
    return pl.pallas_call(
        _bias_kernel,
        grid=(DIL_GROUPS,),
        in_specs=[pl.BlockSpec(memory_space=pltpu.SMEM), blk, blk],
        out_specs=pl.BlockSpec((DIL_HEADS, Q_BLOCK, 2 * Q_BLOCK), lambda g: (g, 0, 0)),
        out_shape=jax.ShapeDtypeStruct((DIL_GROUPS * DIL_HEADS, Q_BLOCK, 2 * Q_BLOCK), F32),
        compiler_params=_params(("arbitrary",)),
        name="bias_expand",
    )(rel_bias, jnp.asarray(idx), jnp.asarray(valid))


def _dil_kernel(q_ref, kc_ref, kp_ref, vc_ref, vp_ref, b_ref, o_ref, l_ref, *, LQ, R):
    is_first = pl.program_id(1) == 0
    for rr in range(R):
        for sb in range(LQ // Q_BLOCK):
            rows = slice(sb * Q_BLOCK, (sb + 1) * Q_BLOCK)
            for h in range(DIL_HEADS):
                hs = slice(h * HEAD_DIM, (h + 1) * HEAD_DIM)
                q = q_ref[rr, rows, hs]
                if sb == 0:
                    kk = jnp.concatenate([kp_ref[rr, :, hs], kc_ref[rr, 0:Q_BLOCK, hs]], axis=0)
                    vv = jnp.concatenate([vp_ref[rr, :, hs], vc_ref[rr, 0:Q_BLOCK, hs]], axis=0)
                else:
                    kk = kc_ref[rr, (sb - 1) * Q_BLOCK:(sb + 1) * Q_BLOCK, hs]
                    vv = vc_ref[rr, (sb - 1) * Q_BLOCK:(sb + 1) * Q_BLOCK, hs]
                s = lax.dot_general(q, kk, (((1,), (1,)), ((), ())), preferred_element_type=F32) + b_ref[h]
                if sb == 0:
                    col = lax.broadcasted_iota(jnp.int32, s.shape, 1)
                    s = jnp.where(jnp.logical_and(is_first, col < Q_BLOCK), NEG_INF, s)
                m = jnp.max(s, axis=-1, keepdims=True)
                p = jnp.exp(s - m)
                den = jnp.sum(p, axis=-1, keepdims=True)
                o = jnp.dot(p.astype(BF16), vv, preferred_element_type=F32) / den
                o_ref[rr, rows, hs] = o.astype(o_ref.dtype)
                l_ref[rr, rows, hs] = jnp.broadcast_to(m + jnp.log(den), (Q_BLOCK, HEAD_DIM))


def _dil_attn(q, k, v, bias, g, name, rows_per_step=1024):
    d, L, _ = q.shape
    assert d == DIL_PATTERNS[g][1]
    LQ = min(rows_per_step, L)
    R = min(d, rows_per_step // LQ)
    ratio = LQ // Q_BLOCK
    kern = functools.partial(_dil_kernel, LQ=LQ, R=R)
    cur = pl.BlockSpec((R, LQ, HEAD_BLOCK), lambda r, n: (r, n, 0))
    prev = pl.BlockSpec((R, Q_BLOCK, HEAD_BLOCK), lambda r, n: (r, jnp.maximum(n * ratio - 1, 0), 0))
    return pl.pallas_call(
        kern,
        grid=(d // R, L // LQ),
        in_specs=[cur, cur, prev, cur, prev,
                  pl.BlockSpec((DIL_HEADS, Q_BLOCK, 2 * Q_BLOCK), lambda r, n: (g, 0, 0))],
        out_specs=[pl.BlockSpec((R, LQ, DIL_OUT_WIDTH), lambda r, n: (r, n, 0)),
                   pl.BlockSpec((R, LQ, DIL_OUT_WIDTH), lambda r, n: (r, n, 0))],
        out_shape=[jax.ShapeDtypeStruct((d, L, DIL_OUT_WIDTH), BF16),
                   jax.ShapeDtypeStruct((d, L, DIL_OUT_WIDTH), F32)],
        compiler_params=_params(("arbitrary", "arbitrary")),
        name=name,
    )(q, k, k, v, v, bias)


def _out_b_kernel(x_ref, o0_ref, l0_ref, o1_ref, l1_ref, o2_ref, l2_ref, mq_ref, mk_ref, mv_ref, w_ref, o_ref,
                  wb_ref, s1_ref, s2_ref, *, tm):
    _cast_weight_once(w_ref, wb_ref)
    mem_out = _mem_attn_rows(mq_ref, mk_ref, mv_ref)
    acc = jnp.dot(mem_out, wb_ref[DIL_OUT_WIDTH:DIL_OUT_WIDTH + MEM_WIDTH, :], preferred_element_type=F32)
    for og_ref, lg_ref, s_ref in ((o1_ref, l1_ref, s1_ref), (o2_ref, l2_ref, s2_ref)):
        d = og_ref.shape[0]
        for r in range(d):
            for c in range(DIL_HEADS):
                cl = slice(c * LANES, (c + 1) * LANES)
                s_ref[c, pl.ds(r, tm // d, stride=d), :] = og_ref[r, :, cl].astype(F32)
                s_ref[DIL_HEADS + c, pl.ds(r, tm // d, stride=d), :] = lg_ref[r, :, cl]
    parts = []
    for h in range(DIL_HEADS):
        hs = slice(h * HEAD_DIM, (h + 1) * HEAD_DIM)
        lses = [l0_ref[0, :, hs], s1_ref[DIL_HEADS + h], s2_ref[DIL_HEADS + h]]
        outs = [o0_ref[0, :, hs].astype(F32), s1_ref[h], s2_ref[h]]
        mx = jnp.maximum(jnp.maximum(lses[0], lses[1]), lses[2])
        es = [jnp.exp(l - mx) for l in lses]
        den = es[0] + es[1] + es[2]
        num = es[0] * outs[0] + es[1] * outs[1] + es[2] * outs[2]
        parts.append((num / den).astype(BF16))
    dil = jnp.concatenate(parts, axis=-1)
    acc += jnp.dot(dil, wb_ref[0:DIL_OUT_WIDTH, :], preferred_element_type=F32)
    o_ref[...] = x_ref[...] + acc


def _out_b(x, ogs, mq, mem_k, mem_v, layer, w, widx, name, tm=512):
    S, D = x.shape
    K = DIL_OUT_WIDTH + MEM_WIDTH
    og_specs, og_args = [], []
    for o, l in ogs:
        d = o.shape[0]
        og_specs += [pl.BlockSpec((d, tm // d, DIL_OUT_WIDTH), lambda i: (0, i, 0)),
                     pl.BlockSpec((d, tm // d, DIL_OUT_WIDTH), lambda i: (0, i, 0))]
        og_args += [o, l]
    slab = pltpu.VMEM((2 * DIL_HEADS, tm, LANES), F32)
    row = pl.BlockSpec((tm, D), lambda i: (i, 0))
    return pl.pallas_call(
        functools.partial(_out_b_kernel, tm=tm),
        grid=(S // tm,),
        in_specs=[
            row,
            *og_specs,
            pl.BlockSpec((tm, MEM_WIDTH), lambda i: (i, 0)),
            *_mem_kv_specs(layer),
            pl.BlockSpec((None, K, D), lambda i: (widx, 0, 0), pipeline_mode=pl.Buffered(1)),
        ],
        out_specs=row,
        out_shape=jax.ShapeDtypeStruct((S, D), F32),
        scratch_shapes=[pltpu.VMEM((K, D), BF16), slab, slab],
        compiler_params=_params(("arbitrary",)),
        name=name,
    )(x, *og_args, mq, mem_k, mem_v, w)


def kernel(x, mem, norm_mix_g, norm_mlp_g, mlp_w1, mlp_w2, mem_norm_g, mem_w_kv, mem_q_norm_g, mem_k_norm_g, a_w_in, a_conv_w, a_conv_b, a_gate_r_w, a_gate_r_b, a_gate_i_w, a_gate_i_b, a_lambda, a_w_out, kv_norm_g, kv_w, k_norm_g, rel_bias, b_w_q, b_q_norm_g, b_w_out):
    B, S, D = x.shape
    assert (B, S, D) == (1, SEQ, D_MODEL) and mem.shape == (1, MEM_TOKENS, D_MODEL)
    xs = x[0]
    ones = jnp.ones((1, HEAD_DIM), F32)

    mem_k, mem_v, (w_in_b, w_out_b) = _mem_kv(mem[0], mem_norm_g, mem_w_kv, mem_k_norm_g,
                                              [(a_w_in, 0), (a_w_out, 0)])
    bias = _bias_expand(rel_bias)
    w1b = w2b = None
    later = {}

    for layer in range(DEPTH):
        mq_gain = mem_q_norm_g[layer][None, :]
        a_in, a_gain = xs, norm_mix_g[layer]
        if layer < N_A_LAYERS:
            i = layer
            wg = jnp.concatenate([a_gate_r_w[i], a_gate_i_w[i]], axis=-1).astype(BF16)
            bg = jnp.concatenate([a_gate_r_b[i], a_gate_i_b[i]], axis=-1)[:, None, :]
            if layer == 0:
                y, mq, w1b = _a_mix(a_in, a_gain, w_in_b, mq_gain, a_conv_w[i], a_conv_b[i], wg, bg,
                                    a_lambda[i], f"a{i}_mix", cast=(mlp_w1, 0))
                xs, w2b = _out_a(xs, y, mq, mem_k, mem_v, layer, w_out_b, f"a{i}_out_proj", tm=512,
                                 cast=(mlp_w2, 0))
            else:
                y, mq = _a_mix(a_in, a_gain, w_in_b, mq_gain, a_conv_w[i], a_conv_b[i], wg, bg,
                               a_lambda[i], f"a{i}_mix")
                xs = _out_a(xs, y, mq, mem_k, mem_v, layer, w_out_b, f"a{i}_out_proj")
        else:
            j = layer - N_A_LAYERS
            dils = tuple(d for _, d in DIL_PATTERNS)
            dil_groups = tuple((1, d, BF16) for d in dils)
            if j == 0:
                kv_gains = jnp.concatenate([k_norm_g, jnp.tile(ones, (DIL_GROUPS, 1))], axis=0)
                kvs = _proj(xs, kv_norm_g, later["kv_w"], kv_gains, dil_groups + dil_groups,
                            (True,) * DIL_GROUPS + (False,) * DIL_GROUPS, 1.0, "shared_kv", tm=1024)
            q_gains = jnp.concatenate([b_q_norm_g[j], mq_gain], axis=0)
            qs = _proj(a_in, a_gain, later["w_q"], q_gains, dil_groups + ((1, 1, BF16),),
                       (True,) * (DIL_GROUPS + 1), ATTN_SCALE, f"b{j}_q_proj", tm=1024)
            ogs = [_dil_attn(qs[g], kvs[g], kvs[DIL_GROUPS + g], bias, g, f"b{j}_dil{g}")
                   for g in range(DIL_GROUPS)]
            xs = _out_b(xs, ogs, qs[DIL_GROUPS].reshape(S, MEM_WIDTH), mem_k, mem_v, layer, b_w_out, j,
                        f"b{j}_out_proj")

        nl = layer + 1
        casts, names = [], []
        if nl < DEPTH:
            casts += [(mlp_w1, nl), (mlp_w2, nl)]
            names += ["w1", "w2"]
            if nl < N_A_LAYERS:
                casts += [(a_w_in, nl), (a_w_out, nl)]
                names += ["w_in", "w_out"]
            else:
                casts.append((b_w_q, nl - N_A_LAYERS))
                names.append("w_q")
                if nl == N_A_LAYERS:
                    casts.append((kv_w, None))
                    names.append("kv_w")
        xs, cast_out = _mlp(xs, norm_mlp_g[layer], w1b, w2b, casts, f"mlp{layer}")
        later = dict(zip(names, cast_out))
        w1b, w2b, w_in_b, w_out_b = (later.get(n) for n in ("w1", "w2", "w_in", "w_out"))
    return xs[None]
```

```python
import functools
import math

import numpy as np
import jax
import jax.numpy as jnp
from jax import lax
from jax.experimental import pallas as pl
from jax.experimental.pallas import tpu as pltpu

F32 = jnp.float32
BF16 = jnp.bfloat16

D_MODEL = 2048
SEQ = 8192
DEPTH = 4
N_A_LAYERS = DEPTH // 2
HEAD_DIM = 128
LRU_WIDTH = 3 * D_MODEL // 4
LRU_BLOCK_WIDTH = 128
LRU_BLOCKS = LRU_WIDTH // LRU_BLOCK_WIDTH
CONV_WIDTH = 4
RG_C = 8.0
MEM_TOKENS = 256
MEM_HEADS = 4
MEM_WIDTH = MEM_HEADS * HEAD_DIM
DIL_PATTERNS = ((128, 1), (512, 4), (2048, 16))
DIL_GROUPS = len(DIL_PATTERNS)
DIL_HEADS = 4
DIL_OUT_WIDTH = DIL_HEADS * HEAD_DIM
Q_BLOCK = 128
REL_BUCKETS = 32
REL_MAX_EXACT = REL_BUCKETS // 2
REL_MAX_DISTANCE = 2048
A_IN_WIDTH = 2 * LRU_WIDTH + MEM_WIDTH
NORM_EPS = 1e-6
NEG_INF = -1e30
ATTN_SCALE = HEAD_DIM ** -0.5

V7X_VMEM_BYTES = 64 * 1024 * 1024
SCOPED_VMEM_MIB = 60
assert SCOPED_VMEM_MIB * 1024 * 1024 < V7X_VMEM_BYTES
LANES = 128
SUBLANES = 8
HEAD_BLOCK = DIL_HEADS * HEAD_DIM


def _params(semantics):
    return pltpu.CompilerParams(dimension_semantics=semantics,
                                vmem_limit_bytes=SCOPED_VMEM_MIB * 1024 * 1024)


def _rms_rows(x, g):
    ms = jnp.mean(x * x, axis=-1, keepdims=True)
    return x * lax.rsqrt(ms + NORM_EPS) * g


def _head_norm_cols(a, g, scale):
    parts = []
    for h in range(a.shape[1] // HEAD_DIM):
        ah = a[:, h * HEAD_DIM:(h + 1) * HEAD_DIM]
        y = ah * lax.rsqrt(jnp.mean(ah * ah, axis=-1, keepdims=True) + NORM_EPS) * g
        parts.append(y * scale if scale != 1.0 else y)
    return jnp.concatenate(parts, axis=-1)


def _proj_kernel(a_ref, *rest, groups, normed, scale, tm, has_norm, has_scr):
    rest = list(rest)
    gn_ref = rest.pop(0) if has_norm else None
    w_ref, hg_ref = rest.pop(0), rest.pop(0)
    o_refs = [rest.pop(0) for _ in groups]
    hn_s = rest.pop(0) if has_norm else None
    scr = rest.pop(0) if has_scr else None
    if has_norm:
        hn_s[...] = _rms_rows(a_ref[...], gn_ref[...]).astype(BF16)
        lhs_ref = hn_s
    else:
        lhs_ref = a_ref
    jb = 0
    for (nb, d, _), o_ref in zip(groups, o_refs):
        for k in range(nb):
            acc = jnp.dot(lhs_ref[...], w_ref[:, jb * HEAD_BLOCK:(jb + 1) * HEAD_BLOCK],
                          preferred_element_type=F32)
            if normed[jb]:
                acc = _head_norm_cols(acc, hg_ref[jb:jb + 1, :], scale)
            if d == 1:
                o_ref[0, :, k * HEAD_BLOCK:(k + 1) * HEAD_BLOCK] = acc.astype(o_ref.dtype)
            else:
                for h in range(HEAD_BLOCK // LANES):
                    scr[h] = acc[:, h * LANES:(h + 1) * LANES]
                for r in range(d):
                    for h in range(HEAD_BLOCK // LANES):
                        o_ref[r, :, h * LANES:(h + 1) * LANES] = (
                            scr[h, pl.ds(r, tm // d, stride=d), :].astype(o_ref.dtype))
            jb += 1


def _proj(a, norm_gain, w, head_gains, groups, normed, scale, name, tm):
    S, D = a.shape
    N = w.shape[1]
    nblk = N // HEAD_BLOCK
    assert sum(nb for nb, _, _ in groups) == len(normed) == nblk
    assert all(nb == 1 for nb, d, _ in groups if d > 1)
    has_norm = norm_gain is not None
    has_scr = any(d > 1 for _, d, _ in groups)
    kern = functools.partial(_proj_kernel, groups=groups, normed=normed, scale=scale, tm=tm,
                             has_norm=has_norm, has_scr=has_scr)
    in_specs = [pl.BlockSpec((tm, D), lambda i: (i, 0))]
    args = [a]
    if has_norm:
        in_specs.append(pl.BlockSpec((1, D), lambda i: (0, 0)))
        args.append(norm_gain.reshape(1, D))
    in_specs += [pl.BlockSpec((D, N), lambda i: (0, 0), pipeline_mode=pl.Buffered(1)),
                 pl.BlockSpec((nblk, HEAD_DIM), lambda i: (0, 0))]
    args += [w, head_gains]
    scratch = []
    if has_norm:
        scratch.append(pltpu.VMEM((tm, D), BF16))
    if has_scr:
        scratch.append(pltpu.VMEM((HEAD_BLOCK // LANES, tm, LANES), F32))
    return pl.pallas_call(
        kern,
        grid=(S // tm,),
        in_specs=in_specs,
        out_specs=[pl.BlockSpec((d, tm // d, nb * HEAD_BLOCK), lambda i: (0, i, 0)) for nb, d, _ in groups],
        out_shape=[jax.ShapeDtypeStruct((d, S // d, nb * HEAD_BLOCK), dt) for nb, d, dt in groups],
        scratch_shapes=scratch,
        compiler_params=_params(("parallel",)),
        name=name,
    )(*args)


def _mlp_kernel(x_ref, g_ref, w1_ref, w2_ref, *rest, n_casts):
    cast_in, o_ref, cast_out, hn_ref = rest[:n_casts], rest[n_casts], rest[n_casts + 1:-1], rest[-1]

    @pl.when(pl.program_id(1) == 0)
    def _():
        x = x_ref[...]
        hn_ref[...] = _rms_rows(x, g_ref[...]).astype(BF16)
        o_ref[...] = x

    h = jnp.dot(hn_ref[...], w1_ref[...], preferred_element_type=F32)
    h = jnp.maximum(h, 0.0)
    h = (h * h).astype(BF16)
    o_ref[...] += jnp.dot(h, w2_ref[...], preferred_element_type=F32)

    for src, dst in zip(cast_in, cast_out):
        dst[...] = src[...].astype(BF16)


def _mlp(x, g, w1b, w2b, casts, name, tm=512, tf=1024):
    S, D = x.shape
    F = w1b.shape[1]
    ni, nj = S // tm, F // tf
    row_spec = pl.BlockSpec((tm, D), lambda i, j: (i, 0))
    in_specs = [row_spec, pl.BlockSpec((1, D), lambda i, j: (0, 0)),
                pl.BlockSpec((D, tf), lambda i, j: (0, j)),
                pl.BlockSpec((tf, D), lambda i, j: (j, 0))]
    args = [x, g.reshape(1, D), w1b, w2b]
    out_specs = [row_spec]
    out_shape = [jax.ShapeDtypeStruct((S, D), F32)]
    for w, idx in casts:
        R, C = w.shape[-2:]
        r = R // (ni * nj)
        assert r * ni * nj == R and r % 16 == 0, (R, ni, nj)
        if idx is None:
            in_specs.append(pl.BlockSpec((r, C), lambda i, j: (i * nj + j, 0)))
        else:
            in_specs.append(pl.BlockSpec((None, r, C), lambda i, j, idx=idx: (idx, i * nj + j, 0)))
        args.append(w)
        out_specs.append(pl.BlockSpec((r, C), lambda i, j: (i * nj + j, 0)))
        out_shape.append(jax.ShapeDtypeStruct((R, C), BF16))
    kern = functools.partial(_mlp_kernel, n_casts=len(casts))
    outs = pl.pallas_call(
        kern,
        grid=(ni, nj),
        in_specs=in_specs,
        out_specs=out_specs,
        out_shape=out_shape,
        scratch_shapes=[pltpu.VMEM((tm, D), BF16)],
        compiler_params=_params(("parallel", "arbitrary")),
        name=name,
    )(*args)
    return outs[0], list(outs[1:])


def _mem_kv_kernel(mem_ref, g_ref, w_ref, kg_ref, *rest, n_casts):
    cast_in, (k_ref, v_ref), cast_out = rest[:n_casts], rest[n_casts:n_casts + 2], rest[n_casts + 2:]
    hn = _rms_rows(mem_ref[...], g_ref[...]).astype(BF16)
    kv = jnp.dot(hn, w_ref[...].astype(BF16), preferred_element_type=F32)
    k_ref[...] = _head_norm_cols(kv[:, :MEM_WIDTH], kg_ref[...], 1.0).astype(BF16)
    v_ref[...] = kv[:, MEM_WIDTH:].astype(BF16)
    for src, dst in zip(cast_in, cast_out):
        dst[...] = src[...].astype(BF16)


def _mem_kv(mem, mem_norm_g, mem_w_kv, mem_k_norm_g, casts):
    M, D = mem.shape
    out = jax.ShapeDtypeStruct((DEPTH, M, MEM_WIDTH), BF16)
    in_specs = [
        pl.BlockSpec((M, D), lambda l: (0, 0)),
        pl.BlockSpec((None, 1, D), lambda l: (l, 0, 0)),
        pl.BlockSpec((None, D, 2 * MEM_WIDTH), lambda l: (l, 0, 0)),
        pl.BlockSpec((None, 1, HEAD_DIM), lambda l: (l, 0, 0)),
    ]
    args = [mem, mem_norm_g.reshape(DEPTH, 1, D), mem_w_kv, mem_k_norm_g.reshape(DEPTH, 1, HEAD_DIM)]
    out_specs = [pl.BlockSpec((None, M, MEM_WIDTH), lambda l: (l, 0, 0)),
                 pl.BlockSpec((None, M, MEM_WIDTH), lambda l: (l, 0, 0))]
    out_shape = [out, out]
    for cast in casts:
        c_in, c_out, c_shape = _side_cast_specs(cast, DEPTH)
        in_specs.append(c_in)
        args.append(cast[0])
        out_specs.append(c_out)
        out_shape.append(c_shape)
    outs = pl.pallas_call(
        functools.partial(_mem_kv_kernel, n_casts=len(casts)),
        grid=(DEPTH,),
        in_specs=in_specs,
        out_specs=out_specs,
        out_shape=out_shape,
        compiler_params=_params(("arbitrary",)),
        name="mem_kv",
    )(*args)
    return outs[0], outs[1], list(outs[2:])


def _mem_attn_rows(q_ref, k_ref, v_ref):
    outs = []
    for h in range(MEM_HEADS):
        hs = slice(h * HEAD_DIM, (h + 1) * HEAD_DIM)
        s = lax.dot_general(q_ref[:, hs], k_ref[:, hs], (((1,), (1,)), ((), ())), preferred_element_type=F32)
        m = jnp.max(s, axis=-1, keepdims=True)
        p = jnp.exp(s - m)
        den = jnp.sum(p, axis=-1, keepdims=True)
        o = jnp.dot(p.astype(BF16), v_ref[:, hs], preferred_element_type=F32) / den
        outs.append(o.astype(BF16))
    return jnp.concatenate(outs, axis=-1)


def _mem_kv_specs(layer):
    spec = pl.BlockSpec((None, MEM_TOKENS, MEM_WIDTH), lambda i: (layer, 0, 0))
    return [spec, spec]


def _rglru_tile(u_ref, gate_ref, cw_ref, cb_ref, wg_ref, bg_ref, lam_ref, y_ref, us, hs, utail, hcar,
                *, T, LB, pitch, before_block=lambda c: None):
    nseg = SUBLANES
    seg = T // nseg
    lam = lam_ref[...]
    sp = jnp.maximum(-lam, 0.0) + jnp.log1p(jnp.exp(-jnp.abs(lam)))
    first_seg = lax.broadcasted_iota(jnp.int32, (nseg, LANES), 0) == 0

    for c in range(LB // LANES):
        before_block(c)
        cs = slice(c * LANES, (c + 1) * LANES)
        for s in range(nseg):
            us[c, s * pitch:s * pitch + seg, :] = u_ref[s * seg:(s + 1) * seg, cs]
        U = [us[c, pl.ds(p, nseg, stride=pitch), :] for p in range(seg)]

        before = {}
        for k in range(1, CONV_WIDTH):
            cur = pltpu.roll(U[seg - k], 1, axis=0)
            prev = pltpu.roll(utail[k - 1, :, cs], 1, axis=0)
            before[-k] = jnp.where(first_seg, prev, cur)
        for k in range(1, CONV_WIDTH):
            utail[k - 1, :, cs] = U[seg - k]
        tap = lambda p: U[p] if p >= 0 else before[p]

        cwb = [jnp.broadcast_to(cw_ref[k:k + 1, cs], (nseg, LANES)) for k in range(CONV_WIDTH)]
        cbb = jnp.broadcast_to(cb_ref[:, cs], (nseg, LANES))
        xc = jnp.stack([cbb + (tap(p - 3) * cwb[0] + tap(p - 2) * cwb[1] + tap(p - 1) * cwb[2] + U[p] * cwb[3])
                        for p in range(seg)])

        z = jnp.dot(xc.reshape(T, LANES).astype(BF16), wg_ref[c], preferred_element_type=F32) + bg_ref[c]
        r = jax.nn.sigmoid(z[:, :LANES]).reshape(seg, nseg, LANES)
        ig = jax.nn.sigmoid(z[:, LANES:]).reshape(seg, nseg, LANES)
        log_a = (-RG_C * r) * sp[:, cs]
        a = jnp.exp(log_a)
        om = 1.0 - a * a
        root = jnp.where(om > 0.0, om * lax.rsqrt(om), 0.0)
        b = root * (ig * xc)

        hl, pr = [b[0]], [a[0]]
        for p in range(1, seg):
            hl.append(a[p] * hl[-1] + b[p])
            pr.append(a[p] * pr[-1])

        h_in = [hcar[:, cs]]
        for s in range(nseg):
            h_in.append(hl[-1][s:s + 1, :] + pr[-1][s:s + 1, :] * h_in[-1])
        hcar[:, cs] = h_in[nseg]
        h_in = jnp.concatenate(h_in[:nseg], axis=0)

        for p in range(seg):
            hs[c, pl.ds(p, nseg, stride=pitch), :] = hl[p] + pr[p] * h_in

        for s in range(nseg):
            rows = slice(s * seg, (s + 1) * seg)
            gt = gate_ref[rows, cs]
            gelu = 0.5 * gt * (1.0 + jnp.tanh(math.sqrt(2.0 / math.pi) * (gt + 0.044715 * (gt * gt * gt))))
            y_ref[rows, cs] = (hs[c, s * pitch:s * pitch + seg, :] * gelu).astype(y_ref.dtype)


def _a_mix_kernel(*refs, T, pitch, has_cast):
    refs = list(refs)
    x_ref, gn_ref, w_ref, hg_ref, cw_ref, cb_ref, wg_ref, bg_ref, lam_ref = refs[:9]
    del refs[:9]
    cast_src = refs.pop(0) if has_cast else None
    y_ref, mq_ref = refs.pop(0), refs.pop(0)
    cast_dst = refs.pop(0) if has_cast else None
    hn_s, u_cur, g_cur, u_new, g_new, us, hs, utail, hcar = refs
    if has_cast:
        cast_dst[...] = cast_src[...].astype(BF16)
    i = pl.program_id(0)

    @pl.when(i <= 1)
    def _():
        utail[...] = jnp.zeros_like(utail)
        hcar[...] = jnp.zeros_like(hcar)

    @pl.when(i == 0)
    def _():
        u_cur[...] = jnp.zeros_like(u_cur)
        g_cur[...] = jnp.zeros_like(g_cur)

    hn_s[...] = _rms_rows(x_ref[...], gn_ref[...]).astype(BF16)
    nb = LRU_WIDTH // HEAD_BLOCK

    def project(jb):
        acc = jnp.dot(hn_s[...], w_ref[:, jb * HEAD_BLOCK:(jb + 1) * HEAD_BLOCK], preferred_element_type=F32)
        if jb < nb:
            u_new[:, jb * HEAD_BLOCK:(jb + 1) * HEAD_BLOCK] = acc
        elif jb < 2 * nb:
            g_new[:, (jb - nb) * HEAD_BLOCK:(jb - nb + 1) * HEAD_BLOCK] = acc
        else:
            mq_ref[...] = _head_norm_cols(acc, hg_ref[...], ATTN_SCALE).astype(mq_ref.dtype)

    n_proj = 2 * nb + 1
    lane_blocks = LRU_WIDTH // LANES
    at_block = {min(2 * jb, lane_blocks - 1): jb for jb in range(n_proj)}
    assert len(at_block) == n_proj

    def before_block(c):
        if c in at_block:
            project(at_block[c])

    _rglru_tile(u_cur, g_cur, cw_ref, cb_ref, wg_ref, bg_ref, lam_ref, y_ref, us, hs, utail, hcar,
                T=T, LB=LRU_WIDTH, pitch=pitch, before_block=before_block)

    u_cur[...] = u_new[...]
    g_cur[...] = g_new[...]


def _side_cast_specs(cast, n_steps):
    w, idx = cast
    R, C = w.shape[-2:]
    r = R // n_steps
    assert r * n_steps == R and r % 16 == 0
    step = lambda i: jnp.minimum(i, n_steps - 1)
    return (pl.BlockSpec((None, r, C), lambda i: (idx, step(i), 0)),
            pl.BlockSpec((r, C), lambda i: (step(i), 0)),
            jax.ShapeDtypeStruct((R, C), BF16))


def _a_mix(x, norm_gain, w_in, mq_gain, conv_w, conv_b, wg, bg, lam, name, T=512, cast=None):
    S, D = x.shape
    W = LRU_WIDTH
    nt = S // T
    seg = T // SUBLANES
    pitch = seg + SUBLANES if (seg // SUBLANES) % 2 == 0 else seg
    slab = pltpu.VMEM((W // LANES, SUBLANES * pitch, LANES), F32)
    const = lambda shape: pl.BlockSpec(shape, lambda i: (0,) * len(shape))
    cur = lambda i: jnp.minimum(i, nt - 1)
    prv = lambda i: jnp.maximum(i - 1, 0)
    kern = functools.partial(_a_mix_kernel, T=T, pitch=pitch, has_cast=cast is not None)
    in_specs = [
        pl.BlockSpec((T, D), lambda i: (cur(i), 0)),
        const((1, D)),
        pl.BlockSpec((D, A_IN_WIDTH), lambda i: (0, 0), pipeline_mode=pl.Buffered(1)),
        const((1, HEAD_DIM)),
        const((CONV_WIDTH, W)),
        const((1, W)),
        const((LRU_BLOCKS, LRU_BLOCK_WIDTH, 2 * LRU_BLOCK_WIDTH)),
        const((LRU_BLOCKS, 1, 2 * LRU_BLOCK_WIDTH)),
        const((1, W)),
    ]
    args = [x, norm_gain.reshape(1, D), w_in, mq_gain, conv_w, conv_b.reshape(1, W), wg, bg, lam.reshape(1, W)]
    out_specs = [pl.BlockSpec((T, W), lambda i: (prv(i), 0)),
                 pl.BlockSpec((T, MEM_WIDTH), lambda i: (cur(i), 0))]
    out_shape = [jax.ShapeDtypeStruct((S, W), BF16), jax.ShapeDtypeStruct((S, MEM_WIDTH), BF16)]
    if cast is not None:
        c_in, c_out, c_shape = _side_cast_specs(cast, nt)
        in_specs.append(c_in)
        args.append(cast[0])
        out_specs.append(c_out)
        out_shape.append(c_shape)
    return pl.pallas_call(
        kern,
        grid=(nt + 1,),
        in_specs=in_specs,
        out_specs=out_specs,
        out_shape=out_shape,
        scratch_shapes=[pltpu.VMEM((T, D), BF16)] + [pltpu.VMEM((T, W), F32)] * 4 + [
                        slab, slab, pltpu.VMEM((CONV_WIDTH - 1, SUBLANES, W), F32), pltpu.VMEM((1, W), F32)],
        compiler_params=_params(("arbitrary",)),
        name=name,
    )(*args)


def _cast_weight_once(w_ref, wb_ref):
    @pl.when(pl.program_id(0) == 0)
    def _():
        wb_ref[...] = w_ref[...].astype(BF16)


def _out_a_kernel(x_ref, y_ref, mq_ref, mk_ref, mv_ref, w_ref, *rest):
    acc = jnp.dot(y_ref[...], w_ref[0:LRU_WIDTH, :], preferred_element_type=F32)
    mem_out = _mem_attn_rows(mq_ref, mk_ref, mv_ref)
    acc += jnp.dot(mem_out, w_ref[LRU_WIDTH:LRU_WIDTH + MEM_WIDTH, :], preferred_element_type=F32)
    if len(rest) == 1:
        (o_ref,) = rest
    else:
        cast_src, o_ref, cast_dst = rest
        cast_dst[...] = cast_src[...].astype(BF16)
    o_ref[...] = x_ref[...] + acc


def _out_a(x, y, mq, mem_k, mem_v, layer, w, name, tm=1024, cast=None):
    S, D = x.shape
    K = LRU_WIDTH + MEM_WIDTH
    row = pl.BlockSpec((tm, D), lambda i: (i, 0))
    in_specs = [
        row,
        pl.BlockSpec((tm, LRU_WIDTH), lambda i: (i, 0)),
        pl.BlockSpec((tm, MEM_WIDTH), lambda i: (i, 0)),
        *_mem_kv_specs(layer),
        pl.BlockSpec((K, D), lambda i: (0, 0), pipeline_mode=pl.Buffered(1)),
    ]
    args = [x, y, mq, mem_k, mem_v, w]
    out_specs, out_shape = [row], [jax.ShapeDtypeStruct((S, D), F32)]
    if cast is not None:
        c_in, c_out, c_shape = _side_cast_specs(cast, S // tm)
        in_specs.append(c_in)
        args.append(cast[0])
        out_specs.append(c_out)
        out_shape.append(c_shape)
    outs = pl.pallas_call(
        _out_a_kernel,
        grid=(S // tm,),
        in_specs=in_specs,
        out_specs=out_specs,
        out_shape=out_shape,
        compiler_params=_params(("parallel",)),
        name=name,
    )(*args)
    return outs if cast is not None else outs[0]


def _bucket_tables():
    qi = np.arange(Q_BLOCK)[:, None]
    ki = np.arange(2 * Q_BLOCK)[None, :]
    u = qi + Q_BLOCK - ki
    tables, valids = [], []
    for window, dilation in DIL_PATTERNS:
        n = np.maximum(u * dilation, 0)
        nf = np.maximum(n, 1).astype(np.float32)
        large = REL_MAX_EXACT + (np.log(nf / np.float32(REL_MAX_EXACT))
                                 / np.float32(math.log(REL_MAX_DISTANCE / REL_MAX_EXACT))
                                 * np.float32(REL_BUCKETS - REL_MAX_EXACT)).astype(np.int32)
        large = np.minimum(large, REL_BUCKETS - 1)
        tables.append(np.where(n < REL_MAX_EXACT, n, large).astype(np.int32))
        valids.append(((u >= 0) & (u <= window // dilation)).astype(np.int32))
    return np.stack(tables), np.stack(valids)


def _bias_kernel(tab_ref, idx_ref, valid_ref, o_ref):
    g = pl.program_id(0)
    idx = idx_ref[...]
    valid = valid_ref[...] > 0
    for h in range(DIL_HEADS):
        acc = jnp.zeros(idx.shape, F32)
        for b in range(REL_BUCKETS):
            acc = jnp.where(idx == b, tab_ref[b, g * DIL_HEADS + h], acc)
        o_ref[h] = jnp.where(valid, acc, NEG_INF)


def _bias_expand(rel_bias):
    idx, valid = _bucket_tables()
    blk = pl.BlockSpec((None, Q_BLOCK, 2 * Q_BLOCK), lambda g: (g, 0, 0))
    return pl.pallas_call(
        _bias_kernel,
        grid=(DIL_GROUPS,),
        in_specs=[pl.BlockSpec(memory_space=pltpu.SMEM), blk, blk],
        out_specs=pl.BlockSpec((DIL_HEADS, Q_BLOCK, 2 * Q_BLOCK), lambda g: (g, 0, 0)),
        out_shape=jax.ShapeDtypeStruct((DIL_GROUPS * DIL_HEADS, Q_BLOCK, 2 * Q_BLOCK), F32),
        compiler_params=_params(("arbitrary",)),
        name="bias_expand",
    )(rel_bias, jnp.asarray(idx), jnp.asarray(valid))


def _dil_kernel(q_ref, kc_ref, kp_ref, vc_ref, vp_ref, b_ref, o_ref, l_ref, *, LQ, R):
    is_first = pl.program_id(1) == 0
    for rr in range(R):
        for sb in range(LQ // Q_BLOCK):
            rows = slice(sb * Q_BLOCK, (sb + 1) * Q_BLOCK)
            for h in range(DIL_HEADS):
                hs = slice(h * HEAD_DIM, (h + 1) * HEAD_DIM)
                q = q_ref[rr, rows, hs]
                if sb == 0:
                    kk = jnp.concatenate([kp_ref[rr, :, hs], kc_ref[rr, 0:Q_BLOCK, hs]], axis=0)
                    vv = jnp.concatenate([vp_ref[rr, :, hs], vc_ref[rr, 0:Q_BLOCK, hs]], axis=0)
                else:
                    kk = kc_ref[rr, (sb - 1) * Q_BLOCK:(sb + 1) * Q_BLOCK, hs]
                    vv = vc_ref[rr, (sb - 1) * Q_BLOCK:(sb + 1) * Q_BLOCK, hs]
                s = lax.dot_general(q, kk, (((1,), (1,)), ((), ())), preferred_element_type=F32) + b_ref[h]
                if sb == 0:
                    col = lax.broadcasted_iota(jnp.int32, s.shape, 1)
                    s = jnp.where(jnp.logical_and(is_first, col < Q_BLOCK), NEG_INF, s)
                m = jnp.max(s, axis=-1, keepdims=True)
                p = jnp.exp(s - m)
                den = jnp.sum(p, axis=-1, keepdims=True)
                o = jnp.dot(p.astype(BF16), vv, preferred_element_type=F32) / den
                o_ref[rr, rows, hs] = o.astype(o_ref.dtype)
                l_ref[rr, rows, hs] = jnp.broadcast_to(m + jnp.log(den), (Q_BLOCK, HEAD_DIM))


def _dil_attn(q, k, v, bias, g, name, rows_per_step=2048):
    d, L, _ = q.shape
    assert d == DIL_PATTERNS[g][1]
    LQ = min(rows_per_step, L)
    R = min(d, rows_per_step // LQ)
    ratio = LQ // Q_BLOCK
    kern = functools.partial(_dil_kernel, LQ=LQ, R=R)
    cur = pl.BlockSpec((R, LQ, HEAD_BLOCK), lambda r, n: (r, n, 0))
    prev = pl.BlockSpec((R, Q_BLOCK, HEAD_BLOCK), lambda r, n: (r, jnp.maximum(n * ratio - 1, 0), 0))
    return pl.pallas_call(
        kern,
        grid=(d // R, L // LQ),
        in_specs=[cur, cur, prev, cur, prev,
                  pl.BlockSpec((DIL_HEADS, Q_BLOCK, 2 * Q_BLOCK), lambda r, n: (g, 0, 0))],
        out_specs=[pl.BlockSpec((R, LQ, DIL_OUT_WIDTH), lambda r, n: (r, n, 0)),
                   pl.BlockSpec((R, LQ, DIL_OUT_WIDTH), lambda r, n: (r, n, 0))],
        out_shape=[jax.ShapeDtypeStruct((d, L, DIL_OUT_WIDTH), BF16),
                   jax.ShapeDtypeStruct((d, L, DIL_OUT_WIDTH), F32)],
        compiler_params=_params(("arbitrary", "arbitrary")),
        name=name,
    )(q, k, k, v, v, bias)


def _out_b_kernel(x_ref, o0_ref, l0_ref, o1_ref, l1_ref, o2_ref, l2_ref, mq_ref, mk_ref, mv_ref, w_ref, o_ref,
                  wb_ref, s1_ref, s2_ref, *, tm):
    _cast_weight_once(w_ref, wb_ref)
    mem_out = _mem_attn_rows(mq_ref, mk_ref, mv_ref)
    acc = jnp.dot(mem_out, wb_ref[DIL_OUT_WIDTH:DIL_OUT_WIDTH + MEM_WIDTH, :], preferred_element_type=F32)
    for og_ref, lg_ref, s_ref in ((o1_ref, l1_ref, s1_ref), (o2_ref, l2_ref, s2_ref)):
        d = og_ref.shape[0]
        for r in range(d):
            for c in range(DIL_HEADS):
                cl = slice(c * LANES, (c + 1) * LANES)
                s_ref[c, pl.ds(r, tm // d, stride=d), :] = og_ref[r, :, cl].astype(F32)
                s_ref[DIL_HEADS + c, pl.ds(r, tm // d, stride=d), :] = lg_ref[r, :, cl]
    parts = []
    for h in range(DIL_HEADS):
        hs = slice(h * HEAD_DIM, (h + 1) * HEAD_DIM)
        lses = [l0_ref[0, :, hs], s1_ref[DIL_HEADS + h], s2_ref[DIL_HEADS + h]]
        outs = [o0_ref[0, :, hs].astype(F32), s1_ref[h], s2_ref[h]]
        mx = jnp.maximum(jnp.maximum(lses[0], lses[1]), lses[2])
        es = [jnp.exp(l - mx) for l in lses]
        den = es[0] + es[1] + es[2]
        num = es[0] * outs[0] + es[1] * outs[1] + es[2] * outs[2]
        parts.append((num / den).astype(BF16))
    dil = jnp.concatenate(parts, axis=-1)
    acc += jnp.dot(dil, wb_ref[0:DIL_OUT_WIDTH, :], preferred_element_type=F32)
    o_ref[...] = x_ref[...] + acc


def _out_b(x, ogs, mq, mem_k, mem_v, layer, w, widx, name, tm=512):
    S, D = x.shape
    K = DIL_OUT_WIDTH + MEM_WIDTH
    og_specs, og_args = [], []
    for o, l in ogs:
        d = o.shape[0]
        og_specs += [pl.BlockSpec((d, tm // d, DIL_OUT_WIDTH), lambda i: (0, i, 0)),
                     pl.BlockSpec((d, tm // d, DIL_OUT_WIDTH), lambda i: (0, i, 0))]
        og_args += [o, l]
    slab = pltpu.VMEM((2 * DIL_HEADS, tm, LANES), F32)
    row = pl.BlockSpec((tm, D), lambda i: (i, 0))
    return pl.pallas_call(
        functools.partial(_out_b_kernel, tm=tm),
        grid=(S // tm,),
        in_specs=[
            row,
            *og_specs,
            pl.BlockSpec((tm, MEM_WIDTH), lambda i: (i, 0)),
            *_mem_kv_specs(layer),
            pl.BlockSpec((None, K, D), lambda i: (widx, 0, 0), pipeline_mode=pl.Buffered(1)),
        ],
        out_specs=row,
        out_shape=jax.ShapeDtypeStruct((S, D), F32),
        scratch_shapes=[pltpu.VMEM((K, D), BF16), slab, slab],
        compiler_params=_params(("arbitrary",)),
        name=name,
    )(x, *og_args, mq, mem_k, mem_v, w)


def kernel(x, mem, norm_mix_g, norm_mlp_g, mlp_w1, mlp_w2, mem_norm_g, mem_w_kv, mem_q_norm_g, mem_k_norm_g, a_w_in, a_conv_w, a_conv_b, a_gate_r_w, a_gate_r_b, a_gate_i_w, a_gate_i_b, a_lambda, a_w_out, kv_norm_g, kv_w, k_norm_g, rel_bias, b_w_q, b_q_norm_g, b_w_out):
    B, S, D = x.shape
    assert (B, S, D) == (1, SEQ, D_MODEL) and mem.shape == (1, MEM_TOKENS, D_MODEL)
    xs = x[0]
    ones = jnp.ones((1, HEAD_DIM), F32)

    mem_k, mem_v, (w_in_b, w_out_b) = _mem_kv(mem[0], mem_norm_g, mem_w_kv, mem_k_norm_g,
                                              [(a_w_in, 0), (a_w_out, 0)])
    bias = _bias_expand(rel_bias)
    w1b = w2b = None
    later = {}

    for layer in range(DEPTH):
        mq_gain = mem_q_norm_g[layer][None, :]
        a_in, a_gain = xs, norm_mix_g[layer]
        if layer < N_A_LAYERS:
            i = layer
            wg = jnp.concatenate([a_gate_r_w[i], a_gate_i_w[i]], axis=-1).astype(BF16)
            bg = jnp.concatenate([a_gate_r_b[i], a_gate_i_b[i]], axis=-1)[:, None, :]
            if layer == 0:
                y, mq, w1b = _a_mix(a_in, a_gain, w_in_b, mq_gain, a_conv_w[i], a_conv_b[i], wg, bg,
                                    a_lambda[i], f"a{i}_mix", cast=(mlp_w1, 0))
                xs, w2b = _out_a(xs, y, mq, mem_k, mem_v, layer, w_out_b, f"a{i}_out_proj", tm=512,
                                 cast=(mlp_w2, 0))
            else:
                y, mq = _a_mix(a_in, a_gain, w_in_b, mq_gain, a_conv_w[i], a_conv_b[i], wg, bg,
                               a_lambda[i], f"a{i}_mix")
                xs = _out_a(xs, y, mq, mem_k, mem_v, layer, w_out_b, f"a{i}_out_proj")
        else:
            j = layer - N_A_LAYERS
            dils = tuple(d for _, d in DIL_PATTERNS)
            dil_groups = tuple((1, d, BF16) for d in dils)
            if j == 0:
                kv_gains = jnp.concatenate([k_norm_g, jnp.tile(ones, (DIL_GROUPS, 1))], axis=0)
                kvs = _proj(xs, kv_norm_g, later["kv_w"], kv_gains, dil_groups + dil_groups,
                            (True,) * DIL_GROUPS + (False,) * DIL_GROUPS, 1.0, "shared_kv", tm=1024)
            q_gains = jnp.concatenate([b_q_norm_g[j], mq_gain], axis=0)
            qs = _proj(a_in, a_gain, later["w_q"], q_gains, dil_groups + ((1, 1, BF16),),
                       (True,) * (DIL_GROUPS + 1), ATTN_SCALE, f"b{j}_q_proj", tm=1024)
            ogs = [_dil_attn(qs[g], kvs[g], kvs[DIL_GROUPS + g], bias, g, f"b{j}_dil{g}")
                   for g in range(DIL_GROUPS)]
            xs = _out_b(xs, ogs, qs[DIL_GROUPS].reshape(S, MEM_WIDTH), mem_k, mem_v, layer, b_w_out, j,
                        f"b{j}_out_proj")

        nl = layer + 1
        casts, names = [], []
        if nl < DEPTH:
            casts += [(mlp_w1, nl), (mlp_w2, nl)]
            names += ["w1", "w2"]
            if nl < N_A_LAYERS:
                casts += [(a_w_in, nl), (a_w_out, nl)]
                names += ["w_in", "w_out"]
            else:
                casts.append((b_w_q, nl - N_A_LAYERS))
                names.append("w_q")
                if nl == N_A_LAYERS:
                    casts.append((kv_w, None))
                    names.append("kv_w")
        xs, cast_out = _mlp(xs, norm_mlp_g[layer], w1b, w2b, casts, f"mlp{layer}")
        later = dict(zip(names, cast_out))
        w1b, w2b, w_in_b, w_out_b = (later.get(n) for n in ("w1", "w2", "w_in", "w_out"))
    return xs[None]
```

```python
import functools
import math

import numpy as np
import jax
import jax.numpy as jnp
from jax import lax
from jax.experimental import pallas as pl
from jax.experimental.pallas import tpu as pltpu

F32 = jnp.float32
BF16 = jnp.bfloat16

D_MODEL = 2048
SEQ = 8192
DEPTH = 4
N_A_LAYERS = DEPTH // 2
HEAD_DIM = 128
LRU_WIDTH = 3 * D_MODEL // 4
LRU_BLOCK_WIDTH = 128
LRU_BLOCKS = LRU_WIDTH // LRU_BLOCK_WIDTH
CONV_WIDTH = 4
RG_C = 8.0
MEM_TOKENS = 256
MEM_HEADS = 4
MEM_WIDTH = MEM_HEADS * HEAD_DIM
DIL_PATTERNS = ((128, 1), (512, 4), (2048, 16))
DIL_GROUPS = len(DIL_PATTERNS)
DIL_HEADS = 4
DIL_OUT_WIDTH = DIL_HEADS * HEAD_DIM
Q_BLOCK = 128
REL_BUCKETS = 32
REL_MAX_EXACT = REL_BUCKETS // 2
REL_MAX_DISTANCE = 2048
A_IN_WIDTH = 2 * LRU_WIDTH + MEM_WIDTH
NORM_EPS = 1e-6
NEG_INF = -1e30
ATTN_SCALE = HEAD_DIM ** -0.5

V7X_VMEM_BYTES = 64 * 1024 * 1024
SCOPED_VMEM_MIB = 60
assert SCOPED_VMEM_MIB * 1024 * 1024 < V7X_VMEM_BYTES
LANES = 128
SUBLANES = 8
HEAD_BLOCK = DIL_HEADS * HEAD_DIM


def _params(semantics):
    return pltpu.CompilerParams(dimension_semantics=semantics,
                                vmem_limit_bytes=SCOPED_VMEM_MIB * 1024 * 1024)


def _rms_rows(x, g):
    ms = jnp.mean(x * x, axis=-1, keepdims=True)
    return x * lax.rsqrt(ms + NORM_EPS) * g


def _head_norm_cols(a, g, scale):
    parts = []
    for h in range(a.shape[1] // HEAD_DIM):
        ah = a[:, h * HEAD_DIM:(h + 1) * HEAD_DIM]
        y = ah * lax.rsqrt(jnp.mean(ah * ah, axis=-1, keepdims=True) + NORM_EPS) * g
        parts.append(y * scale if scale != 1.0 else y)
    return jnp.concatenate(parts, axis=-1)


def _proj_kernel(x_ref, *rest, parts, tm):
    rest = list(rest)
    part_refs = [(rest.pop(0), rest.pop(0), rest.pop(0)) for _ in parts]
    o_refs = [[rest.pop(0) for _ in groups] for groups, _, _ in parts]
    hn_s, scr = rest
    for (groups, normed, scale), (gn_ref, w_ref, hg_ref), outs in zip(parts, part_refs, o_refs):
        hn_s[...] = _rms_rows(x_ref[...], gn_ref[...]).astype(BF16)
        jb = 0
        for (nb, d, _), o_ref in zip(groups, outs):
            for k in range(nb):
                acc = jnp.dot(hn_s[...], w_ref[:, jb * HEAD_BLOCK:(jb + 1) * HEAD_BLOCK],
                              preferred_element_type=F32)
                if normed[jb]:
                    acc = _head_norm_cols(acc, hg_ref[jb:jb + 1, :], scale)
                if d == 1:
                    o_ref[0, :, k * HEAD_BLOCK:(k + 1) * HEAD_BLOCK] = acc.astype(o_ref.dtype)
                else:
                    for h in range(HEAD_BLOCK // LANES):
                        scr[h] = acc[:, h * LANES:(h + 1) * LANES]
                    for r in range(d):
                        for h in range(HEAD_BLOCK // LANES):
                            o_ref[r, :, h * LANES:(h + 1) * LANES] = (
                                scr[h, pl.ds(r, tm // d, stride=d), :].astype(o_ref.dtype))
                jb += 1


def _proj(x, parts, name, tm):
    S, D = x.shape
    in_specs = [pl.BlockSpec((tm, D), lambda i: (i, 0))]
    args = [x]
    out_specs, out_shape, static = [], [], []
    for norm_gain, w, head_gains, groups, normed, scale in parts:
        N = w.shape[1]
        nblk = N // HEAD_BLOCK
        assert sum(nb for nb, _, _ in groups) == len(normed) == nblk
        assert all(nb == 1 for nb, d, _ in groups if d > 1)
        in_specs += [pl.BlockSpec((1, D), lambda i: (0, 0)),
                     pl.BlockSpec((D, N), lambda i: (0, 0), pipeline_mode=pl.Buffered(1)),
                     pl.BlockSpec((nblk, HEAD_DIM), lambda i: (0, 0))]
        args += [norm_gain.reshape(1, D), w, head_gains]
        out_specs += [pl.BlockSpec((d, tm // d, nb * HEAD_BLOCK), lambda i: (0, i, 0)) for nb, d, _ in groups]
        out_shape += [jax.ShapeDtypeStruct((d, S // d, nb * HEAD_BLOCK), dt) for nb, d, dt in groups]
        static.append((groups, normed, scale))
    outs = pl.pallas_call(
        functools.partial(_proj_kernel, parts=tuple(static), tm=tm),
        grid=(S // tm,),
        in_specs=in_specs,
        out_specs=out_specs,
        out_shape=out_shape,
        scratch_shapes=[pltpu.VMEM((tm, D), BF16), pltpu.VMEM((HEAD_BLOCK // LANES, tm, LANES), F32)],
        compiler_params=_params(("parallel",)),
        name=name,
    )(*args)
    result, pos = [], 0
    for _, _, _, groups, _, _ in parts:
        result.append(list(outs[pos:pos + len(groups)]))
        pos += len(groups)
    return result


def _mlp_kernel(x_ref, g_ref, w1_ref, w2_ref, *rest, n_casts):
    cast_in, o_ref, cast_out, hn_ref = rest[:n_casts], rest[n_casts], rest[n_casts + 1:-1], rest[-1]

    @pl.when(pl.program_id(1) == 0)
    def _():
        x = x_ref[...]
        hn_ref[...] = _rms_rows(x, g_ref[...]).astype(BF16)
        o_ref[...] = x

    h = jnp.dot(hn_ref[...], w1_ref[...], preferred_element_type=F32)
    h = jnp.maximum(h, 0.0)
    h = (h * h).astype(BF16)
    o_ref[...] += jnp.dot(h, w2_ref[...], preferred_element_type=F32)

    for src, dst in zip(cast_in, cast_out):
        dst[...] = src[...].astype(BF16)


def _mlp(x, g, w1b, w2b, casts, name, tm=1024, tf=512):
    S, D = x.shape
    F = w1b.shape[1]
    ni, nj = S // tm, F // tf
    row_spec = pl.BlockSpec((tm, D), lambda i, j: (i, 0))
    in_specs = [row_spec, pl.BlockSpec((1, D), lambda i, j: (0, 0)),
                pl.BlockSpec((D, tf), lambda i, j: (0, j)),
                pl.BlockSpec((tf, D), lambda i, j: (j, 0))]
    args = [x, g.reshape(1, D), w1b, w2b]
    out_specs = [row_spec]
    out_shape = [jax.ShapeDtypeStruct((S, D), F32)]
    for w, idx in casts:
        R, C = w.shape[-2:]
        r = R // (ni * nj)
        assert r * ni * nj == R and r % 16 == 0, (R, ni, nj)
        if idx is None:
            in_specs.append(pl.BlockSpec((r, C), lambda i, j: (i * nj + j, 0)))
        else:
            in_specs.append(pl.BlockSpec((None, r, C), lambda i, j, idx=idx: (idx, i * nj + j, 0)))
        args.append(w)
        out_specs.append(pl.BlockSpec((r, C), lambda i, j: (i * nj + j, 0)))
        out_shape.append(jax.ShapeDtypeStruct((R, C), BF16))
    kern = functools.partial(_mlp_kernel, n_casts=len(casts))
    outs = pl.pallas_call(
        kern,
        grid=(ni, nj),
        in_specs=in_specs,
        out_specs=out_specs,
        out_shape=out_shape,
        scratch_shapes=[pltpu.VMEM((tm, D), BF16)],
        compiler_params=_params(("parallel", "arbitrary")),
        name=name,
    )(*args)
    return outs[0], list(outs[1:])


def _mem_kv_kernel(mem_ref, g_ref, w_ref, kg_ref, *rest, n_casts):
    cast_in, (k_ref, v_ref), cast_out = rest[:n_casts], rest[n_casts:n_casts + 2], rest[n_casts + 2:]
    hn = _rms_rows(mem_ref[...], g_ref[...]).astype(BF16)
    kv = jnp.dot(hn, w_ref[...].astype(BF16), preferred_element_type=F32)
    k_ref[...] = _head_norm_cols(kv[:, :MEM_WIDTH], kg_ref[...], 1.0).astype(BF16)
    v_ref[...] = kv[:, MEM_WIDTH:].astype(BF16)
    for src, dst in zip(cast_in, cast_out):
        dst[...] = src[...].astype(BF16)


def _mem_kv(mem, mem_norm_g, mem_w_kv, mem_k_norm_g, casts):
    M, D = mem.shape
    out = jax.ShapeDtypeStruct((DEPTH, M, MEM_WIDTH), BF16)
    in_specs = [
        pl.BlockSpec((M, D), lambda l: (0, 0)),
        pl.BlockSpec((None, 1, D), lambda l: (l, 0, 0)),
        pl.BlockSpec((None, D, 2 * MEM_WIDTH), lambda l: (l, 0, 0)),
        pl.BlockSpec((None, 1, HEAD_DIM), lambda l: (l, 0, 0)),
    ]
    args = [mem, mem_norm_g.reshape(DEPTH, 1, D), mem_w_kv, mem_k_norm_g.reshape(DEPTH, 1, HEAD_DIM)]
    out_specs = [pl.BlockSpec((None, M, MEM_WIDTH), lambda l: (l, 0, 0)),
                 pl.BlockSpec((None, M, MEM_WIDTH), lambda l: (l, 0, 0))]
    out_shape = [out, out]
    for cast in casts:
        c_in, c_out, c_shape = _side_cast_specs(cast, DEPTH)
        in_specs.append(c_in)
        args.append(cast[0])
        out_specs.append(c_out)
        out_shape.append(c_shape)
    outs = pl.pallas_call(
        functools.partial(_mem_kv_kernel, n_casts=len(casts)),
        grid=(DEPTH,),
        in_specs=in_specs,
        out_specs=out_specs,
        out_shape=out_shape,
        compiler_params=_params(("arbitrary",)),
        name="mem_kv",
    )(*args)
    return outs[0], outs[1], list(outs[2:])


def _mem_attn_rows(q_ref, k_ref, v_ref):
    outs = []
    for h in range(MEM_HEADS):
        hs = slice(h * HEAD_DIM, (h + 1) * HEAD_DIM)
        s = lax.dot_general(q_ref[:, hs], k_ref[:, hs], (((1,), (1,)), ((), ())), preferred_element_type=F32)
        m = jnp.max(s, axis=-1, keepdims=True)
        p = jnp.exp(s - m)
        den = jnp.sum(p, axis=-1, keepdims=True)
        o = jnp.dot(p.astype(BF16), v_ref[:, hs], preferred_element_type=F32) / den
        outs.append(o.astype(BF16))
    return jnp.concatenate(outs, axis=-1)


def _mem_kv_specs(layer):
    spec = pl.BlockSpec((None, MEM_TOKENS, MEM_WIDTH), lambda i: (layer, 0, 0))
    return [spec, spec]


def _rglru_tile(u_ref, gate_ref, cw_ref, cb_ref, wg_ref, bg_ref, lam_ref, y_ref, us, hs, utail, hcar,
                *, T, LB, pitch, before_block=lambda c: None):
    nseg = SUBLANES
    seg = T // nseg
    lam = lam_ref[...]
    sp = jnp.maximum(-lam, 0.0) + jnp.log1p(jnp.exp(-jnp.abs(lam)))
    first_seg = lax.broadcasted_iota(jnp.int32, (nseg, LANES), 0) == 0

    for c in range(LB // LANES):
        before_block(c)
        cs = slice(c * LANES, (c + 1) * LANES)
        for s in range(nseg):
            us[c, s * pitch:s * pitch + seg, :] = u_ref[s * seg:(s + 1) * seg, cs]
        U = [us[c, pl.ds(p, nseg, stride=pitch), :] for p in range(seg)]

        before = {}
        for k in range(1, CONV_WIDTH):
            cur = pltpu.roll(U[seg - k], 1, axis=0)
            prev = pltpu.roll(utail[k - 1, :, cs], 1, axis=0)
            before[-k] = jnp.where(first_seg, prev, cur)
        for k in range(1, CONV_WIDTH):
            utail[k - 1, :, cs] = U[seg - k]
        tap = lambda p: U[p] if p >= 0 else before[p]

        cwb = [jnp.broadcast_to(cw_ref[k:k + 1, cs], (nseg, LANES)) for k in range(CONV_WIDTH)]
        cbb = jnp.broadcast_to(cb_ref[:, cs], (nseg, LANES))
        xc = jnp.stack([cbb + (tap(p - 3) * cwb[0] + tap(p - 2) * cwb[1] + tap(p - 1) * cwb[2] + U[p] * cwb[3])
                        for p in range(seg)])

        z = jnp.dot(xc.reshape(T, LANES).astype(BF16), wg_ref[c], preferred_element_type=F32) + bg_ref[c]
        r = jax.nn.sigmoid(z[:, :LANES]).reshape(seg, nseg, LANES)
        ig = jax.nn.sigmoid(z[:, LANES:]).reshape(seg, nseg, LANES)
        log_a = (-RG_C * r) * sp[:, cs]
        a = jnp.exp(log_a)
        om = 1.0 - a * a
        root = jnp.where(om > 0.0, om * lax.rsqrt(om), 0.0)
        b = root * (ig * xc)

        hl, pr = [b[0]], [a[0]]
        for p in range(1, seg):
            hl.append(a[p] * hl[-1] + b[p])
            pr.append(a[p] * pr[-1])

        h_in = [hcar[:, cs]]
        for s in range(nseg):
            h_in.append(hl[-1][s:s + 1, :] + pr[-1][s:s + 1, :] * h_in[-1])
        hcar[:, cs] = h_in[nseg]
        h_in = jnp.concatenate(h_in[:nseg], axis=0)

        for p in range(seg):
            hs[c, pl.ds(p, nseg, stride=pitch), :] = hl[p] + pr[p] * h_in

        for s in range(nseg):
            rows = slice(s * seg, (s + 1) * seg)
            gt = gate_ref[rows, cs]
            gelu = 0.5 * gt * (1.0 + jnp.tanh(math.sqrt(2.0 / math.pi) * (gt + 0.044715 * (gt * gt * gt))))
            y_ref[rows, cs] = (hs[c, s * pitch:s * pitch + seg, :] * gelu).astype(y_ref.dtype)


def _a_mix_kernel(*refs, T, pitch, has_cast):
    refs = list(refs)
    x_ref, gn_ref, w_ref, hg_ref, cw_ref, cb_ref, wg_ref, bg_ref, lam_ref = refs[:9]
    del refs[:9]
    cast_src = refs.pop(0) if has_cast else None
    y_ref, mq_ref = refs.pop(0), refs.pop(0)
    cast_dst = refs.pop(0) if has_cast else None
    hn_s, u_cur, g_cur, u_new, g_new, us, hs, utail, hcar = refs
    if has_cast:
        cast_dst[...] = cast_src[...].astype(BF16)
    i = pl.program_id(0)

    @pl.when(i <= 1)
    def _():
        utail[...] = jnp.zeros_like(utail)
        hcar[...] = jnp.zeros_like(hcar)

    @pl.when(i == 0)
    def _():
        u_cur[...] = jnp.zeros_like(u_cur)
        g_cur[...] = jnp.zeros_like(g_cur)

    hn_s[...] = _rms_rows(x_ref[...], gn_ref[...]).astype(BF16)
    nb = LRU_WIDTH // HEAD_BLOCK

    def project(jb):
        acc = jnp.dot(hn_s[...], w_ref[:, jb * HEAD_BLOCK:(jb + 1) * HEAD_BLOCK], preferred_element_type=F32)
        if jb < nb:
            u_new[:, jb * HEAD_BLOCK:(jb + 1) * HEAD_BLOCK] = acc
        elif jb < 2 * nb:
            g_new[:, (jb - nb) * HEAD_BLOCK:(jb - nb + 1) * HEAD_BLOCK] = acc
        else:
            mq_ref[...] = _head_norm_cols(acc, hg_ref[...], ATTN_SCALE).astype(mq_ref.dtype)

    n_proj = 2 * nb + 1
    lane_blocks = LRU_WIDTH // LANES
    at_block = {min(2 * jb, lane_blocks - 1): jb for jb in range(n_proj)}
    assert len(at_block) == n_proj

    def before_block(c):
        if c in at_block:
            project(at_block[c])

    _rglru_tile(u_cur, g_cur, cw_ref, cb_ref, wg_ref, bg_ref, lam_ref, y_ref, us, hs, utail, hcar,
                T=T, LB=LRU_WIDTH, pitch=pitch, before_block=before_block)

    u_cur[...] = u_new[...]
    g_cur[...] = g_new[...]


def _side_cast_specs(cast, n_steps):
    w, idx = cast
    R, C = w.shape[-2:]
    r = R // n_steps
    assert r * n_steps == R and r % 16 == 0
    step = lambda i: jnp.minimum(i, n_steps - 1)
    return (pl.BlockSpec((None, r, C), lambda i: (idx, step(i), 0)),
            pl.BlockSpec((r, C), lambda i: (step(i), 0)),
            jax.ShapeDtypeStruct((R, C), BF16))


def _a_mix(x, norm_gain, w_in, mq_gain, conv_w, conv_b, wg, bg, lam, name, T=512, cast=None):
    S, D = x.shape
    W = LRU_WIDTH
    nt = S // T
    seg = T // SUBLANES
    pitch = seg + SUBLANES if (seg // SUBLANES) % 2 == 0 else seg
    slab = pltpu.VMEM((W // LANES, SUBLANES * pitch, LANES), F32)
    const = lambda shape: pl.BlockSpec(shape, lambda i: (0,) * len(shape))
    cur = lambda i: jnp.minimum(i, nt - 1)
    prv = lambda i: jnp.maximum(i - 1, 0)
    kern = functools.partial(_a_mix_kernel, T=T, pitch=pitch, has_cast=cast is not None)
    in_specs = [
        pl.BlockSpec((T, D), lambda i: (cur(i), 0)),
        const((1, D)),
        pl.BlockSpec((D, A_IN_WIDTH), lambda i: (0, 0), pipeline_mode=pl.Buffered(1)),
        const((1, HEAD_DIM)),
        const((CONV_WIDTH, W)),
        const((1, W)),
        const((LRU_BLOCKS, LRU_BLOCK_WIDTH, 2 * LRU_BLOCK_WIDTH)),
        const((LRU_BLOCKS, 1, 2 * LRU_BLOCK_WIDTH)),
        const((1, W)),
    ]
    args = [x, norm_gain.reshape(1, D), w_in, mq_gain, conv_w, conv_b.reshape(1, W), wg, bg, lam.reshape(1, W)]
    out_specs = [pl.BlockSpec((T, W), lambda i: (prv(i), 0)),
                 pl.BlockSpec((T, MEM_WIDTH), lambda i: (cur(i), 0))]
    out_shape = [jax.ShapeDtypeStruct((S, W), BF16), jax.ShapeDtypeStruct((S, MEM_WIDTH), BF16)]
    if cast is not None:
        c_in, c_out, c_shape = _side_cast_specs(cast, nt)
        in_specs.append(c_in)
        args.append(cast[0])
        out_specs.append(c_out)
        out_shape.append(c_shape)
    return pl.pallas_call(
        kern,
        grid=(nt + 1,),
        in_specs=in_specs,
        out_specs=out_specs,
        out_shape=out_shape,
        scratch_shapes=[pltpu.VMEM((T, D), BF16)] + [pltpu.VMEM((T, W), F32)] * 4 + [
                        slab, slab, pltpu.VMEM((CONV_WIDTH - 1, SUBLANES, W), F32), pltpu.VMEM((1, W), F32)],
        compiler_params=_params(("arbitrary",)),
        name=name,
    )(*args)


def _cast_weight_once(w_ref, wb_ref):
    @pl.when(pl.program_id(0) == 0)
    def _():
        wb_ref[...] = w_ref[...].astype(BF16)


def _out_a_kernel(x_ref, y_ref, mq_ref, mk_ref, mv_ref, w_ref, *rest):
    acc = jnp.dot(y_ref[...], w_ref[0:LRU_WIDTH, :], preferred_element_type=F32)
    mem_out = _mem_attn_rows(mq_ref, mk_ref, mv_ref)
    acc += jnp.dot(mem_out, w_ref[LRU_WIDTH:LRU_WIDTH + MEM_WIDTH, :], preferred_element_type=F32)
    if len(rest) == 1:
        (o_ref,) = rest
    else:
        cast_src, o_ref, cast_dst = rest
        cast_dst[...] = cast_src[...].astype(BF16)
    o_ref[...] = x_ref[...] + acc


def _out_a(x, y, mq, mem_k, mem_v, layer, w, name, tm=1024, cast=None):
    S, D = x.shape
    K = LRU_WIDTH + MEM_WIDTH
    row = pl.BlockSpec((tm, D), lambda i: (i, 0))
    in_specs = [
        row,
        pl.BlockSpec((tm, LRU_WIDTH), lambda i: (i, 0)),
        pl.BlockSpec((tm, MEM_WIDTH), lambda i: (i, 0)),
        *_mem_kv_specs(layer),
        pl.BlockSpec((K, D), lambda i: (0, 0), pipeline_mode=pl.Buffered(1)),
    ]
    args = [x, y, mq, mem_k, mem_v, w]
    out_specs, out_shape = [row], [jax.ShapeDtypeStruct((S, D), F32)]
    if cast is not None:
        c_in, c_out, c_shape = _side_cast_specs(cast, S // tm)
        in_specs.append(c_in)
        args.append(cast[0])
        out_specs.append(c_out)
        out_shape.append(c_shape)
    outs = pl.pallas_call(
        _out_a_kernel,
        grid=(S // tm,),
        in_specs=in_specs,
        out_specs=out_specs,
        out_shape=out_shape,
        compiler_params=_params(("parallel",)),
        name=name,
    )(*args)
    return outs if cast is not None else outs[0]


def _bucket_tables():
    qi = np.arange(Q_BLOCK)[:, None]
    ki = np.arange(2 * Q_BLOCK)[None, :]
    u = qi + Q_BLOCK - ki
    tables, valids = [], []
    for window, dilation in DIL_PATTERNS:
        n = np.maximum(u * dilation, 0)
        nf = np.maximum(n, 1).astype(np.float32)
        large = REL_MAX_EXACT + (np.log(nf / np.float32(REL_MAX_EXACT))
                                 / np.float32(math.log(REL_MAX_DISTANCE / REL_MAX_EXACT))
                                 * np.float32(REL_BUCKETS - REL_MAX_EXACT)).astype(np.int32)
        large = np.minimum(large, REL_BUCKETS - 1)
        tables.append(np.where(n < REL_MAX_EXACT, n, large).astype(np.int32))
        valids.append(((u >= 0) & (u <= window // dilation)).astype(np.int32))
    return np.stack(tables), np.stack(valids)


def _bias_kernel(tab_ref, idx_ref, valid_ref, o_ref):
    g = pl.program_id(0)
    idx = idx_ref[...]
    valid = valid_ref[...] > 0
    for h in range(DIL_HEADS):
        acc = jnp.zeros(idx.shape, F32)
        for b in range(REL_BUCKETS):
            acc = jnp.where(idx == b, tab_ref[b, g * DIL_HEADS + h], acc)
        o_ref[h] = jnp.where(valid, acc, NEG_INF)


def _bias_expand(rel_bias):
    idx, valid = _bucket_tables()
    blk = pl.BlockSpec((None, Q_BLOCK, 2 * Q_BLOCK), lambda g: (g, 0, 0))
    return pl.pallas_call(
        _bias_kernel,
        grid=(DIL_GROUPS,),
        in_specs=[pl.BlockSpec(memory_space=pltpu.SMEM), blk, blk],
        out_specs=pl.BlockSpec((DIL_HEADS, Q_BLOCK, 2 * Q_BLOCK), lambda g: (g, 0, 0)),
        out_shape=jax.ShapeDtypeStruct((DIL_GROUPS * DIL_HEADS, Q_BLOCK, 2 * Q_BLOCK), F32),
        compiler_params=_params(("arbitrary",)),
        name="bias_expand",
    )(rel_bias, jnp.asarray(idx), jnp.asarray(valid))


def _dil_kernel(q_ref, kc_ref, kp_ref, vc_ref, vp_ref, b_ref, o_ref, l_ref, *, LQ, R):
    is_first = pl.program_id(1) == 0
    for rr in range(R):
        for sb in range(LQ // Q_BLOCK):
            rows = slice(sb * Q_BLOCK, (sb + 1) * Q_BLOCK)
            for h in range(DIL_HEADS):
                hs = slice(h * HEAD_DIM, (h + 1) * HEAD_DIM)
                q = q_ref[rr, rows, hs]
                if sb == 0:
                    kk = jnp.concatenate([kp_ref[rr, :, hs], kc_ref[rr, 0:Q_BLOCK, hs]], axis=0)
                    vv = jnp.concatenate([vp_ref[rr, :, hs], vc_ref[rr, 0:Q_BLOCK, hs]], axis=0)
                else:
                    kk = kc_ref[rr, (sb - 1) * Q_BLOCK:(sb + 1) * Q_BLOCK, hs]
                    vv = vc_ref[rr, (sb - 1) * Q_BLOCK:(sb + 1) * Q_BLOCK, hs]
                s = lax.dot_general(q, kk, (((1,), (1,)), ((), ())), preferred_element_type=F32) + b_ref[h]
                if sb == 0:
                    col = lax.broadcasted_iota(jnp.int32, s.shape, 1)
                    s = jnp.where(jnp.logical_and(is_first, col < Q_BLOCK), NEG_INF, s)
                m = jnp.max(s, axis=-1, keepdims=True)
                p = jnp.exp(s - m)
                den = jnp.sum(p, axis=-1, keepdims=True)
                o = jnp.dot(p.astype(BF16), vv, preferred_element_type=F32) / den
                o_ref[rr, rows, hs] = o.astype(o_ref.dtype)
                l_ref[rr, rows, hs] = jnp.broadcast_to(m + jnp.log(den), (Q_BLOCK, HEAD_DIM))


def _dil_attn(q, k, v, bias, g, name, rows_per_step=2048):
    d, L, _ = q.shape
    assert d == DIL_PATTERNS[g][1]
    LQ = min(rows_per_step, L)
    R = min(d, rows_per_step // LQ)
    ratio = LQ // Q_BLOCK
    kern = functools.partial(_dil_kernel, LQ=LQ, R=R)
    cur = pl.BlockSpec((R, LQ, HEAD_BLOCK), lambda r, n: (r, n, 0))
    prev = pl.BlockSpec((R, Q_BLOCK, HEAD_BLOCK), lambda r, n: (r, jnp.maximum(n * ratio - 1, 0), 0))
    return pl.pallas_call(
        kern,
        grid=(d // R, L // LQ),
        in_specs=[cur, cur, prev, cur, prev,
                  pl.BlockSpec((DIL_HEADS, Q_BLOCK, 2 * Q_BLOCK), lambda r, n: (g, 0, 0))],
        out_specs=[pl.BlockSpec((R, LQ, DIL_OUT_WIDTH), lambda r, n: (r, n, 0)),
                   pl.BlockSpec((R, LQ, DIL_OUT_WIDTH), lambda r, n: (r, n, 0))],
        out_shape=[jax.ShapeDtypeStruct((d, L, DIL_OUT_WIDTH), BF16),
                   jax.ShapeDtypeStruct((d, L, DIL_OUT_WIDTH), F32)],
        compiler_params=_params(("arbitrary", "arbitrary")),
        name=name,
    )(q, k, k, v, v, bias)


def _out_b_kernel(x_ref, o0_ref, l0_ref, o1_ref, l1_ref, o2_ref, l2_ref, mq_ref, mk_ref, mv_ref, w_ref, o_ref,
                  wb_ref, s1_ref, s2_ref, *, tm):
    _cast_weight_once(w_ref, wb_ref)
    mem_out = _mem_attn_rows(mq_ref, mk_ref, mv_ref)
    acc = jnp.dot(mem_out, wb_ref[DIL_OUT_WIDTH:DIL_OUT_WIDTH + MEM_WIDTH, :], preferred_element_type=F32)
    for og_ref, lg_ref, s_ref in ((o1_ref, l1_ref, s1_ref), (o2_ref, l2_ref, s2_ref)):
        d = og_ref.shape[0]
        for r in range(d):
            for c in range(DIL_HEADS):
                cl = slice(c * LANES, (c + 1) * LANES)
                s_ref[c, pl.ds(r, tm // d, stride=d), :] = og_ref[r, :, cl].astype(F32)
                s_ref[DIL_HEADS + c, pl.ds(r, tm // d, stride=d), :] = lg_ref[r, :, cl]
    parts = []
    for h in range(DIL_HEADS):
        hs = slice(h * HEAD_DIM, (h + 1) * HEAD_DIM)
        lses = [l0_ref[0, :, hs], s1_ref[DIL_HEADS + h], s2_ref[DIL_HEADS + h]]
        outs = [o0_ref[0, :, hs].astype(F32), s1_ref[h], s2_ref[h]]
        mx = jnp.maximum(jnp.maximum(lses[0], lses[1]), lses[2])
        es = [jnp.exp(l - mx) for l in lses]
        den = es[0] + es[1] + es[2]
        num = es[0] * outs[0] + es[1] * outs[1] + es[2] * outs[2]
        parts.append((num / den).astype(BF16))
    dil = jnp.concatenate(parts, axis=-1)
    acc += jnp.dot(dil, wb_ref[0:DIL_OUT_WIDTH, :], preferred_element_type=F32)
    o_ref[...] = x_ref[...] + acc


def _out_b(x, ogs, mq, mem_k, mem_v, layer, w, widx, name, tm=512):
    S, D = x.shape
    K = DIL_OUT_WIDTH + MEM_WIDTH
    og_specs, og_args = [], []
    for o, l in ogs:
        d = o.shape[0]
        og_specs += [pl.BlockSpec((d, tm // d, DIL_OUT_WIDTH), lambda i: (0, i, 0)),
                     pl.BlockSpec((d, tm // d, DIL_OUT_WIDTH), lambda i: (0, i, 0))]
        og_args += [o, l]
    slab = pltpu.VMEM((2 * DIL_HEADS, tm, LANES), F32)
    row = pl.BlockSpec((tm, D), lambda i: (i, 0))
    return pl.pallas_call(
        functools.partial(_out_b_kernel, tm=tm),
        grid=(S // tm,),
        in_specs=[
            row,
            *og_specs,
            pl.BlockSpec((tm, MEM_WIDTH), lambda i: (i, 0)),
            *_mem_kv_specs(layer),
            pl.BlockSpec((None, K, D), lambda i: (widx, 0, 0), pipeline_mode=pl.Buffered(1)),
        ],
        out_specs=row,
        out_shape=jax.ShapeDtypeStruct((S, D), F32),
        scratch_shapes=[pltpu.VMEM((K, D), BF16), slab, slab],
        compiler_params=_params(("arbitrary",)),
        name=name,
    )(x, *og_args, mq, mem_k, mem_v, w)


def kernel(x, mem, norm_mix_g, norm_mlp_g, mlp_w1, mlp_w2, mem_norm_g, mem_w_kv, mem_q_norm_g, mem_k_norm_g, a_w_in, a_conv_w, a_conv_b, a_gate_r_w, a_gate_r_b, a_gate_i_w, a_gate_i_b, a_lambda, a_w_out, kv_norm_g, kv_w, k_norm_g, rel_bias, b_w_q, b_q_norm_g, b_w_out):
    B, S, D = x.shape
    assert (B, S, D) == (1, SEQ, D_MODEL) and mem.shape == (1, MEM_TOKENS, D_MODEL)
    xs = x[0]
    ones = jnp.ones((1, HEAD_DIM), F32)

    mem_k, mem_v, (w_in_b, w_out_b) = _mem_kv(mem[0], mem_norm_g, mem_w_kv, mem_k_norm_g,
                                              [(a_w_in, 0), (a_w_out, 0)])
    bias = _bias_expand(rel_bias)
    w1b = w2b = None
    later = {}

    for layer in range(DEPTH):
        mq_gain = mem_q_norm_g[layer][None, :]
        a_in, a_gain = xs, norm_mix_g[layer]
        if layer < N_A_LAYERS:
            i = layer
            wg = jnp.concatenate([a_gate_r_w[i], a_gate_i_w[i]], axis=-1).astype(BF16)
            bg = jnp.concatenate([a_gate_r_b[i], a_gate_i_b[i]], axis=-1)[:, None, :]
            if layer == 0:
                y, mq, w1b = _a_mix(a_in, a_gain, w_in_b, mq_gain, a_conv_w[i], a_conv_b[i], wg, bg,
                                    a_lambda[i], f"a{i}_mix", cast=(mlp_w1, 0))
                xs, w2b = _out_a(xs, y, mq, mem_k, mem_v, layer, w_out_b, f"a{i}_out_proj", tm=512,
                                 cast=(mlp_w2, 0))
            else:
                y, mq = _a_mix(a_in, a_gain, w_in_b, mq_gain, a_conv_w[i], a_conv_b[i], wg, bg,
                               a_lambda[i], f"a{i}_mix")
                xs = _out_a(xs, y, mq, mem_k, mem_v, layer, w_out_b, f"a{i}_out_proj")
        else:
            j = layer - N_A_LAYERS
            dils = tuple(d for _, d in DIL_PATTERNS)
            dil_groups = tuple((1, d, BF16) for d in dils)
            q_gains = jnp.concatenate([b_q_norm_g[j], mq_gain], axis=0)
            q_part = (a_gain, later["w_q"], q_gains, dil_groups + ((1, 1, BF16),), (True,) * (DIL_GROUPS + 1),
                      ATTN_SCALE)
            if j == 0:
                kv_gains = jnp.concatenate([k_norm_g, jnp.tile(ones, (DIL_GROUPS, 1))], axis=0)
                kv_part = (kv_norm_g, later["kv_w"], kv_gains, dil_groups + dil_groups,
                           (True,) * DIL_GROUPS + (False,) * DIL_GROUPS, 1.0)
                kvs, qs = _proj(a_in, [kv_part, q_part], "b0_kv_q_proj", tm=512)
            else:
                (qs,) = _proj(a_in, [q_part], f"b{j}_q_proj", tm=1024)
            ogs = [_dil_attn(qs[g], kvs[g], kvs[DIL_GROUPS + g], bias, g, f"b{j}_dil{g}")
                   for g in range(DIL_GROUPS)]
            xs = _out_b(xs, ogs, qs[DIL_GROUPS].reshape(S, MEM_WIDTH), mem_k, mem_v, layer, b_w_out, j,
                        f"b{j}_out_proj")

        nl = layer + 1
        casts, names = [], []
        if nl < DEPTH:
            casts += [(mlp_w1, nl), (mlp_w2, nl)]
            names += ["w1", "w2"]
            if nl < N_A_LAYERS:
                casts += [(a_w_in, nl), (a_w_out, nl)]
                names += ["w_in", "w_out"]
            else:
                casts.append((b_w_q, nl - N_A_LAYERS))
                names.append("w_q")
                if nl == N_A_LAYERS:
                    casts.append((kv_w, None))
                    names.append("kv_w")
        xs, cast_out = _mlp(xs, norm_mlp_g[layer], w1b, w2b, casts, f"mlp{layer}")
        later = dict(zip(names, cast_out))
        w1b, w2b, w_in_b, w_out_b = (later.get(n) for n in ("w1", "w2", "w_in", "w_out"))
    return xs[None]
```
